```python
import math, functools
import jax, jax.numpy as jnp
from jax import lax
import numpy as np

D_MODEL = 1024
BATCH = 16
SEQ = 2048
DEPTH = 1
DEC_BATCH = 128
DEC_SEQ = 8
PAST_LEN = 8192
PAGE_SIZE = 128

DIL_PATTERNS = ((128, 1), (512, 4), (2048, 16))
N_DIL = 3
ATT_HEADS = 8
HEAD_DIM = 128
ATT_WIDTH = ATT_HEADS * HEAD_DIM
ATT_SCALE = HEAD_DIM ** -0.5
MAX_WINDOW = 2048
NUM_BUCKETS = 32
MAX_DISTANCE = 2048
D_INNER = 2 * D_MODEL
SSM_HEAD_DIM = 64
SSM_HEADS = D_INNER // SSM_HEAD_DIM
SSM_GROUPS = 4
D_STATE = 128
D_CONV = 4
CONV_DIM = D_INNER + 2 * SSM_GROUPS * D_STATE
SSD_CHUNK = 128
N_EXPERTS = 256
TOP_K = 8
N_EXPERT_GROUPS = 8
TOPK_GROUPS = 4
D_EXPERT = D_MODEL // 4
D_SHARED = D_EXPERT
ROUTE_SCALE = 2.5
EXPERT_BLOCK = 128
RMS_EPS = 1e-6
IN_SIZES = (N_DIL * ATT_WIDTH, ATT_WIDTH, ATT_WIDTH, D_INNER, CONV_DIM, SSM_HEADS, D_MODEL, D_MODEL)
IN_COLS = N_DIL * ATT_WIDTH + 2 * ATT_WIDTH + D_INNER + CONV_DIM + SSM_HEADS + 2 * D_MODEL

kernel_name = 'hybrid_dilated_ssd_moe_decoder_step'


def _rms(x, g):
    xf = x.astype(jnp.float32)
    y = xf * lax.rsqrt(jnp.mean(xf * xf, axis=-1, keepdims=True) + RMS_EPS)
    return (y * g.astype(jnp.float32)).astype(x.dtype)


def _split_cols(a, sizes):
    out, off = [], 0
    for s in sizes:
        out.append(a[..., off:off + s])
        off += s
    return out


def _t5_bucket(dist):
    max_exact = NUM_BUCKETS // 2
    far = max_exact + (jnp.log(jnp.maximum(dist, 1).astype(jnp.float32) / max_exact)
                       / math.log(MAX_DISTANCE / max_exact) * (NUM_BUCKETS - max_exact)).astype(jnp.int32)
    return jnp.where(dist < max_exact, dist, jnp.minimum(far, NUM_BUCKETS - 1))


def _softmax_stats(s):
    m = jnp.max(s, axis=-1, keepdims=True)
    p = jnp.exp(s - m)
    den = jnp.sum(p, axis=-1)
    return p, den, m[..., 0] + jnp.log(den)


def _dilated_band(q, k, v, bias_tab, dil, n_back):
    bsz, seqlen, nh, hd = q.shape
    sub_len = seqlen // dil
    w = n_back
    nb = -(-sub_len // w)
    lp = nb * w

    def sub(t):
        t = t.reshape(bsz, sub_len, dil, nh, hd).transpose(0, 2, 1, 3, 4)
        return jnp.pad(t, ((0, 0), (0, 0), (0, lp - sub_len), (0, 0), (0, 0)))

    def band(t):
        tp = jnp.pad(t, ((0, 0), (0, 0), (w, 0), (0, 0), (0, 0)))
        prev = tp[:, :, :lp].reshape(bsz, dil, nb, w, nh, hd)
        cur = t.reshape(bsz, dil, nb, w, nh, hd)
        return jnp.concatenate([prev, cur], axis=3)

    qb = sub(q).reshape(bsz, dil, nb, w, nh, hd)
    kb, vb = band(sub(k)), band(sub(v))
    qi = jnp.arange(w)[:, None]
    sj = jnp.arange(2 * w)[None, :]
    sub_dist = qi + w - sj
    in_band = (sub_dist >= 0) & (sub_dist <= w)
    key_pos = jnp.arange(nb)[:, None, None] * w + sj[None] - w
    mask = in_band[None] & (key_pos >= 0)
    bias = jnp.transpose(bias_tab[_t5_bucket(jnp.clip(sub_dist, 0, w) * dil)], (2, 0, 1)).astype(jnp.float32)
    s = jnp.einsum('bdnqhk,bdnshk->bdnhqs', qb, kb).astype(jnp.float32) * ATT_SCALE + bias
    s = jnp.where(mask[:, None], s, -jnp.inf)
    p, den, lse = _softmax_stats(s)
    o = jnp.einsum('bdnhqs,bdnshk->bdnqhk', p, vb.astype(jnp.float32)) / jnp.swapaxes(den, -1, -2)[..., None]
    o = o.reshape(bsz, dil, lp, nh, hd)[:, :, :sub_len].transpose(0, 2, 1, 3, 4).reshape(bsz, seqlen, nh, hd)
    lse = jnp.swapaxes(lse, -1, -2).reshape(bsz, dil, lp, nh)[:, :, :sub_len]
    lse = lse.transpose(0, 2, 1, 3).reshape(bsz, seqlen, nh)
    return o, lse


def _dilated_gather(q, k_all, v_all, bias_tab, dil, n_back, n_buf):
    t = q.shape[1]
    offs = jnp.arange(n_back + 1) * dil
    idx = n_buf + jnp.arange(t)[:, None] - offs[None, :]
    valid = idx >= 0
    idx = jnp.maximum(idx, 0)
    kg = k_all[:, idx]
    vg = v_all[:, idx]
    bias = bias_tab[_t5_bucket(offs)].T.astype(jnp.float32)
    s = jnp.einsum('bthd,btkhd->bthk', q, kg).astype(jnp.float32) * ATT_SCALE + bias
    s = jnp.where(valid[None, :, None, :], s, -jnp.inf)
    p, den, lse = _softmax_stats(s)
    o = jnp.einsum('bthk,btkhd->bthd', p, vg.astype(jnp.float32)) / den[..., None]
    return o, lse


def _merge_dilations(outs, lses):
    wts = jax.nn.softmax(jnp.stack(lses), axis=0)
    return jnp.einsum('gblh,gblhd->blhd', wts, jnp.stack(outs))


def _attn_prompt(q, k, v, rel_bias):
    outs, lses = [], []
    for g, (win, dil) in enumerate(DIL_PATTERNS):
        o, lse = _dilated_band(q[:, :, g], k, v, rel_bias[:, g * ATT_HEADS:(g + 1) * ATT_HEADS], dil, win // dil)
        outs.append(o)
        lses.append(lse)
    n_keep = min(MAX_WINDOW, k.shape[1])
    return _merge_dilations(outs, lses).astype(q.dtype), k[:, -n_keep:], v[:, -n_keep:]


def _attn_sample(q, k, v, buf_k, buf_v, rel_bias):
    n_buf = buf_k.shape[1]
    k_all = jnp.concatenate([buf_k.astype(k.dtype), k], axis=1)
    v_all = jnp.concatenate([buf_v.astype(v.dtype), v], axis=1)
    outs, lses = [], []
    for g, (win, dil) in enumerate(DIL_PATTERNS):
        o, lse = _dilated_gather(q[:, :, g], k_all, v_all, rel_bias[:, g * ATT_HEADS:(g + 1) * ATT_HEADS],
                                 dil, win // dil, n_buf)
        outs.append(o)
        lses.append(lse)
    return _merge_dilations(outs, lses).astype(q.dtype), k_all[:, -n_buf:], v_all[:, -n_buf:]


def _ssd(x, dt, a, bm, cm, h0):
    bsz, seqlen, nh, hp = x.shape
    ng, ns = bm.shape[2], bm.shape[3]
    nr = nh // ng
    ql = min(SSD_CHUNK, seqlen)
    nc = -(-seqlen // ql)
    pad = nc * ql - seqlen
    if pad:
        x = jnp.pad(x, ((0, 0), (0, pad), (0, 0), (0, 0)))
        dt = jnp.pad(dt, ((0, 0), (0, pad), (0, 0)))
        bm = jnp.pad(bm, ((0, 0), (0, pad), (0, 0), (0, 0)))
        cm = jnp.pad(cm, ((0, 0), (0, pad), (0, 0), (0, 0)))
    x = x.reshape(bsz, nc, ql, ng, nr, hp)
    dt = dt.reshape(bsz, nc, ql, ng, nr)
    bm = bm.reshape(bsz, nc, ql, ng, ns)
    cm = cm.reshape(bsz, nc, ql, ng, ns)
    acs = jnp.cumsum(dt * a.reshape(ng, nr), axis=2)
    tri = jnp.tril(jnp.ones((ql, ql), bool))[None, None, :, :, None, None]
    seg = acs[:, :, :, None] - acs[:, :, None, :]
    decay = jnp.exp(jnp.where(tri, seg, -jnp.inf))
    cb = jnp.einsum('bclgn,bcsgn->bclsg', cm, bm)
    mmat = cb[..., None] * decay * dt[:, :, None]
    y_diag = jnp.einsum('bclsgr,bcsgrp->bclgrp', mmat, x)
    w_state = jnp.exp(acs[:, :, -1:] - acs) * dt
    states = jnp.einsum('bclgn,bclgr,bclgrp->bcgrpn', bm, w_state, x)
    chunk_decay = jnp.exp(acs[:, :, -1])

    def step(h, inp):
        st, dec = inp
        return h * dec[..., None, None] + st, h

    h_last, h_in = lax.scan(step, h0.reshape(bsz, ng, nr, hp, ns),
                            (jnp.moveaxis(states, 1, 0), jnp.moveaxis(chunk_decay, 1, 0)))
    h_in = jnp.moveaxis(h_in, 0, 1)
    y_off = jnp.einsum('bclgn,bcgrpn,bclgr->bclgrp', cm, h_in, jnp.exp(acs))
    y = (y_diag + y_off).reshape(bsz, nc * ql, nh, hp)[:, :seqlen]
    return y, h_last.reshape(bsz, nh, hp, ns)


def _mamba2_branch(xbc, z, dt_raw, conv_buf, h0, lw):
    bsz, seqlen, _ = xbc.shape
    up = jnp.concatenate([conv_buf.astype(xbc.dtype), xbc], axis=1)
    conv = lax.conv_general_dilated(up, lw['conv_w'][:, None, :].astype(xbc.dtype), (1,), 'VALID',
                                    dimension_numbers=('NWC', 'WIO', 'NWC'), feature_group_count=CONV_DIM)
    u = jax.nn.silu(conv + lw['conv_b'])
    new_conv = up[:, -(D_CONV - 1):]
    xs, bs, cs = _split_cols(u, (D_INNER, SSM_GROUPS * D_STATE, SSM_GROUPS * D_STATE))
    dt = jax.nn.softplus(dt_raw.astype(jnp.float32) + lw['dt_bias'].astype(jnp.float32))
    a = -jnp.exp(lw['a_log'].astype(jnp.float32))
    xh = xs.reshape(bsz, seqlen, SSM_HEADS, SSM_HEAD_DIM).astype(jnp.float32)
    y, h_last = _ssd(xh, dt, a,
                     bs.reshape(bsz, seqlen, SSM_GROUPS, D_STATE).astype(jnp.float32),
                     cs.reshape(bsz, seqlen, SSM_GROUPS, D_STATE).astype(jnp.float32),
                     h0.astype(jnp.float32))
    y = y + lw['d_skip'].astype(jnp.float32)[:, None] * xh
    y = y.reshape(bsz, seqlen, D_INNER) * jax.nn.silu(z.astype(jnp.float32))
    yg = y.reshape(bsz, seqlen, SSM_GROUPS, D_INNER // SSM_GROUPS)
    yg = yg * lax.rsqrt(jnp.mean(yg * yg, axis=-1, keepdims=True) + RMS_EPS)
    y = yg.reshape(bsz, seqlen, D_INNER) * lw['g_ssm_norm'].astype(jnp.float32)
    return y.astype(xbc.dtype), new_conv, h_last.astype(xbc.dtype)


def _routed_experts(h, idx, w, wg, wu, wd):
    t, kk = idx.shape
    n_exp = wg.shape[0]
    n_assign = t * kk
    e_flat = idx.reshape(-1)
    tok_flat = jnp.arange(n_assign, dtype=jnp.int32) // kk
    order = jnp.argsort(e_flat)
    e_sorted = e_flat[order]
    tok_sorted = tok_flat[order]
    w_sorted = w.reshape(-1)[order]
    counts = jnp.bincount(e_flat, length=n_exp)
    padded = (counts + EXPERT_BLOCK - 1) // EXPERT_BLOCK * EXPERT_BLOCK
    start = jnp.cumsum(counts) - counts
    pad_end = jnp.cumsum(padded)
    pad_start = pad_end - padded
    dest = pad_start[e_sorted] + (jnp.arange(n_assign) - start[e_sorted])
    n_blocks = (n_assign + n_exp * (EXPERT_BLOCK - 1)) // EXPERT_BLOCK
    rows = n_blocks * EXPERT_BLOCK
    tok_buf = jnp.full((rows,), t, jnp.int32).at[dest].set(tok_sorted)
    w_buf = jnp.zeros((rows,), h.dtype).at[dest].set(w_sorted.astype(h.dtype))
    block_e = jnp.minimum(jnp.searchsorted(pad_end, jnp.arange(n_blocks) * EXPERT_BLOCK, side='right'), n_exp - 1)
    h_pad = jnp.concatenate([h, jnp.zeros((1, h.shape[1]), h.dtype)], axis=0)

    def one_block(args):
        tok, wt, e = args
        xb = h_pad[tok]
        act = jax.nn.silu(xb @ wg[e]) * (xb @ wu[e])
        return (act @ wd[e]) * wt[:, None]

    out = lax.map(one_block, (tok_buf.reshape(n_blocks, EXPERT_BLOCK), w_buf.reshape(n_blocks, EXPERT_BLOCK), block_e))
    y = jnp.zeros((t + 1, h.shape[1]), h.dtype).at[tok_buf].add(out.reshape(rows, h.shape[1]))
    return y[:t]


def _moe(h, lw):
    t = h.shape[0]
    scores = jax.nn.sigmoid((h @ lw['w_router']).astype(jnp.float32))
    biased = scores + lw['b_router'].astype(jnp.float32)
    grp = biased.reshape(t, N_EXPERT_GROUPS, N_EXPERTS // N_EXPERT_GROUPS)
    grp_score = lax.top_k(grp, 2)[0].sum(-1)
    _, top_grp = lax.top_k(grp_score, TOPK_GROUPS)
    grp_mask = (top_grp[..., None] == jnp.arange(N_EXPERT_GROUPS)).any(axis=1)
    masked = jnp.where(grp_mask[..., None], grp, -jnp.inf).reshape(t, N_EXPERTS)
    _, idx = lax.top_k(masked, TOP_K)
    w = jnp.take_along_axis(scores, idx, axis=1)
    w = w / jnp.sum(w, axis=-1, keepdims=True) * ROUTE_SCALE
    routed = _routed_experts(h, idx, w, lw['w_exp_gate'], lw['w_exp_up'], lw['w_exp_down'])
    shared = (jax.nn.silu(h @ lw['w_sh_gate']) * (h @ lw['w_sh_up'])) @ lw['w_sh_down']
    return routed + shared


def _layer(x, c, attend, conv_buf, ssm_h0, lw):
    bsz, seqlen, _ = x.shape
    mod = (jax.nn.silu(c) @ lw['w_mod'] + lw['b_mod'])[:, None, :]
    sh1, sc1, gt1, sh2, sc2, gt2 = jnp.split(mod, 6, axis=-1)
    h = _rms(x, lw['g_pre_mix']) * (1 + sc1) + sh1
    q, k, v, z, xbc, dt_raw, ga, gb = _split_cols(h @ lw['w_in'], IN_SIZES)
    q = q.reshape(bsz, seqlen, N_DIL, ATT_HEADS, HEAD_DIM)
    k = k.reshape(bsz, seqlen, ATT_HEADS, HEAD_DIM)
    v = v.reshape(bsz, seqlen, ATT_HEADS, HEAD_DIM)
    o_att, new_k, new_v = attend(q, k, v)
    y_ssm, new_conv, new_h = _mamba2_branch(xbc, z, dt_raw, conv_buf, ssm_h0, lw)
    ya = o_att.reshape(bsz, seqlen, ATT_WIDTH) @ lw['w_branch_a']
    yb = y_ssm @ lw['w_branch_b']
    mix = (jax.nn.sigmoid(ga) * ya + jax.nn.sigmoid(gb) * yb) @ lw['w_out']
    x = x + gt1 * _rms(mix, lw['g_post_mix'])
    h2 = _rms(x, lw['g_pre_ffn']) * (1 + sc2) + sh2
    f = _moe(h2.reshape(bsz * seqlen, D_MODEL), lw).reshape(bsz, seqlen, D_MODEL)
    x = x + gt2 * _rms(f, lw['g_post_ffn'])
    return x, new_k, new_v, new_conv, new_h


def setup_inputs(seed: int = 0) -> dict:
    key = jax.random.key(seed)
    ks = iter(jax.random.split(key, 48))
    f32 = jnp.float32

    def nrm(shape, scale):
        return jax.random.normal(next(ks), shape, f32) * scale

    def gain(shape):
        return 1.0 + nrm(shape, 0.02)

    win_buf = min(MAX_WINDOW, PAST_LEN)
    nl = DEPTH
    dt0 = jnp.exp(jax.random.uniform(next(ks), (nl, SSM_HEADS), f32, math.log(1e-3), math.log(1e-1)))
    dt_bias = dt0 + jnp.log(-jnp.expm1(-dt0))
    a_log = jnp.log(jax.random.uniform(next(ks), (nl, SSM_HEADS), f32, 1.0, 16.0))
    return {
        'x_prompt': nrm((BATCH, SEQ, D_MODEL), 1.0),
        'x_sample': nrm((DEC_BATCH, DEC_SEQ, D_MODEL), 1.0),
        'cache_win_k': nrm((nl, DEC_BATCH, win_buf, ATT_HEADS, HEAD_DIM), 1.0),
        'cache_win_v': nrm((nl, DEC_BATCH, win_buf, ATT_HEADS, HEAD_DIM), 1.0),
        'state_conv': nrm((nl, DEC_BATCH, D_CONV - 1, CONV_DIM), 1.0),
        'state_ssm': nrm((nl, DEC_BATCH, SSM_HEADS, SSM_HEAD_DIM, D_STATE), 0.5),
        'c_prompt': nrm((BATCH, D_MODEL), 1.0),
        'c_sample': nrm((DEC_BATCH, D_MODEL), 1.0),
        'rel_bias': nrm((NUM_BUCKETS, N_DIL * ATT_HEADS), 0.5),
        'w_mod': nrm((nl, D_MODEL, 6 * D_MODEL), 0.5 * D_MODEL ** -0.5),
        'b_mod': nrm((nl, 6 * D_MODEL), 0.1),
        'g_pre_mix': gain((nl, D_MODEL)),
        'g_post_mix': gain((nl, D_MODEL)),
        'g_pre_ffn': gain((nl, D_MODEL)),
        'g_post_ffn': gain((nl, D_MODEL)),
        'w_in': nrm((nl, D_MODEL, IN_COLS), D_MODEL ** -0.5),
        'conv_w': nrm((nl, D_CONV, CONV_DIM), D_CONV ** -0.5),
        'conv_b': nrm((nl, CONV_DIM), 0.01),
        'dt_bias': dt_bias,
        'a_log': a_log,
        'd_skip': 1.0 + nrm((nl, SSM_HEADS), 0.1),
        'g_ssm_norm': gain((nl, D_INNER)),
        'w_branch_a': nrm((nl, ATT_WIDTH, D_MODEL), ATT_WIDTH ** -0.5),
        'w_branch_b': nrm((nl, D_INNER, D_MODEL), D_INNER ** -0.5),
        'w_out': nrm((nl, D_MODEL, D_MODEL), D_MODEL ** -0.5),
        'w_router': nrm((nl, D_MODEL, N_EXPERTS), D_MODEL ** -0.5),
        'b_router': nrm((nl, N_EXPERTS), 0.01),
        'w_exp_gate': nrm((nl, N_EXPERTS, D_MODEL, D_EXPERT), D_MODEL ** -0.5),
        'w_exp_up': nrm((nl, N_EXPERTS, D_MODEL, D_EXPERT), D_MODEL ** -0.5),
        'w_exp_down': nrm((nl, N_EXPERTS, D_EXPERT, D_MODEL), D_EXPERT ** -0.5),
        'w_sh_gate': nrm((nl, D_MODEL, D_SHARED), D_MODEL ** -0.5),
        'w_sh_up': nrm((nl, D_MODEL, D_SHARED), D_MODEL ** -0.5),
        'w_sh_down': nrm((nl, D_SHARED, D_MODEL), D_SHARED ** -0.5),
    }


def reference(x_prompt, x_sample, cache_win_k, cache_win_v, state_conv, state_ssm, c_prompt, c_sample,
              rel_bias, w_mod, b_mod, g_pre_mix, g_post_mix, g_pre_ffn, g_post_ffn, w_in, conv_w, conv_b,
              dt_bias, a_log, d_skip, g_ssm_norm, w_branch_a, w_branch_b, w_out, w_router, b_router,
              w_exp_gate, w_exp_up, w_exp_down, w_sh_gate, w_sh_up, w_sh_down):
    y_p, y_s = x_prompt, x_sample
    pk, pv, pc, ph = [], [], [], []
    sk, sv, sc, sh = [], [], [], []
    for l in range(DEPTH):
        lw = {
            'w_mod': w_mod[l], 'b_mod': b_mod[l],
            'g_pre_mix': g_pre_mix[l], 'g_post_mix': g_post_mix[l],
            'g_pre_ffn': g_pre_ffn[l], 'g_post_ffn': g_post_ffn[l],
            'w_in': w_in[l], 'conv_w': conv_w[l], 'conv_b': conv_b[l],
            'dt_bias': dt_bias[l], 'a_log': a_log[l], 'd_skip': d_skip[l], 'g_ssm_norm': g_ssm_norm[l],
            'w_branch_a': w_branch_a[l], 'w_branch_b': w_branch_b[l], 'w_out': w_out[l],
            'w_router': w_router[l], 'b_router': b_router[l],
            'w_exp_gate': w_exp_gate[l], 'w_exp_up': w_exp_up[l], 'w_exp_down': w_exp_down[l],
            'w_sh_gate': w_sh_gate[l], 'w_sh_up': w_sh_up[l], 'w_sh_down': w_sh_down[l],
        }
        b_p = y_p.shape[0]
        conv0 = jnp.zeros((b_p, D_CONV - 1, CONV_DIM), y_p.dtype)
        h0 = jnp.zeros((b_p, SSM_HEADS, SSM_HEAD_DIM, D_STATE), y_p.dtype)
        y_p, k_new, v_new, c_new, h_new = _layer(
            y_p, c_prompt, functools.partial(_attn_prompt, rel_bias=rel_bias), conv0, h0, lw)
        pk.append(k_new); pv.append(v_new); pc.append(c_new); ph.append(h_new)
        y_s, k_new, v_new, c_new, h_new = _layer(
            y_s, c_sample,
            functools.partial(_attn_sample, buf_k=cache_win_k[l], buf_v=cache_win_v[l], rel_bias=rel_bias),
            state_conv[l], state_ssm[l], lw)
        sk.append(k_new); sv.append(v_new); sc.append(c_new); sh.append(h_new)
    return (y_p, y_s, jnp.stack(pk), jnp.stack(pv), jnp.stack(pc), jnp.stack(ph),
            jnp.stack(sk), jnp.stack(sv), jnp.stack(sc), jnp.stack(sh))
```

```python
import functools
import math

import jax
import jax.numpy as jnp
from jax import lax
from jax.experimental import pallas as pl
from jax.experimental.pallas import tpu as pltpu

F32 = jnp.float32
BF16 = jnp.bfloat16

DIL_PATTERNS = ((128, 1), (512, 4), (2048, 16))
N_DIL = len(DIL_PATTERNS)
ATT_HEADS = 8
HEAD_DIM = 128
ATT_WIDTH = ATT_HEADS * HEAD_DIM
ATT_SCALE = HEAD_DIM ** -0.5
MAX_WINDOW = 2048
NUM_BUCKETS = 32
MAX_DISTANCE = 2048
SSM_HEAD_DIM = 64
SSM_GROUPS = 4
D_STATE = 128
D_CONV = 4
SSD_CHUNK = 128
TOP_K = 8
N_EXPERT_GROUPS = 8
TOPK_GROUPS = 4
ROUTE_SCALE = 2.5
RMS_EPS = 1e-6

LANES = 128
SUBLANES = 8
VMEM_LIMIT = 56 * 1024 * 1024
EXPERT_ROWS = 128
NT_DIMS = (((1,), (1,)), ((), ()))


def _params(*sem):
    return pltpu.CompilerParams(dimension_semantics=sem, vmem_limit_bytes=VMEM_LIMIT)


def _silu(x):
    return x * jax.nn.sigmoid(x)


def _rms_rows(x, g):
    return x * lax.rsqrt(jnp.mean(x * x, axis=-1, keepdims=True) + RMS_EPS) * g


def _mod_kernel(c_ref, w_ref, b_ref, o_ref):
    s = _silu(c_ref[...]).astype(BF16)
    o_ref[...] = jnp.dot(s, w_ref[...].astype(BF16), preferred_element_type=F32) + b_ref[...]


def _modulation(c, w_mod, b_mod):
    m, d = c.shape
    n = w_mod.shape[1]
    tn = n // 4
    return pl.pallas_call(
        _mod_kernel,
        grid=(n // tn,),
        in_specs=[pl.BlockSpec((m, d), lambda j: (0, 0)),
                  pl.BlockSpec((d, tn), lambda j: (0, j)),
                  pl.BlockSpec((1, tn), lambda j: (0, j))],
        out_specs=pl.BlockSpec((m, tn), lambda j: (0, j)),
        out_shape=jax.ShapeDtypeStruct((m, n), F32),
        compiler_params=_params("arbitrary"),
        name="modulation",
    )(c, w_mod, b_mod.reshape(1, n))


class _Mod:
    def __init__(self, mod, seq, tm, d):
        self.d = d
        b = mod.shape[0]
        if seq % tm == 0:
            per = seq // tm
            self.arr = mod.reshape(b, 1, mod.shape[1])
            self._spec = lambda col: pl.BlockSpec((None, 1, d), lambda i: (i // per, 0, col))
        else:
            self.arr = jnp.repeat(mod, seq, axis=0)
            self._spec = lambda col: pl.BlockSpec((tm, d), lambda i: (i, col))

    def spec(self, col):
        return self._spec(col)


def _prenorm_kernel(x_ref, g_ref, sc_ref, sh_ref, o_ref):
    y = _rms_rows(x_ref[...], g_ref[...])
    o_ref[...] = (y * (1 + sc_ref[...]) + sh_ref[...]).astype(o_ref.dtype)


def _prenorm(x2, g, mod, col_scale, col_shift, tm):
    t, d = x2.shape
    return pl.pallas_call(
        _prenorm_kernel,
        grid=(t // tm,),
        in_specs=[pl.BlockSpec((tm, d), lambda i: (i, 0)),
                  pl.BlockSpec((1, d), lambda i: (0, 0)),
                  mod.spec(col_scale), mod.spec(col_shift)],
        out_specs=pl.BlockSpec((tm, d), lambda i: (i, 0)),
        out_shape=jax.ShapeDtypeStruct((t, d), BF16),
        compiler_params=_params("parallel"),
        name="prenorm",
    )(x2, g.reshape(1, d), mod.arr, mod.arr)


def _mm_kernel(h_ref, w_ref, o_ref):
    o_ref[...] = jnp.dot(h_ref[...], w_ref[...], preferred_element_type=F32).astype(o_ref.dtype)


def _matmul(h, w, out_dtype, tm, name):
    t, k = h.shape
    n = w.shape[1]
    return pl.pallas_call(
        _mm_kernel,
        grid=(t // tm,),
        in_specs=[pl.BlockSpec((tm, k), lambda i: (i, 0)),
                  pl.BlockSpec((k, n), lambda i: (0, 0))],
        out_specs=pl.BlockSpec((tm, n), lambda i: (i, 0)),
        out_shape=jax.ShapeDtypeStruct((t, n), out_dtype),
        compiler_params=_params("parallel"),
        name=name,
    )(h, w)


def _bucket(dist):
    max_exact = NUM_BUCKETS // 2
    far = max_exact + (jnp.log(jnp.maximum(dist, 1).astype(F32) / max_exact)
                       / math.log(MAX_DISTANCE / max_exact) * (NUM_BUCKETS - max_exact)).astype(jnp.int32)
    return jnp.where(dist < max_exact, dist, jnp.minimum(far, NUM_BUCKETS - 1))


def _band_bias(rel_bias, w):
    qi = jnp.arange(w)[:, None]
    sj = jnp.arange(2 * w)[None, :]
    sub_dist = jnp.clip(qi + w - sj, 0, w)
    out = []
    for g, (_, dil) in enumerate(DIL_PATTERNS):
        tab = rel_bias[:, g * ATT_HEADS:(g + 1) * ATT_HEADS]
        out.append(jnp.transpose(tab[_bucket(sub_dist * dil)], (2, 0, 1)))
    return jnp.concatenate(out, axis=0).astype(F32)


def _attn_prompt_kernel(q0_ref, q1_ref, q2_ref, k_ref, v_ref, b0_ref, b1_ref, b2_ref, o_ref,
                        og_scr, lse_scr, *, seq, w):
    q_refs = (q0_ref, q1_ref, q2_ref)
    b_refs = (b0_ref, b1_ref, b2_ref)
    qi2 = lax.broadcasted_iota(jnp.int32, (w, 2 * w), 0)
    sj2 = lax.broadcasted_iota(jnp.int32, (w, 2 * w), 1)
    dist2 = qi2 + w - sj2
    band2 = jnp.logical_and(dist2 >= 0, dist2 <= w)
    qi1 = lax.broadcasted_iota(jnp.int32, (w, w), 0)
    sj1 = lax.broadcasted_iota(jnp.int32, (w, w), 1)
    band1 = qi1 >= sj1

    def rows(start, size, dil):
        return pl.ds(start, size) if dil == 1 else pl.ds(start, size, stride=dil)

    for g, (_, dil) in enumerate(DIL_PATTERNS):
        nb = seq // dil // w
        for r in range(dil):
            for n in range(nb):
                q_rows = rows(r + dil * n * w, w, dil)
                qb = q_refs[g][q_rows, :].astype(BF16)
                if n == 0:
                    k_rows = rows(r, w, dil)
                    bias = b_refs[g][:, w:]
                    band = band1
                else:
                    k_rows = rows(r + dil * (n - 1) * w, 2 * w, dil)
                    bias = b_refs[g][...]
                    band = band2
                kb = k_ref[k_rows, :].astype(BF16)
                vb = v_ref[k_rows, :].astype(BF16)
                s = lax.dot_general(qb, kb, NT_DIMS, preferred_element_type=F32) * ATT_SCALE + bias
                s = jnp.where(band, s, -jnp.inf)
                m = jnp.max(s, axis=-1, keepdims=True)
                p = jnp.exp(s - m)
                den = jnp.sum(p, axis=-1, keepdims=True)
                o = jnp.dot(p.astype(BF16), vb, preferred_element_type=F32) / den
                og_scr[g, q_rows, :] = o
                lse_scr[g, q_rows, :] = jnp.broadcast_to(m + jnp.log(den), (w, LANES))

    step = 256
    for c in range(seq // step):
        sl = pl.ds(c * step, step)
        l0, l1, l2 = lse_scr[0, sl, :], lse_scr[1, sl, :], lse_scr[2, sl, :]
        mm = jnp.maximum(jnp.maximum(l0, l1), l2)
        e0, e1, e2 = jnp.exp(l0 - mm), jnp.exp(l1 - mm), jnp.exp(l2 - mm)
        tot = e0 + e1 + e2
        o = (e0 / tot) * og_scr[0, sl, :] + (e1 / tot) * og_scr[1, sl, :] + (e2 / tot) * og_scr[2, sl, :]
        o_ref[sl, :] = o.astype(o_ref.dtype)


def _attn_prompt(q, k, v, rel_bias):
    b, seq, _ = k.shape
    w = DIL_PATTERNS[0][0] // DIL_PATTERNS[0][1]
    for win, dil in DIL_PATTERNS:
        assert win // dil == w and seq % (dil * w) == 0
    bias = _band_bias(rel_bias, w)
    hd = HEAD_DIM

    def q_spec(g):
        return pl.BlockSpec((None, seq, hd), lambda i, h: (i, 0, g * ATT_HEADS + h))

    def b_spec(g):
        return pl.BlockSpec((None, w, 2 * w), lambda i, h: (g * ATT_HEADS + h, 0, 0))

    kv_spec = pl.BlockSpec((None, seq, hd), lambda i, h: (i, 0, h))
    return pl.pallas_call(
        functools.partial(_attn_prompt_kernel, seq=seq, w=w),
        grid=(b, ATT_HEADS),
        in_specs=[q_spec(0), q_spec(1), q_spec(2), kv_spec, kv_spec, b_spec(0), b_spec(1), b_spec(2)],
        out_specs=pl.BlockSpec((None, seq, hd), lambda i, h: (i, 0, h)),
        out_shape=jax.ShapeDtypeStruct((b, seq, ATT_WIDTH), BF16),
        scratch_shapes=[pltpu.VMEM((N_DIL, seq, hd), F32), pltpu.VMEM((N_DIL, seq, LANES), F32)],
        compiler_params=_params("parallel", "parallel"),
        name="attn_prompt",
    )(q, q, q, k, v, bias, bias, bias)


SAMPLE_HEADS_PER_STEP = 4


def _attn_sample_kernel(q0_ref, q1_ref, q2_ref, kn_ref, vn_ref, ck_ref, cv_ref, bm_ref, bx_ref,
                        o_ref, wk_ref, wv_ref, *, t, n_buf):
    wk_ref[0:n_buf - t, :] = ck_ref[t:n_buf, :]
    wk_ref[n_buf - t:n_buf, :] = kn_ref[...]
    wv_ref[0:n_buf - t, :] = cv_ref[t:n_buf, :]
    wv_ref[n_buf - t:n_buf, :] = vn_ref[...]
    for hl in range(SAMPLE_HEADS_PER_STEP):
        cs = slice(hl * HEAD_DIM, (hl + 1) * HEAD_DIM)
        qs = [r[:, cs] for r in (q0_ref, q1_ref, q2_ref)]
        qa = jnp.concatenate(qs, axis=0).astype(BF16)
        kw = wk_ref[:, cs].astype(BF16)
        vw = wv_ref[:, cs].astype(BF16)
        s = lax.dot_general(qa, kw, NT_DIMS, preferred_element_type=F32) * ATT_SCALE + bm_ref[hl]
        sx = jnp.sum(qs[2] * ck_ref[0:t, cs], axis=-1, keepdims=True) * ATT_SCALE + bx_ref[hl][:, 0:1]
        m = jnp.max(s, axis=-1, keepdims=True)
        m2 = jnp.maximum(m[2 * t:], sx)
        m = jnp.concatenate([m[:2 * t], m2], axis=0)
        p = jnp.exp(s - m)
        px = jnp.exp(sx - m2)
        den = jnp.sum(p, axis=-1, keepdims=True)
        den = jnp.concatenate([den[:2 * t], den[2 * t:] + px], axis=0)
        o = jnp.dot(p.astype(BF16), vw, preferred_element_type=F32)
        ox = px * cv_ref[0:t, cs]
        o = jnp.concatenate([o[:2 * t], o[2 * t:] + ox], axis=0) / den
        lse = m + jnp.log(den)
        l0, l1, l2 = lse[0:t], lse[t:2 * t], lse[2 * t:]
        mm = jnp.maximum(jnp.maximum(l0, l1), l2)
        e0, e1, e2 = jnp.exp(l0 - mm), jnp.exp(l1 - mm), jnp.exp(l2 - mm)
        tot = e0 + e1 + e2
        o_ref[:, cs] = (e0 / tot) * o[0:t] + (e1 / tot) * o[t:2 * t] + (e2 / tot) * o[2 * t:]


def _sample_bias(rel_bias, t, n_buf):
    tok = jnp.arange(t)[:, None]
    key = jnp.arange(n_buf)[None, :] + t
    dist = n_buf + tok - key
    main = []
    for g, (win, dil) in enumerate(DIL_PATTERNS):
        ok = (dist >= 0) & (dist % dil == 0) & (dist <= win)
        tab = rel_bias[:, g * ATT_HEADS:(g + 1) * ATT_HEADS]
        val = jnp.transpose(tab[_bucket(jnp.maximum(dist, 0))], (2, 0, 1))
        main.append(jnp.where(ok[None], val, -jnp.inf))
    main = jnp.concatenate(main, axis=1).astype(F32)
    g_last = N_DIL - 1
    far = rel_bias[_bucket(jnp.array([n_buf], jnp.int32))[0], g_last * ATT_HEADS:(g_last + 1) * ATT_HEADS]
    extra = jnp.broadcast_to(far.astype(F32)[:, None, None], (ATT_HEADS, t, LANES))
    return main, extra


def _attn_sample(q, k, v, cache_k, cache_v, rel_bias):
    b, t, _ = k.shape
    n_buf = cache_k.shape[1]
    win, dil = DIL_PATTERNS[-1]
    assert t == SUBLANES and n_buf == win and all(wn <= win for wn, _ in DIL_PATTERNS)
    assert (n_buf % dil) == 0 and t <= dil
    bm, bx = _sample_bias(rel_bias, t, n_buf)
    hs = SAMPLE_HEADS_PER_STEP
    cw = hs * HEAD_DIM
    steps = ATT_HEADS // hs

    def q_spec(g):
        return pl.BlockSpec((None, t, cw), lambda i, j: (i, 0, g * steps + j))

    new_spec = pl.BlockSpec((None, t, cw), lambda i, j: (i, 0, j))
    buf_spec = pl.BlockSpec((None, n_buf, cw), lambda i, j: (i, 0, j))
    return pl.pallas_call(
        functools.partial(_attn_sample_kernel, t=t, n_buf=n_buf),
        grid=(b, steps),
        in_specs=[q_spec(0), q_spec(1), q_spec(2), new_spec, new_spec, buf_spec, buf_spec,
                  pl.BlockSpec((hs, N_DIL * t, n_buf), lambda i, j: (j, 0, 0)),
                  pl.BlockSpec((hs, t, LANES), lambda i, j: (j, 0, 0))],
        out_specs=[new_spec, buf_spec, buf_spec],
        out_shape=[jax.ShapeDtypeStruct((b, t, ATT_WIDTH), F32),
                   jax.ShapeDtypeStruct((b, n_buf, ATT_WIDTH), F32),
                   jax.ShapeDtypeStruct((b, n_buf, ATT_WIDTH), F32)],
        compiler_params=_params("parallel", "parallel"),
        name="attn_sample",
    )(q, q, q, k, v, cache_k, cache_v, bm, bx)


def _softplus(x):
    return jnp.maximum(x, 0.0) + jnp.log(1.0 + jnp.exp(-jnp.abs(x)))


def _ssd_kernel(xbc_ref, z_ref, dt_ref, cbuf_ref, h0_ref, cw_ref, cb_ref, dtb_ref, alog_ref, dsk_ref, gn_ref,
                y_ref, hl_ref, ext_scr, st_scr, y_scr, acst_scr, dtt_scr, wstt_scr, *, lv, nc, d_inner):
    q = SSD_CHUNK
    n = D_STATE
    c = pl.program_id(1)
    n_pairs = d_inner // LANES
    pairs_per_group = n_pairs // SSM_GROUPS

    @pl.when(c == 0)
    def _():
        ext_scr[0:SUBLANES, :] = cbuf_ref[...]
        for i in range(n_pairs):
            st_scr[:, i * LANES:(i + 1) * LANES] = h0_ref[i * LANES:(i + 1) * LANES, :].T

    ext_scr[SUBLANES:SUBLANES + lv, :] = xbc_ref[...]
    if lv < q:
        ext_scr[SUBLANES + lv:SUBLANES + q, :] = jnp.zeros((q - lv, ext_scr.shape[1]), F32)
    first = SUBLANES - (D_CONV - 1)
    u = cb_ref[...] + cw_ref[0:1, :] * ext_scr[first:first + q, :]
    for j in range(1, D_CONV):
        u = u + cw_ref[j:j + 1, :] * ext_scr[first + j:first + j + q, :]
    u = _silu(u)
    if nc > 1:
        ext_scr[0:SUBLANES, :] = ext_scr[q:q + SUBLANES, :]

    dtv = _softplus(dt_ref_rows(dt_ref, lv, q) + dtb_ref[...])
    if lv < q:
        row = lax.broadcasted_iota(jnp.int32, (q, LANES), 0)
        dtv = jnp.where(row < lv, dtv, 0.0)
    a = -jnp.exp(alog_ref[...])
    da = dtv * a
    li = lax.broadcasted_iota(jnp.int32, (q, q), 0)
    si = lax.broadcasted_iota(jnp.int32, (q, q), 1)
    tri = li >= si
    acs = jnp.dot(tri.astype(F32), da, preferred_element_type=F32, precision=lax.Precision.HIGHEST)
    acs_last = acs[q - 1:q, :]
    acst_scr[...] = acs.T
    dtt_scr[...] = dtv.T
    wstt_scr[...] = (jnp.exp(acs_last - acs) * dtv).T

    lane = lax.broadcasted_iota(jnp.int32, (1, LANES), 1)
    left = lane < SSM_HEAD_DIM
    for g in range(SSM_GROUPS):
        bg = u[:, d_inner + g * n:d_inner + (g + 1) * n]
        cg = u[:, d_inner + (SSM_GROUPS + g) * n:d_inner + (SSM_GROUPS + g + 1) * n].astype(BF16)
        cb = lax.dot_general(cg, bg.astype(BF16), NT_DIMS, preferred_element_type=F32)
        bgt = bg.T
        for jp in range(pairs_per_group):
            pair = g * pairs_per_group + jp
            cols = slice(pair * LANES, (pair + 1) * LANES)
            x_pair = u[:, cols]
            lhs, e_col, dec = [], [], []
            for h in (2 * pair, 2 * pair + 1):
                a_col = jnp.broadcast_to(acs[:, h:h + 1], (q, q))
                seg = a_col - acst_scr[h:h + 1, :]
                m_h = cb * jnp.exp(jnp.where(tri, seg, -jnp.inf)) * dtt_scr[h:h + 1, :]
                lhs.append(m_h.astype(BF16))
                e_col.append(jnp.exp(a_col))
                dec.append(jnp.exp(a_col[q - 1:q, :]))
            for h in (2 * pair, 2 * pair + 1):
                lhs.append((bgt * wstt_scr[h:h + 1, :]).astype(BF16))
            res = jnp.dot(jnp.concatenate(lhs, axis=0), x_pair.astype(BF16), preferred_element_type=F32)
            y_diag = jnp.where(left, res[0:q], res[q:2 * q])
            d_state = jnp.where(left, res[2 * q:2 * q + n], res[2 * q + n:2 * q + 2 * n])
            st = st_scr[:, cols]
            y_off = jnp.dot(cg, st.astype(BF16), preferred_element_type=F32) * jnp.where(left, e_col[0], e_col[1])
            st_scr[:, cols] = st * jnp.where(left, dec[0], dec[1]) + d_state
            y_scr[:, cols] = y_diag + y_off + dsk_ref[:, cols] * x_pair

    gw = d_inner // SSM_GROUPS
    for g in range(SSM_GROUPS):
        cols = slice(g * gw, (g + 1) * gw)
        yg = y_scr[0:lv, cols] * _silu(z_ref[:, cols])
        yg = yg * lax.rsqrt(jnp.mean(yg * yg, axis=-1, keepdims=True) + RMS_EPS)
        y_ref[:, cols] = (yg * gn_ref[:, cols]).astype(y_ref.dtype)

    @pl.when(c == nc - 1)
    def _():
        for i in range(n_pairs):
            hl_ref[i * LANES:(i + 1) * LANES, :] = st_scr[:, i * LANES:(i + 1) * LANES].T


def dt_ref_rows(dt_ref, lv, q):
    if lv == q:
        return dt_ref[...]
    return jnp.concatenate([dt_ref[...], jnp.zeros((q - lv, dt_ref.shape[1]), F32)], axis=0)


def _ssd_branch(xbc, z, dt, conv_buf, h0, conv_w, conv_b, dt_bias, a_log, d_skip, g_norm, out_dtype):
    b, seqlen, conv_dim = xbc.shape
    d_inner = z.shape[2]
    heads = d_inner // SSM_HEAD_DIM
    q = SSD_CHUNK
    lv = min(q, seqlen)
    assert seqlen % lv == 0 and lv % SUBLANES == 0 and heads <= LANES
    nc = seqlen // lv
    cbuf = jnp.pad(conv_buf, ((0, 0), (SUBLANES - (D_CONV - 1), 0), (0, 0)))
    h0f = h0.reshape(b, heads * SSM_HEAD_DIM, D_STATE)
    pad = LANES - heads
    row = lambda a: jnp.pad(a.astype(F32), (0, pad)).reshape(1, LANES)
    dsk = jnp.repeat(d_skip.astype(F32), SSM_HEAD_DIM).reshape(1, d_inner)

    def per_chunk(width):
        return pl.BlockSpec((None, lv, width), lambda i, c: (i, c, 0))

    def per_batch(r, width):
        return pl.BlockSpec((None, r, width), lambda i, c: (i, 0, 0))

    def const(r, width):
        return pl.BlockSpec((r, width), lambda i, c: (0, 0))

    y, h_last = pl.pallas_call(
        functools.partial(_ssd_kernel, lv=lv, nc=nc, d_inner=d_inner),
        grid=(b, nc),
        in_specs=[per_chunk(conv_dim), per_chunk(d_inner), per_chunk(LANES),
                  per_batch(SUBLANES, conv_dim), per_batch(heads * SSM_HEAD_DIM, D_STATE),
                  const(D_CONV, conv_dim), const(1, conv_dim), const(1, LANES), const(1, LANES),
                  const(1, d_inner), const(1, d_inner)],
        out_specs=[per_chunk(d_inner), per_batch(heads * SSM_HEAD_DIM, D_STATE)],
        out_shape=[jax.ShapeDtypeStruct((b, seqlen, d_inner), out_dtype),
                   jax.ShapeDtypeStruct((b, heads * SSM_HEAD_DIM, D_STATE), F32)],
        scratch_shapes=[pltpu.VMEM((q + 2 * SUBLANES, conv_dim), F32),
                        pltpu.VMEM((D_STATE, d_inner), F32),
                        pltpu.VMEM((q, d_inner), F32),
                        pltpu.VMEM((LANES, q), F32), pltpu.VMEM((LANES, q), F32), pltpu.VMEM((LANES, q), F32)],
        compiler_params=_params("parallel", "arbitrary"),
        name="ssd",
    )(xbc, z, dt, cbuf, h0f, conv_w, conv_b.reshape(1, conv_dim), row(dt_bias), row(a_log), dsk,
      g_norm.reshape(1, d_inner))
    return y, h_last.reshape(b, heads, SSM_HEAD_DIM, D_STATE)


def _mix_kernel(oa_ref, ys_ref, gate_ref, x_ref, gt_ref, wa_ref, wb_ref, wo_ref, g_ref, o_ref, *, d):
    ya = jnp.dot(oa_ref[...].astype(BF16), wa_ref[...], preferred_element_type=F32)
    yb = jnp.dot(ys_ref[...].astype(BF16), wb_ref[...], preferred_element_type=F32)
    mixed = jax.nn.sigmoid(gate_ref[:, 0:d]) * ya + jax.nn.sigmoid(gate_ref[:, d:2 * d]) * yb
    mix = jnp.dot(mixed.astype(BF16), wo_ref[...], preferred_element_type=F32)
    o_ref[...] = x_ref[...] + gt_ref[...] * _rms_rows(mix, g_ref[...])


def _mix(o_att, y_ssm, gates, x2, mod, wa, wb, wo, g_post, tm):
    t, d = x2.shape
    full = lambda a: pl.BlockSpec(a.shape, lambda i: (0, 0))
    rows = lambda a: pl.BlockSpec((tm, a.shape[1]), lambda i: (i, 0))
    g2 = g_post.reshape(1, d)
    return pl.pallas_call(
        functools.partial(_mix_kernel, d=d),
        grid=(t // tm,),
        in_specs=[rows(o_att), rows(y_ssm), rows(gates), rows(x2), mod.spec(2), full(wa), full(wb), full(wo),
                  full(g2)],
        out_specs=pl.BlockSpec((tm, d), lambda i: (i, 0)),
        out_shape=jax.ShapeDtypeStruct((t, d), F32),
        compiler_params=_params("parallel"),
        name="mix",
    )(o_att, y_ssm, gates, x2, mod.arr, wa, wb, wo, g2)


def _ffn_pre_kernel(x_ref, g_ref, sc_ref, sh_ref, wr_ref, h_ref, s_ref):
    y = _rms_rows(x_ref[...], g_ref[...])
    h = (y * (1 + sc_ref[...]) + sh_ref[...]).astype(BF16)
    h_ref[...] = h
    s_ref[...] = jax.nn.sigmoid(jnp.dot(h, wr_ref[...], preferred_element_type=F32))


def _ffn_pre(x1, g, mod, w_router, tm):
    t, d = x1.shape
    e = w_router.shape[1]
    return pl.pallas_call(
        _ffn_pre_kernel,
        grid=(t // tm,),
        in_specs=[pl.BlockSpec((tm, d), lambda i: (i, 0)),
                  pl.BlockSpec((1, d), lambda i: (0, 0)),
                  mod.spec(4), mod.spec(3),
                  pl.BlockSpec((d, e), lambda i: (0, 0))],
        out_specs=[pl.BlockSpec((tm, d), lambda i: (i, 0)), pl.BlockSpec((tm, e), lambda i: (i, 0))],
        out_shape=[jax.ShapeDtypeStruct((t, d), BF16), jax.ShapeDtypeStruct((t, e), F32)],
        compiler_params=_params("parallel"),
        name="ffn_pre",
    )(x1, g.reshape(1, d), mod.arr, mod.arr, w_router)


def _route(scores, b_router):
    t, n_exp = scores.shape
    biased = scores + b_router.astype(F32)
    grp = biased.reshape(t, N_EXPERT_GROUPS, n_exp // N_EXPERT_GROUPS)
    grp_score = lax.top_k(grp, 2)[0].sum(-1)
    _, top_grp = lax.top_k(grp_score, TOPK_GROUPS)
    grp_mask = (top_grp[..., None] == jnp.arange(N_EXPERT_GROUPS)).any(axis=1)
    masked = jnp.where(grp_mask[..., None], grp, -jnp.inf).reshape(t, n_exp)
    _, idx = lax.top_k(masked, TOP_K)
    w = jnp.take_along_axis(scores, idx, axis=1)
    w = w / jnp.sum(w, axis=-1, keepdims=True) * ROUTE_SCALE
    return idx, w


def _experts_kernel(be_ref, nu_ref, x_ref, wt_ref, wg_ref, wu_ref, wd_ref, o_ref, wg_scr, wu_scr, wd_scr):
    i = pl.program_id(0)
    prev = be_ref[jnp.maximum(i - 1, 0)]
    fresh = jnp.logical_or(i == 0, be_ref[i] != prev)

    @pl.when(jnp.logical_and(fresh, i < nu_ref[0]))
    def _():
        wg_scr[...] = wg_ref[...].astype(BF16)
        wu_scr[...] = wu_ref[...].astype(BF16)
        wd_scr[...] = wd_ref[...].astype(BF16)

    @pl.when(i < nu_ref[0])
    def _():
        xb = x_ref[...]
        act = _silu(jnp.dot(xb, wg_scr[...], preferred_element_type=F32)) * jnp.dot(
            xb, wu_scr[...], preferred_element_type=F32)
        out = jnp.dot(act.astype(BF16), wd_scr[...], preferred_element_type=F32)
        o_ref[...] = out * wt_ref[...]

    @pl.when(i >= nu_ref[0])
    def _():
        o_ref[...] = jnp.zeros(o_ref.shape, o_ref.dtype)


def _routed_experts(h, idx, w, wg, wu, wd):
    t, kk = idx.shape
    d = h.shape[1]
    n_exp, _, de = wg.shape
    blk = EXPERT_ROWS
    n_assign = t * kk
    e_flat = idx.reshape(-1).astype(jnp.int32)
    order = jnp.argsort(e_flat)
    e_sorted = e_flat[order]
    tok_sorted = (order // kk).astype(jnp.int32)
    counts = jnp.bincount(e_flat, length=n_exp).astype(jnp.int32)
    padded = (counts + blk - 1) // blk * blk
    start = jnp.cumsum(counts) - counts
    pad_end = jnp.cumsum(padded)
    pad_start = pad_end - padded
    dest = pad_start[e_sorted] + (jnp.arange(n_assign, dtype=jnp.int32) - start[e_sorted])
    n_blocks = (n_assign + n_exp * (blk - 1)) // blk
    rows = n_blocks * blk
    tok_buf = jnp.full((rows,), t, jnp.int32).at[dest].set(tok_sorted)
    w_buf = jnp.zeros((rows,), F32).at[dest].set(w.reshape(-1)[order])
    block_e = jnp.minimum(jnp.searchsorted(pad_end, jnp.arange(n_blocks, dtype=jnp.int32) * blk, side='right'),
                          n_exp - 1).astype(jnp.int32)
    n_used = (pad_end[-1] // blk).astype(jnp.int32).reshape(1)
    h_pad = jnp.concatenate([h, jnp.zeros((1, d), h.dtype)], axis=0)
    x_buf = h_pad[tok_buf]
    grid_spec = pltpu.PrefetchScalarGridSpec(
        num_scalar_prefetch=2,
        grid=(n_blocks,),
        in_specs=[pl.BlockSpec((blk, d), lambda i, be, nu: (i, 0)),
                  pl.BlockSpec((blk, 1), lambda i, be, nu: (i, 0)),
                  pl.BlockSpec((None, d, de), lambda i, be, nu: (be[i], 0, 0)),
                  pl.BlockSpec((None, d, de), lambda i, be, nu: (be[i], 0, 0)),
                  pl.BlockSpec((None, de, d), lambda i, be, nu: (be[i], 0, 0))],
        out_specs=pl.BlockSpec((blk, d), lambda i, be, nu: (i, 0)),
        scratch_shapes=[pltpu.VMEM((d, de), BF16), pltpu.VMEM((d, de), BF16), pltpu.VMEM((de, d), BF16)],
    )
    out = pl.pallas_call(
        _experts_kernel,
        grid_spec=grid_spec,
        out_shape=jax.ShapeDtypeStruct((rows, d), F32),
        compiler_params=_params("arbitrary"),
        name="experts",
    )(block_e, n_used, x_buf, w_buf.reshape(rows, 1), wg, wu, wd)
    pos = jnp.zeros((n_assign,), jnp.int32).at[order].set(dest)
    return out[pos.reshape(t, kk)].sum(axis=1)


def _ffn_post_kernel(h_ref, r_ref, x_ref, gt_ref, wg_ref, wu_ref, wd_ref, g_ref, o_ref):
    h = h_ref[...]
    act = _silu(jnp.dot(h, wg_ref[...], preferred_element_type=F32)) * jnp.dot(
        h, wu_ref[...], preferred_element_type=F32)
    f = r_ref[...] + jnp.dot(act.astype(BF16), wd_ref[...], preferred_element_type=F32)
    o_ref[...] = x_ref[...] + gt_ref[...] * _rms_rows(f, g_ref[...])


def _ffn_post(h2, routed, x1, mod, wg, wu, wd, g_post, tm):
    t, d = x1.shape
    full = lambda a: pl.BlockSpec(a.shape, lambda i: (0, 0))
    rows = lambda a: pl.BlockSpec((tm, a.shape[1]), lambda i: (i, 0))
    g2 = g_post.reshape(1, d)
    return pl.pallas_call(
        _ffn_post_kernel,
        grid=(t // tm,),
        in_specs=[rows(h2), rows(routed), rows(x1), mod.spec(5), full(wg), full(wu), full(wd), full(g2)],
        out_specs=pl.BlockSpec((tm, d), lambda i: (i, 0)),
        out_shape=jax.ShapeDtypeStruct((t, d), F32),
        compiler_params=_params("parallel"),
        name="ffn_post",
    )(h2, routed, x1, mod.arr, wg, wu, wd, g2)


def _layer(x, mod_rows, attend, conv_buf, h0, lw, tm, ssm_dtype):
    b, seq, d = x.shape
    t = b * seq
    x2 = x.reshape(t, d)
    mod = _Mod(mod_rows, seq, tm, d)
    h = _prenorm(x2, lw['g_pre_mix'], mod, 1, 0, tm)
    proj = {name: _matmul(h, w, dt, tm, "proj_" + name) for name, (w, dt) in lw['w_in'].items()}
    q = proj['q'].reshape(b, seq, -1)
    k = proj['k'].reshape(b, seq, -1)
    v = proj['v'].reshape(b, seq, -1)
    o_att, new_k, new_v = attend(q, k, v)
    xbc = proj['xbc'].reshape(b, seq, -1)
    y_ssm, h_last = _ssd_branch(xbc, proj['z'].reshape(b, seq, -1), proj['dt'].reshape(b, seq, -1),
                                conv_buf, h0, lw['conv_w'], lw['conv_b'], lw['dt_bias'], lw['a_log'],
                                lw['d_skip'], lw['g_ssm_norm'], ssm_dtype)
    new_conv = jnp.concatenate([conv_buf, xbc], axis=1)[:, -(D_CONV - 1):] if seq < D_CONV - 1 \
        else xbc[:, seq - (D_CONV - 1):]
    x1 = _mix(o_att.reshape(t, -1), y_ssm.reshape(t, -1), proj['gate'], x2, mod,
              lw['w_branch_a'], lw['w_branch_b'], lw['w_out'], lw['g_post_mix'], tm)
    h2, scores = _ffn_pre(x1, lw['g_pre_ffn'], mod, lw['w_router'], tm)
    idx, wts = _route(scores, lw['b_router'])
    routed = _routed_experts(h2, idx, wts, lw['w_exp_gate'], lw['w_exp_up'], lw['w_exp_down'])
    out = _ffn_post(h2, routed, x1, mod, lw['w_sh_gate'], lw['w_sh_up'], lw['w_sh_down'], lw['g_post_ffn'], tm)
    return out.reshape(b, seq, d), new_k, new_v, new_conv, h_last


def _split_in_proj(w_in, d, d_inner, conv_dim, heads):
    sizes = (N_DIL * ATT_WIDTH, ATT_WIDTH, ATT_WIDTH, d_inner, conv_dim, heads, d, d)
    offs = [0]
    for s in sizes:
        offs.append(offs[-1] + s)
    part = lambda i, j=None: w_in[:, offs[i]:offs[(i if j is None else j) + 1]].astype(BF16)
    w_dt = jnp.pad(part(5), ((0, 0), (0, LANES - heads)))
    return {'q': (part(0), F32), 'k': (part(1), F32), 'v': (part(2), F32), 'z': (part(3), F32),
            'xbc': (part(4), F32), 'dt': (w_dt, F32), 'gate': (part(6, 7), F32)}


def kernel(x_prompt, x_sample, cache_win_k, cache_win_v, state_conv, state_ssm, c_prompt, c_sample, rel_bias, w_mod, b_mod, g_pre_mix, g_post_mix, g_pre_ffn, g_post_ffn, w_in, conv_w, conv_b, dt_bias, a_log, d_skip, g_ssm_norm, w_branch_a, w_branch_b, w_out, w_router, b_router, w_exp_gate, w_exp_up, w_exp_down, w_sh_gate, w_sh_up, w_sh_down):
    depth = w_mod.shape[0]
    bp, sp, d = x_prompt.shape
    bs, ss, _ = x_sample.shape
    d_inner = g_ssm_norm.shape[1]
    conv_dim = conv_w.shape[2]
    heads = dt_bias.shape[1]
    y_p, y_s = x_prompt, x_sample
    outs = [[] for _ in range(8)]
    for l in range(depth):
        lw = {
            'g_pre_mix': g_pre_mix[l], 'g_post_mix': g_post_mix[l],
            'g_pre_ffn': g_pre_ffn[l], 'g_post_ffn': g_post_ffn[l],
            'w_in': _split_in_proj(w_in[l], d, d_inner, conv_dim, heads),
            'conv_w': conv_w[l], 'conv_b': conv_b[l],
            'dt_bias': dt_bias[l], 'a_log': a_log[l], 'd_skip': d_skip[l], 'g_ssm_norm': g_ssm_norm[l],
            'w_branch_a': w_branch_a[l].astype(BF16), 'w_branch_b': w_branch_b[l].astype(BF16),
            'w_out': w_out[l].astype(BF16),
            'w_router': w_router[l].astype(BF16), 'b_router': b_router[l],
            'w_exp_gate': w_exp_gate[l], 'w_exp_up': w_exp_up[l], 'w_exp_down': w_exp_down[l],
            'w_sh_gate': w_sh_gate[l].astype(BF16), 'w_sh_up': w_sh_up[l].astype(BF16),
            'w_sh_down': w_sh_down[l].astype(BF16),
        }
        mod = _modulation(jnp.concatenate([c_prompt, c_sample], axis=0), w_mod[l], b_mod[l])
        conv0 = jnp.zeros((bp, D_CONV - 1, conv_dim), F32)
        h0 = jnp.zeros((bp, heads, SSM_HEAD_DIM, D_STATE), F32)
        y_p, k_new, v_new, c_new, h_new = _layer(
            y_p, mod[:bp], functools.partial(_prompt_attend, rel_bias=rel_bias), conv0, h0, lw, 512, BF16)
        for o, val in zip(outs[:4], (k_new, v_new, c_new, h_new)):
            o.append(val)
        y_s, k_new, v_new, c_new, h_new = _layer(
            y_s, mod[bp:],
            functools.partial(_sample_attend, buf_k=cache_win_k[l], buf_v=cache_win_v[l], rel_bias=rel_bias),
            state_conv[l], state_ssm[l], lw, 256, F32)
        for o, val in zip(outs[4:], (k_new, v_new, c_new, h_new)):
            o.append(val)
    return (y_p, y_s) + tuple(jnp.stack(o) for o in outs)


def _prompt_attend(q, k, v, rel_bias):
    b, seq, _ = k.shape
    o = _attn_prompt(q, k, v, rel_bias)
    n_keep = min(MAX_WINDOW, seq)
    shape = (b, n_keep, ATT_HEADS, HEAD_DIM)
    return o, k[:, seq - n_keep:].reshape(shape), v[:, seq - n_keep:].reshape(shape)


def _sample_attend(q, k, v, buf_k, buf_v, rel_bias):
    b, n_buf = buf_k.shape[0], buf_k.shape[1]
    o, win_k, win_v = _attn_sample(q, k, v, buf_k.reshape(b, n_buf, ATT_WIDTH), buf_v.reshape(b, n_buf, ATT_WIDTH),
                                   rel_bias)
    shape = (b, n_buf, ATT_HEADS, HEAD_DIM)
    return o, win_k.reshape(shape), win_v.reshape(shape)
```

```python
import functools
import math

import jax
import jax.numpy as jnp
from jax import lax
from jax.experimental import pallas as pl
from jax.experimental.pallas import tpu as pltpu

F32 = jnp.float32
BF16 = jnp.bfloat16

DIL_PATTERNS = ((128, 1), (512, 4), (2048, 16))
N_DIL = len(DIL_PATTERNS)
ATT_HEADS = 8
HEAD_DIM = 128
ATT_WIDTH = ATT_HEADS * HEAD_DIM
ATT_SCALE = HEAD_DIM ** -0.5
MAX_WINDOW = 2048
NUM_BUCKETS = 32
MAX_DISTANCE = 2048
SSM_HEAD_DIM = 64
SSM_GROUPS = 4
D_STATE = 128
D_CONV = 4
SSD_CHUNK = 128
TOP_K = 8
N_EXPERT_GROUPS = 8
TOPK_GROUPS = 4
ROUTE_SCALE = 2.5
RMS_EPS = 1e-6

LANES = 128
SUBLANES = 8
VMEM_LIMIT = 56 * 1024 * 1024
EXPERT_ROWS = 256
NT_DIMS = (((1,), (1,)), ((), ()))


def _params(*sem):
    return pltpu.CompilerParams(dimension_semantics=sem, vmem_limit_bytes=VMEM_LIMIT)


def _silu(x):
    return x * jax.nn.sigmoid(x)


def _rms_rows(x, g):
    return x * lax.rsqrt(jnp.mean(x * x, axis=-1, keepdims=True) + RMS_EPS) * g


def _mod_kernel(c_ref, w_ref, b_ref, o_ref):
    s = _silu(c_ref[...]).astype(BF16)
    o_ref[...] = jnp.dot(s, w_ref[...].astype(BF16), preferred_element_type=F32) + b_ref[...]


def _modulation(c, w_mod, b_mod):
    m, d = c.shape
    n = w_mod.shape[1]
    tn = n // 4
    return pl.pallas_call(
        _mod_kernel,
        grid=(n // tn,),
        in_specs=[pl.BlockSpec((m, d), lambda j: (0, 0)),
                  pl.BlockSpec((d, tn), lambda j: (0, j)),
                  pl.BlockSpec((1, tn), lambda j: (0, j))],
        out_specs=pl.BlockSpec((m, tn), lambda j: (0, j)),
        out_shape=jax.ShapeDtypeStruct((m, n), F32),
        compiler_params=_params("arbitrary"),
        name="modulation",
    )(c, w_mod, b_mod.reshape(1, n))


class _Mod:
    def __init__(self, mod, seq, tm, d):
        self.d = d
        b = mod.shape[0]
        if seq % tm == 0:
            per = seq // tm
            self.arr = mod.reshape(b, 1, mod.shape[1])
            self._spec = lambda col: pl.BlockSpec((None, 1, d), lambda i: (i // per, 0, col))
        else:
            self.arr = jnp.repeat(mod, seq, axis=0)
            self._spec = lambda col: pl.BlockSpec((tm, d), lambda i: (i, col))

    def spec(self, col):
        return self._spec(col)


def _prenorm_kernel(x_ref, g_ref, sc_ref, sh_ref, o_ref):
    y = _rms_rows(x_ref[...], g_ref[...])
    o_ref[...] = (y * (1 + sc_ref[...]) + sh_ref[...]).astype(o_ref.dtype)


def _prenorm(x2, g, mod, col_scale, col_shift, tm):
    t, d = x2.shape
    return pl.pallas_call(
        _prenorm_kernel,
        grid=(t // tm,),
        in_specs=[pl.BlockSpec((tm, d), lambda i: (i, 0)),
                  pl.BlockSpec((1, d), lambda i: (0, 0)),
                  mod.spec(col_scale), mod.spec(col_shift)],
        out_specs=pl.BlockSpec((tm, d), lambda i: (i, 0)),
        out_shape=jax.ShapeDtypeStruct((t, d), BF16),
        compiler_params=_params("parallel"),
        name="prenorm",
    )(x2, g.reshape(1, d), mod.arr, mod.arr)


def _mm_kernel(h_ref, w_ref, o_ref):
    o_ref[...] = jnp.dot(h_ref[...], w_ref[...], preferred_element_type=F32).astype(o_ref.dtype)


def _matmul(h, w, out_dtype, tm, name):
    t, k = h.shape
    n = w.shape[1]
    return pl.pallas_call(
        _mm_kernel,
        grid=(t // tm,),
        in_specs=[pl.BlockSpec((tm, k), lambda i: (i, 0)),
                  pl.BlockSpec((k, n), lambda i: (0, 0))],
        out_specs=pl.BlockSpec((tm, n), lambda i: (i, 0)),
        out_shape=jax.ShapeDtypeStruct((t, n), out_dtype),
        compiler_params=_params("parallel"),
        name=name,
    )(h, w)


def _bucket(dist):
    max_exact = NUM_BUCKETS // 2
    far = max_exact + (jnp.log(jnp.maximum(dist, 1).astype(F32) / max_exact)
                       / math.log(MAX_DISTANCE / max_exact) * (NUM_BUCKETS - max_exact)).astype(jnp.int32)
    return jnp.where(dist < max_exact, dist, jnp.minimum(far, NUM_BUCKETS - 1))


def _band_bias(rel_bias, w):
    qi = jnp.arange(w)[:, None]
    sj = jnp.arange(2 * w)[None, :]
    sub_dist = jnp.clip(qi + w - sj, 0, w)
    out = []
    for g, (_, dil) in enumerate(DIL_PATTERNS):
        tab = rel_bias[:, g * ATT_HEADS:(g + 1) * ATT_HEADS]
        out.append(jnp.transpose(tab[_bucket(sub_dist * dil)], (2, 0, 1)))
    return jnp.concatenate(out, axis=0).astype(F32)


def _attn_prompt_kernel(q0_ref, q1_ref, q2_ref, k_ref, v_ref, b0_ref, b1_ref, b2_ref, o_ref,
                        og_scr, lse_scr, *, seq, w):
    q_refs = (q0_ref, q1_ref, q2_ref)
    b_refs = (b0_ref, b1_ref, b2_ref)
    qi2 = lax.broadcasted_iota(jnp.int32, (w, 2 * w), 0)
    sj2 = lax.broadcasted_iota(jnp.int32, (w, 2 * w), 1)
    dist2 = qi2 + w - sj2
    band2 = jnp.logical_and(dist2 >= 0, dist2 <= w)
    qi1 = lax.broadcasted_iota(jnp.int32, (w, w), 0)
    sj1 = lax.broadcasted_iota(jnp.int32, (w, w), 1)
    band1 = qi1 >= sj1

    def rows(start, size, dil):
        return pl.ds(start, size) if dil == 1 else pl.ds(start, size, stride=dil)

    for g, (_, dil) in enumerate(DIL_PATTERNS):
        nb = seq // dil // w
        for r in range(dil):
            for n in range(nb):
                q_rows = rows(r + dil * n * w, w, dil)
                qb = q_refs[g][q_rows, :].astype(BF16)
                if n == 0:
                    k_rows = rows(r, w, dil)
                    bias = b_refs[g][:, w:]
                    band = band1
                else:
                    k_rows = rows(r + dil * (n - 1) * w, 2 * w, dil)
                    bias = b_refs[g][...]
                    band = band2
                kb = k_ref[k_rows, :].astype(BF16)
                vb = v_ref[k_rows, :].astype(BF16)
                s = lax.dot_general(qb, kb, NT_DIMS, preferred_element_type=F32) * ATT_SCALE + bias
                s = jnp.where(band, s, -jnp.inf)
                m = jnp.max(s, axis=-1, keepdims=True)
                p = jnp.exp(s - m)
                den = jnp.sum(p, axis=-1, keepdims=True)
                o = jnp.dot(p.astype(BF16), vb, preferred_element_type=F32) / den
                og_scr[g, q_rows, :] = o
                lse_scr[g, q_rows, :] = jnp.broadcast_to(m + jnp.log(den), (w, LANES))

    step = 256
    for c in range(seq // step):
        sl = pl.ds(c * step, step)
        l0, l1, l2 = lse_scr[0, sl, :], lse_scr[1, sl, :], lse_scr[2, sl, :]
        mm = jnp.maximum(jnp.maximum(l0, l1), l2)
        e0, e1, e2 = jnp.exp(l0 - mm), jnp.exp(l1 - mm), jnp.exp(l2 - mm)
        tot = e0 + e1 + e2
        o = (e0 / tot) * og_scr[0, sl, :] + (e1 / tot) * og_scr[1, sl, :] + (e2 / tot) * og_scr[2, sl, :]
        o_ref[sl, :] = o.astype(o_ref.dtype)


def _attn_prompt(q, k, v, rel_bias):
    b, seq, _ = k.shape
    w = DIL_PATTERNS[0][0] // DIL_PATTERNS[0][1]
    for win, dil in DIL_PATTERNS:
        assert win // dil == w and seq % (dil * w) == 0
    bias = _band_bias(rel_bias, w)
    hd = HEAD_DIM

    def q_spec(g):
        return pl.BlockSpec((None, seq, hd), lambda i, h: (i, 0, g * ATT_HEADS + h))

    def b_spec(g):
        return pl.BlockSpec((None, w, 2 * w), lambda i, h: (g * ATT_HEADS + h, 0, 0))

    kv_spec = pl.BlockSpec((None, seq, hd), lambda i, h: (i, 0, h))
    return pl.pallas_call(
        functools.partial(_attn_prompt_kernel, seq=seq, w=w),
        grid=(b, ATT_HEADS),
        in_specs=[q_spec(0), q_spec(1), q_spec(2), kv_spec, kv_spec, b_spec(0), b_spec(1), b_spec(2)],
        out_specs=pl.BlockSpec((None, seq, hd), lambda i, h: (i, 0, h)),
        out_shape=jax.ShapeDtypeStruct((b, seq, ATT_WIDTH), BF16),
        scratch_shapes=[pltpu.VMEM((N_DIL, seq, hd), F32), pltpu.VMEM((N_DIL, seq, LANES), F32)],
        compiler_params=_params("parallel", "parallel"),
        name="attn_prompt",
    )(q, q, q, k, v, bias, bias, bias)


SAMPLE_KEY_CHUNK = 512


def _attn_sample_kernel(q_ref, kn_ref, vn_ref, kn4_ref, vn4_ref, ck_ref, cv_ref, bm_ref, bn_ref,
                        o_ref, wk_ref, wv_ref, m_scr, l_scr, acc_scr, ck_scr, cv_scr, *, t, nch):
    c = pl.program_id(1)
    ch = ck_ref.shape[0]
    tr = t * ATT_HEADS
    rows = N_DIL * t

    @pl.when(c == 0)
    def _():
        m_scr[...] = jnp.full(m_scr.shape, -jnp.inf, F32)
        l_scr[...] = jnp.zeros(l_scr.shape, F32)
        acc_scr[...] = jnp.zeros(acc_scr.shape, F32)
        ck_scr[...] = kn4_ref[...]
        cv_scr[...] = vn4_ref[...]

    wk_ref[0:ch - tr, :] = ck_ref[tr:ch, :]
    wk_ref[ch - tr:ch, :] = ck_scr[...]
    ck_scr[...] = ck_ref[0:tr, :]
    wv_ref[0:ch - tr, :] = cv_ref[tr:ch, :]
    wv_ref[ch - tr:ch, :] = cv_scr[...]
    cv_scr[...] = cv_ref[0:tr, :]

    def head_q(h):
        return jnp.concatenate(
            [q_ref[:, g * ATT_WIDTH + h * HEAD_DIM:g * ATT_WIDTH + (h + 1) * HEAD_DIM] for g in range(N_DIL)], axis=0)

    for h in range(ATT_HEADS):
        qa = head_q(h).astype(BF16)
        head_rows = pl.ds(h, ch // ATT_HEADS, stride=ATT_HEADS)
        kh = ck_ref[head_rows, :].astype(BF16)
        vh = cv_ref[head_rows, :].astype(BF16)
        s = lax.dot_general(qa, kh, NT_DIMS, preferred_element_type=F32) * ATT_SCALE + bm_ref[h]
        m_old = m_scr[h]
        m_new = jnp.maximum(m_old, jnp.max(s, axis=-1, keepdims=True))
        alpha = jnp.exp(m_old - m_new)
        p = jnp.exp(s - m_new[:, 0:1])
        l_scr[h] = alpha * l_scr[h] + jnp.sum(p, axis=-1, keepdims=True)
        acc_scr[h] = alpha * acc_scr[h] + jnp.dot(p.astype(BF16), vh, preferred_element_type=F32)
        m_scr[h] = m_new

    @pl.when(c == nch - 1)
    def _():
        for h in range(ATT_HEADS):
            cs = slice(h * HEAD_DIM, (h + 1) * HEAD_DIM)
            qa = head_q(h)
            s_new = [jnp.sum(qa * kn_ref[j:j + 1, cs], axis=-1, keepdims=True) * ATT_SCALE + bn_ref[h, j]
                     for j in range(t)]
            m_old = m_scr[h]
            m_new = m_old
            for sj in s_new:
                m_new = jnp.maximum(m_new, sj)
            alpha = jnp.exp(m_old - m_new)
            den = alpha * l_scr[h]
            acc = alpha * acc_scr[h]
            for j, sj in enumerate(s_new):
                pj = jnp.exp(sj - m_new)
                den = den + pj
                acc = acc + pj * vn_ref[j:j + 1, cs]
            o = acc / den
            lse = m_new + jnp.log(den)
            l0, l1, l2 = lse[0:t], lse[t:2 * t], lse[2 * t:rows]
            mm = jnp.maximum(jnp.maximum(l0, l1), l2)
            e0, e1, e2 = jnp.exp(l0 - mm), jnp.exp(l1 - mm), jnp.exp(l2 - mm)
            tot = e0 + e1 + e2
            o_ref[:, cs] = (e0 / tot) * o[0:t] + (e1 / tot) * o[t:2 * t] + (e2 / tot) * o[2 * t:rows]


def _sample_bias(rel_bias, t, n_buf):
    tok = jnp.arange(t)[:, None]
    key = jnp.arange(n_buf + t)[None, :]
    dist = n_buf + tok - key
    full = []
    for g, (win, dil) in enumerate(DIL_PATTERNS):
        ok = (dist >= 0) & (dist % dil == 0) & (dist <= win)
        tab = rel_bias[:, g * ATT_HEADS:(g + 1) * ATT_HEADS]
        val = jnp.transpose(tab[_bucket(jnp.maximum(dist, 0))], (2, 0, 1))
        full.append(jnp.where(ok[None], val, -jnp.inf))
    full = jnp.concatenate(full, axis=1).astype(F32)
    new = jnp.transpose(full[:, :, n_buf:], (0, 2, 1))
    new = jnp.broadcast_to(new[..., None], new.shape + (LANES,))
    return full[:, :, :n_buf], new


def _attn_sample(q, k, v, cache_k, cache_v, rel_bias):
    b, t, _ = k.shape
    n_buf = cache_k.shape[1]
    ch = SAMPLE_KEY_CHUNK
    assert t == SUBLANES and n_buf % ch == 0
    assert all(win <= n_buf for win, _ in DIL_PATTERNS)
    nch = n_buf // ch
    bm, bn = _sample_bias(rel_bias, t, n_buf)
    flat = lambda a: a.reshape(b, -1, HEAD_DIM)
    rows = N_DIL * t
    tr = t * ATT_HEADS

    def per_b(*shape):
        return pl.BlockSpec((None,) + shape, lambda i, c: (i,) + (0,) * len(shape))

    buf_spec = pl.BlockSpec((None, ch * ATT_HEADS, HEAD_DIM), lambda i, c: (i, nch - 1 - c, 0))
    buf_shape = jax.ShapeDtypeStruct((b, n_buf * ATT_HEADS, HEAD_DIM), F32)
    o, win_k, win_v = pl.pallas_call(
        functools.partial(_attn_sample_kernel, t=t, nch=nch),
        grid=(b, nch),
        in_specs=[per_b(t, N_DIL * ATT_WIDTH), per_b(t, ATT_WIDTH), per_b(t, ATT_WIDTH),
                  per_b(tr, HEAD_DIM), per_b(tr, HEAD_DIM), buf_spec, buf_spec,
                  pl.BlockSpec((ATT_HEADS, rows, ch), lambda i, c: (0, 0, nch - 1 - c)),
                  pl.BlockSpec((ATT_HEADS, t, rows, LANES), lambda i, c: (0, 0, 0, 0))],
        out_specs=[per_b(t, ATT_WIDTH), buf_spec, buf_spec],
        out_shape=[jax.ShapeDtypeStruct((b, t, ATT_WIDTH), F32), buf_shape, buf_shape],
        scratch_shapes=[pltpu.VMEM((ATT_HEADS, rows, LANES), F32), pltpu.VMEM((ATT_HEADS, rows, LANES), F32),
                        pltpu.VMEM((ATT_HEADS, rows, HEAD_DIM), F32),
                        pltpu.VMEM((tr, HEAD_DIM), F32), pltpu.VMEM((tr, HEAD_DIM), F32)],
        compiler_params=_params("parallel", "arbitrary"),
        name="attn_sample",
    )(q, k, v, flat(k), flat(v), flat(cache_k), flat(cache_v), bm, bn)
    return o, win_k.reshape(cache_k.shape), win_v.reshape(cache_v.shape)


def _softplus(x):
    return jnp.maximum(x, 0.0) + jnp.log(1.0 + jnp.exp(-jnp.abs(x)))


def _ssd_kernel(xbc_ref, z_ref, dt_ref, cbuf_ref, h0_ref, cw_ref, cb_ref, dtb_ref, alog_ref, dsk_ref, gn_ref,
                y_ref, hl_ref, ext_scr, st_scr, y_scr, acst_scr, dtt_scr, wstt_scr, *, lv, nc, d_inner):
    q = SSD_CHUNK
    n = D_STATE
    c = pl.program_id(1)
    n_pairs = d_inner // LANES
    pairs_per_group = n_pairs // SSM_GROUPS

    @pl.when(c == 0)
    def _():
        ext_scr[0:SUBLANES, :] = cbuf_ref[...]
        for i in range(n_pairs):
            st_scr[:, i * LANES:(i + 1) * LANES] = h0_ref[i * LANES:(i + 1) * LANES, :].T

    ext_scr[SUBLANES:SUBLANES + lv, :] = xbc_ref[...]
    if lv < q:
        ext_scr[SUBLANES + lv:SUBLANES + q, :] = jnp.zeros((q - lv, ext_scr.shape[1]), F32)
    first = SUBLANES - (D_CONV - 1)
    u = cb_ref[...] + cw_ref[0:1, :] * ext_scr[first:first + q, :]
    for j in range(1, D_CONV):
        u = u + cw_ref[j:j + 1, :] * ext_scr[first + j:first + j + q, :]
    u = _silu(u)
    if nc > 1:
        ext_scr[0:SUBLANES, :] = ext_scr[q:q + SUBLANES, :]

    dtv = _softplus(dt_ref_rows(dt_ref, lv, q) + dtb_ref[...])
    if lv < q:
        row = lax.broadcasted_iota(jnp.int32, (q, LANES), 0)
        dtv = jnp.where(row < lv, dtv, 0.0)
    a = -jnp.exp(alog_ref[...])
    da = dtv * a
    li = lax.broadcasted_iota(jnp.int32, (q, q), 0)
    si = lax.broadcasted_iota(jnp.int32, (q, q), 1)
    tri = li >= si
    acs = jnp.dot(tri.astype(F32), da, preferred_element_type=F32, precision=lax.Precision.HIGHEST)
    acs_last = acs[q - 1:q, :]
    acst_scr[...] = acs.T
    dtt_scr[...] = dtv.T
    wstt_scr[...] = (jnp.exp(acs_last - acs) * dtv).T

    lane = lax.broadcasted_iota(jnp.int32, (1, LANES), 1)
    left = lane < SSM_HEAD_DIM
    for g in range(SSM_GROUPS):
        bg = u[:, d_inner + g * n:d_inner + (g + 1) * n]
        cg = u[:, d_inner + (SSM_GROUPS + g) * n:d_inner + (SSM_GROUPS + g + 1) * n].astype(BF16)
        cb = lax.dot_general(cg, bg.astype(BF16), NT_DIMS, preferred_element_type=F32)
        bgt = bg.T
        for jp in range(pairs_per_group):
            pair = g * pairs_per_group + jp
            cols = slice(pair * LANES, (pair + 1) * LANES)
            x_pair = u[:, cols]
            lhs, e_col, dec = [], [], []
            for h in (2 * pair, 2 * pair + 1):
                a_col = jnp.broadcast_to(acs[:, h:h + 1], (q, q))
                seg = a_col - acst_scr[h:h + 1, :]
                m_h = cb * jnp.exp(jnp.where(tri, seg, -jnp.inf)) * dtt_scr[h:h + 1, :]
                lhs.append(m_h.astype(BF16))
                e_col.append(jnp.exp(a_col))
                dec.append(jnp.exp(a_col[q - 1:q, :]))
            for h in (2 * pair, 2 * pair + 1):
                lhs.append((bgt * wstt_scr[h:h + 1, :]).astype(BF16))
            res = jnp.dot(jnp.concatenate(lhs, axis=0), x_pair.astype(BF16), preferred_element_type=F32)
            y_diag = jnp.where(left, res[0:q], res[q:2 * q])
            d_state = jnp.where(left, res[2 * q:2 * q + n], res[2 * q + n:2 * q + 2 * n])
            st = st_scr[:, cols]
            y_off = jnp.dot(cg, st.astype(BF16), preferred_element_type=F32) * jnp.where(left, e_col[0], e_col[1])
            st_scr[:, cols] = st * jnp.where(left, dec[0], dec[1]) + d_state
            y_scr[:, cols] = y_diag + y_off + dsk_ref[:, cols] * x_pair

    gw = d_inner // SSM_GROUPS
    for g in range(SSM_GROUPS):
        cols = slice(g * gw, (g + 1) * gw)
        yg = y_scr[0:lv, cols] * _silu(z_ref[:, cols])
        yg = yg * lax.rsqrt(jnp.mean(yg * yg, axis=-1, keepdims=True) + RMS_EPS)
        y_ref[:, cols] = (yg * gn_ref[:, cols]).astype(y_ref.dtype)

    @pl.when(c == nc - 1)
    def _():
        for i in range(n_pairs):
            hl_ref[i * LANES:(i + 1) * LANES, :] = st_scr[:, i * LANES:(i + 1) * LANES].T


def dt_ref_rows(dt_ref, lv, q):
    if lv == q:
        return dt_ref[...]
    return jnp.concatenate([dt_ref[...], jnp.zeros((q - lv, dt_ref.shape[1]), F32)], axis=0)


def _ssd_branch(xbc, z, dt, conv_buf, h0, conv_w, conv_b, dt_bias, a_log, d_skip, g_norm, out_dtype):
    b, seqlen, conv_dim = xbc.shape
    d_inner = z.shape[2]
    heads = d_inner // SSM_HEAD_DIM
    q = SSD_CHUNK
    lv = min(q, seqlen)
    assert seqlen % lv == 0 and lv % SUBLANES == 0 and heads <= LANES
    nc = seqlen // lv
    cbuf = jnp.pad(conv_buf, ((0, 0), (SUBLANES - (D_CONV - 1), 0), (0, 0)))
    h0f = h0.reshape(b, heads * SSM_HEAD_DIM, D_STATE)
    pad = LANES - heads
    row = lambda a: jnp.pad(a.astype(F32), (0, pad)).reshape(1, LANES)
    dsk = jnp.repeat(d_skip.astype(F32), SSM_HEAD_DIM).reshape(1, d_inner)

    def per_chunk(width):
        return pl.BlockSpec((None, lv, width), lambda i, c: (i, c, 0))

    def per_batch(r, width):
        return pl.BlockSpec((None, r, width), lambda i, c: (i, 0, 0))

    def const(r, width):
        return pl.BlockSpec((r, width), lambda i, c: (0, 0))

    y, h_last = pl.pallas_call(
        functools.partial(_ssd_kernel, lv=lv, nc=nc, d_inner=d_inner),
        grid=(b, nc),
        in_specs=[per_chunk(conv_dim), per_chunk(d_inner), per_chunk(LANES),
                  per_batch(SUBLANES, conv_dim), per_batch(heads * SSM_HEAD_DIM, D_STATE),
                  const(D_CONV, conv_dim), const(1, conv_dim), const(1, LANES), const(1, LANES),
                  const(1, d_inner), const(1, d_inner)],
        out_specs=[per_chunk(d_inner), per_batch(heads * SSM_HEAD_DIM, D_STATE)],
        out_shape=[jax.ShapeDtypeStruct((b, seqlen, d_inner), out_dtype),
                   jax.ShapeDtypeStruct((b, heads * SSM_HEAD_DIM, D_STATE), F32)],
        scratch_shapes=[pltpu.VMEM((q + 2 * SUBLANES, conv_dim), F32),
                        pltpu.VMEM((D_STATE, d_inner), F32),
                        pltpu.VMEM((q, d_inner), F32),
                        pltpu.VMEM((LANES, q), F32), pltpu.VMEM((LANES, q), F32), pltpu.VMEM((LANES, q), F32)],
        compiler_params=_params("parallel", "arbitrary"),
        name="ssd",
    )(xbc, z, dt, cbuf, h0f, conv_w, conv_b.reshape(1, conv_dim), row(dt_bias), row(a_log), dsk,
      g_norm.reshape(1, d_inner))
    return y, h_last.reshape(b, heads, SSM_HEAD_DIM, D_STATE)


def _mix_kernel(oa_ref, ys_ref, gate_ref, x_ref, gt_ref, wa_ref, wb_ref, wo_ref, g_ref, o_ref, *, d):
    ya = jnp.dot(oa_ref[...].astype(BF16), wa_ref[...], preferred_element_type=F32)
    yb = jnp.dot(ys_ref[...].astype(BF16), wb_ref[...], preferred_element_type=F32)
    mixed = jax.nn.sigmoid(gate_ref[:, 0:d]) * ya + jax.nn.sigmoid(gate_ref[:, d:2 * d]) * yb
    mix = jnp.dot(mixed.astype(BF16), wo_ref[...], preferred_element_type=F32)
    o_ref[...] = x_ref[...] + gt_ref[...] * _rms_rows(mix, g_ref[...])


def _mix(o_att, y_ssm, gates, x2, mod, wa, wb, wo, g_post, tm):
    t, d = x2.shape
    full = lambda a: pl.BlockSpec(a.shape, lambda i: (0, 0))
    rows = lambda a: pl.BlockSpec((tm, a.shape[1]), lambda i: (i, 0))
    g2 = g_post.reshape(1, d)
    return pl.pallas_call(
        functools.partial(_mix_kernel, d=d),
        grid=(t // tm,),
        in_specs=[rows(o_att), rows(y_ssm), rows(gates), rows(x2), mod.spec(2), full(wa), full(wb), full(wo),
                  full(g2)],
        out_specs=pl.BlockSpec((tm, d), lambda i: (i, 0)),
        out_shape=jax.ShapeDtypeStruct((t, d), F32),
        compiler_params=_params("parallel"),
        name="mix",
    )(o_att, y_ssm, gates, x2, mod.arr, wa, wb, wo, g2)


def _first_index(hit, iota, size):
    return jnp.min(jnp.where(hit, iota, size), axis=0, keepdims=True)


def _ffn_pre_kernel(x_ref, g_ref, sc_ref, sh_ref, wr_ref, br_ref, h_ref, idx_ref, w_ref, *, n_exp):
    y = _rms_rows(x_ref[...], g_ref[...])
    h = (y * (1 + sc_ref[...]) + sh_ref[...]).astype(BF16)
    h_ref[...] = h
    tm = h.shape[0]
    scores = jax.nn.sigmoid(lax.dot_general(wr_ref[...], h, NT_DIMS, preferred_element_type=F32))
    biased = scores + br_ref[...]
    gsz = n_exp // N_EXPERT_GROUPS
    riota = lax.broadcasted_iota(jnp.int32, (gsz, tm), 0)
    gs = []
    for g in range(N_EXPERT_GROUPS):
        xg = biased[g * gsz:(g + 1) * gsz, :]
        m1 = jnp.max(xg, axis=0, keepdims=True)
        i1 = _first_index(xg == m1, riota, gsz)
        m2 = jnp.max(jnp.where(riota == i1, -jnp.inf, xg), axis=0, keepdims=True)
        gs.append(m1 + m2)
    gs = jnp.concatenate(gs, axis=0)
    giota = lax.broadcasted_iota(jnp.int32, gs.shape, 0)
    keep = jnp.zeros(gs.shape, F32)
    for _ in range(TOPK_GROUPS):
        gi = _first_index(gs == jnp.max(gs, axis=0, keepdims=True), giota, N_EXPERT_GROUPS)
        sel = giota == gi
        keep = jnp.where(sel, 1.0, keep)
        gs = jnp.where(sel, -jnp.inf, gs)
    masked = jnp.concatenate(
        [jnp.where(keep[g:g + 1, :] > 0.5, biased[g * gsz:(g + 1) * gsz, :], -jnp.inf)
         for g in range(N_EXPERT_GROUPS)], axis=0)
    eiota = lax.broadcasted_iota(jnp.int32, (n_exp, tm), 0)
    idxs, ws = [], []
    for _ in range(TOP_K):
        ik = _first_index(masked == jnp.max(masked, axis=0, keepdims=True), eiota, n_exp)
        hit = eiota == ik
        idxs.append(ik)
        ws.append(jnp.sum(jnp.where(hit, scores, 0.0), axis=0, keepdims=True))
        masked = jnp.where(hit, -jnp.inf, masked)
    idx_ref[...] = jnp.concatenate(idxs, axis=0)
    w = jnp.concatenate(ws, axis=0)
    w_ref[...] = w / jnp.sum(w, axis=0, keepdims=True) * ROUTE_SCALE


def _ffn_pre(x1, g, mod, w_router_t, b_router, tm):
    t, d = x1.shape
    e = w_router_t.shape[0]
    br = jnp.broadcast_to(b_router.astype(F32)[:, None], (e, tm))
    return pl.pallas_call(
        functools.partial(_ffn_pre_kernel, n_exp=e),
        grid=(t // tm,),
        in_specs=[pl.BlockSpec((tm, d), lambda i: (i, 0)),
                  pl.BlockSpec((1, d), lambda i: (0, 0)),
                  mod.spec(4), mod.spec(3),
                  pl.BlockSpec((e, d), lambda i: (0, 0)),
                  pl.BlockSpec((e, tm), lambda i: (0, 0))],
        out_specs=[pl.BlockSpec((tm, d), lambda i: (i, 0)),
                   pl.BlockSpec((TOP_K, tm), lambda i: (0, i)),
                   pl.BlockSpec((TOP_K, tm), lambda i: (0, i))],
        out_shape=[jax.ShapeDtypeStruct((t, d), BF16), jax.ShapeDtypeStruct((TOP_K, t), jnp.int32),
                   jax.ShapeDtypeStruct((TOP_K, t), F32)],
        compiler_params=_params("parallel"),
        name="ffn_pre",
    )(x1, g.reshape(1, d), mod.arr, mod.arr, w_router_t, br)


def _rank_kernel(idx_ref, rank_ref, cnt_ref, base_scr, *, n_exp):
    i = pl.program_id(0)
    tm = idx_ref.shape[1]

    @pl.when(i == 0)
    def _():
        base_scr[...] = jnp.zeros(base_scr.shape, F32)

    eiota = lax.broadcasted_iota(jnp.int32, (n_exp, tm), 0)
    idx = idx_ref[...]
    hits = [eiota == idx[k:k + 1, :] for k in range(TOP_K)]
    multi = jnp.zeros((n_exp, tm), F32)
    for hit in hits:
        multi = multi + jnp.where(hit, 1.0, 0.0)
    multi = multi.astype(BF16)
    ti = lax.broadcasted_iota(jnp.int32, (tm, tm), 0)
    tj = lax.broadcasted_iota(jnp.int32, (tm, tm), 1)
    earlier = jnp.where(ti < tj, 1.0, 0.0).astype(BF16)
    before = jnp.dot(multi, earlier, preferred_element_type=F32) + base_scr[:, 0:1]
    ranks = [jnp.sum(jnp.where(hit, before, 0.0), axis=0, keepdims=True) for hit in hits]
    rank_ref[...] = jnp.concatenate(ranks, axis=0).astype(jnp.int32)
    base_scr[...] = base_scr[...] + jnp.dot(multi, jnp.ones((tm, LANES), BF16), preferred_element_type=F32)
    cnt_ref[...] = base_scr[...]


def _expert_ranks(idx_t, n_exp, tm):
    t = idx_t.shape[1]
    return pl.pallas_call(
        functools.partial(_rank_kernel, n_exp=n_exp),
        grid=(t // tm,),
        in_specs=[pl.BlockSpec((TOP_K, tm), lambda i: (0, i))],
        out_specs=[pl.BlockSpec((TOP_K, tm), lambda i: (0, i)), pl.BlockSpec((n_exp, LANES), lambda i: (0, 0))],
        out_shape=[jax.ShapeDtypeStruct((TOP_K, t), jnp.int32), jax.ShapeDtypeStruct((n_exp, LANES), F32)],
        scratch_shapes=[pltpu.VMEM((n_exp, LANES), F32)],
        compiler_params=_params("arbitrary"),
        name="expert_ranks",
    )(idx_t)


def _pos_kernel(idx_ref, rank_ref, start_ref, pos_ref, *, n_exp):
    tm = idx_ref.shape[1]
    eiota = lax.broadcasted_iota(jnp.int32, (n_exp, tm), 0)
    idx = idx_ref[...]
    start = start_ref[:, 0:1]
    offs = [jnp.sum(jnp.where(eiota == idx[k:k + 1, :], start, 0.0), axis=0, keepdims=True) for k in range(TOP_K)]
    pos_ref[...] = rank_ref[...] + jnp.concatenate(offs, axis=0).astype(jnp.int32)


def _positions(idx_t, rank_t, pad_start, tm):
    t = idx_t.shape[1]
    n_exp = pad_start.shape[0]
    start = jnp.broadcast_to(pad_start.astype(F32)[:, None], (n_exp, LANES))
    spec = pl.BlockSpec((TOP_K, tm), lambda i: (0, i))
    return pl.pallas_call(
        functools.partial(_pos_kernel, n_exp=n_exp),
        grid=(t // tm,),
        in_specs=[spec, spec, pl.BlockSpec((n_exp, LANES), lambda i: (0, 0))],
        out_specs=spec,
        out_shape=jax.ShapeDtypeStruct((TOP_K, t), jnp.int32),
        compiler_params=_params("parallel"),
        name="positions",
    )(idx_t, rank_t, start)


def _experts_kernel(be_ref, nu_ref, x_ref, wg_ref, wu_ref, wd_ref, o_ref, wg_scr, wu_scr, wd_scr):
    i = pl.program_id(0)
    prev = be_ref[jnp.maximum(i - 1, 0)]
    fresh = jnp.logical_or(i == 0, be_ref[i] != prev)

    @pl.when(jnp.logical_and(fresh, i < nu_ref[0]))
    def _():
        wg_scr[...] = wg_ref[...].astype(BF16)
        wu_scr[...] = wu_ref[...].astype(BF16)
        wd_scr[...] = wd_ref[...].astype(BF16)

    @pl.when(i < nu_ref[0])
    def _():
        xb = x_ref[...]
        act = _silu(jnp.dot(xb, wg_scr[...], preferred_element_type=F32)) * jnp.dot(
            xb, wu_scr[...], preferred_element_type=F32)
        o_ref[...] = jnp.dot(act.astype(BF16), wd_scr[...], preferred_element_type=F32).astype(o_ref.dtype)

    @pl.when(i >= nu_ref[0])
    def _():
        o_ref[...] = jnp.zeros(o_ref.shape, o_ref.dtype)


def _routed_experts(h, idx_t, wg, wu, wd):
    kk, t = idx_t.shape
    d = h.shape[1]
    n_exp, _, de = wg.shape
    blk = EXPERT_ROWS
    n_assign = t * kk
    tm = 512
    assert t % tm == 0
    rank_t, cnt = _expert_ranks(idx_t, n_exp, tm)
    counts = cnt[:, 0].astype(jnp.int32)
    padded = (counts + blk - 1) // blk * blk
    pad_end = jnp.cumsum(padded)
    pad_start = pad_end - padded
    pos_t = _positions(idx_t, rank_t, pad_start, tm)
    n_blocks = (n_assign + n_exp * (blk - 1)) // blk
    rows = n_blocks * blk
    tok_buf = jnp.full((rows,), t, jnp.int32).at[pos_t.reshape(-1)].set(
        jnp.tile(jnp.arange(t, dtype=jnp.int32), kk))
    block_e = jnp.minimum(jnp.searchsorted(pad_end, jnp.arange(n_blocks, dtype=jnp.int32) * blk, side='right'),
                          n_exp - 1).astype(jnp.int32)
    n_used = (pad_end[-1] // blk).astype(jnp.int32).reshape(1)
    h_pad = jnp.concatenate([h, jnp.zeros((1, d), h.dtype)], axis=0)
    x_buf = h_pad[tok_buf]
    grid_spec = pltpu.PrefetchScalarGridSpec(
        num_scalar_prefetch=2,
        grid=(n_blocks,),
        in_specs=[pl.BlockSpec((blk, d), lambda i, be, nu: (i, 0)),
                  pl.BlockSpec((None, d, de), lambda i, be, nu: (be[i], 0, 0)),
                  pl.BlockSpec((None, d, de), lambda i, be, nu: (be[i], 0, 0)),
                  pl.BlockSpec((None, de, d), lambda i, be, nu: (be[i], 0, 0))],
        out_specs=pl.BlockSpec((blk, d), lambda i, be, nu: (i, 0)),
        scratch_shapes=[pltpu.VMEM((d, de), BF16), pltpu.VMEM((d, de), BF16), pltpu.VMEM((de, d), BF16)],
    )
    out = pl.pallas_call(
        _experts_kernel,
        grid_spec=grid_spec,
        out_shape=jax.ShapeDtypeStruct((rows, d), BF16),
        compiler_params=_params("arbitrary"),
        name="experts",
    )(block_e, n_used, x_buf, wg, wu, wd)
    return jnp.take(out, pos_t, axis=0)


def _ffn_post_kernel(h_ref, r_ref, wt_ref, x_ref, gt_ref, wg_ref, wu_ref, wd_ref, g_ref, o_ref):
    h = h_ref[...]
    act = _silu(jnp.dot(h, wg_ref[...], preferred_element_type=F32)) * jnp.dot(
        h, wu_ref[...], preferred_element_type=F32)
    f = jnp.dot(act.astype(BF16), wd_ref[...], preferred_element_type=F32)
    for k in range(TOP_K):
        f = f + r_ref[k].astype(F32) * wt_ref[:, k:k + 1]
    o_ref[...] = x_ref[...] + gt_ref[...] * _rms_rows(f, g_ref[...])


def _ffn_post(h2, routed, w_rows, row0, x1, mod, wg, wu, wd, g_post, tm):
    t, d = x1.shape
    assert row0 % tm == 0
    blk0 = row0 // tm
    full = lambda a: pl.BlockSpec(a.shape, lambda i: (0, 0))
    rows = lambda a: pl.BlockSpec((tm, a.shape[1]), lambda i: (i, 0))
    g2 = g_post.reshape(1, d)
    return pl.pallas_call(
        _ffn_post_kernel,
        grid=(t // tm,),
        in_specs=[rows(h2),
                  pl.BlockSpec((TOP_K, tm, d), lambda i: (0, i + blk0, 0)),
                  pl.BlockSpec((tm, TOP_K), lambda i: (i + blk0, 0)),
                  rows(x1), mod.spec(5), full(wg), full(wu), full(wd), full(g2)],
        out_specs=pl.BlockSpec((tm, d), lambda i: (i, 0)),
        out_shape=jax.ShapeDtypeStruct((t, d), F32),
        compiler_params=_params("parallel"),
        name="ffn_post",
    )(h2, routed, w_rows, x1, mod.arr, wg, wu, wd, g2)


def _channel_mixer(x1s, mods, tms, lw):
    pre = [_ffn_pre(x1, lw['g_pre_ffn'], mod, lw['w_router_t'], lw['b_router'], tm)
           for x1, mod, tm in zip(x1s, mods, tms)]
    h2 = jnp.concatenate([p[0] for p in pre], axis=0)
    idx_t = jnp.concatenate([p[1] for p in pre], axis=1)
    w_rows = jnp.concatenate([p[2] for p in pre], axis=1).T
    routed = _routed_experts(h2, idx_t, lw['w_exp_gate'], lw['w_exp_up'], lw['w_exp_down'])
    outs, row0 = [], 0
    for x1, mod, tm, p in zip(x1s, mods, tms, pre):
        outs.append(_ffn_post(p[0], routed, w_rows, row0, x1, mod, lw['w_sh_gate'], lw['w_sh_up'],
                              lw['w_sh_down'], lw['g_post_ffn'], tm))
        row0 += x1.shape[0]
    return outs


def _token_mixer(x, mod_rows, attend, conv_buf, h0, lw, tm, ssm_dtype):
    b, seq, d = x.shape
    t = b * seq
    x2 = x.reshape(t, d)
    mod = _Mod(mod_rows, seq, tm, d)
    h = _prenorm(x2, lw['g_pre_mix'], mod, 1, 0, tm)
    proj = {name: _matmul(h, w, dt, tm, "proj_" + name) for name, (w, dt) in lw['w_in'].items()}
    q = proj['q'].reshape(b, seq, -1)
    k = proj['k'].reshape(b, seq, -1)
    v = proj['v'].reshape(b, seq, -1)
    o_att, new_k, new_v = attend(q, k, v)
    xbc = proj['xbc'].reshape(b, seq, -1)
    y_ssm, h_last = _ssd_branch(xbc, proj['z'].reshape(b, seq, -1), proj['dt'].reshape(b, seq, -1),
                                conv_buf, h0, lw['conv_w'], lw['conv_b'], lw['dt_bias'], lw['a_log'],
                                lw['d_skip'], lw['g_ssm_norm'], ssm_dtype)
    new_conv = jnp.concatenate([conv_buf, xbc], axis=1)[:, -(D_CONV - 1):] if seq < D_CONV - 1 \
        else xbc[:, seq - (D_CONV - 1):]
    x1 = _mix(o_att.reshape(t, -1), y_ssm.reshape(t, -1), proj['gate'], x2, mod,
              lw['w_branch_a'], lw['w_branch_b'], lw['w_out'], lw['g_post_mix'], tm)
    return x1, mod, new_k, new_v, new_conv, h_last


def _split_in_proj(w_in, d, d_inner, conv_dim, heads):
    sizes = (N_DIL * ATT_WIDTH, ATT_WIDTH, ATT_WIDTH, d_inner, conv_dim, heads, d, d)
    offs = [0]
    for s in sizes:
        offs.append(offs[-1] + s)
    part = lambda i, j=None: w_in[:, offs[i]:offs[(i if j is None else j) + 1]].astype(BF16)
    w_dt = jnp.pad(part(5), ((0, 0), (0, LANES - heads)))
    return {'q': (part(0), F32), 'k': (part(1), F32), 'v': (part(2), F32), 'z': (part(3), F32),
            'xbc': (part(4), F32), 'dt': (w_dt, F32), 'gate': (part(6, 7), F32)}


def kernel(x_prompt, x_sample, cache_win_k, cache_win_v, state_conv, state_ssm, c_prompt, c_sample, rel_bias, w_mod, b_mod, g_pre_mix, g_post_mix, g_pre_ffn, g_post_ffn, w_in, conv_w, conv_b, dt_bias, a_log, d_skip, g_ssm_norm, w_branch_a, w_branch_b, w_out, w_router, b_router, w_exp_gate, w_exp_up, w_exp_down, w_sh_gate, w_sh_up, w_sh_down):
    depth = w_mod.shape[0]
    bp, sp, d = x_prompt.shape
    bs, ss, _ = x_sample.shape
    d_inner = g_ssm_norm.shape[1]
    conv_dim = conv_w.shape[2]
    heads = dt_bias.shape[1]
    y_p, y_s = x_prompt, x_sample
    outs = [[] for _ in range(8)]
    for l in range(depth):
        lw = {
            'g_pre_mix': g_pre_mix[l], 'g_post_mix': g_post_mix[l],
            'g_pre_ffn': g_pre_ffn[l], 'g_post_ffn': g_post_ffn[l],
            'w_in': _split_in_proj(w_in[l], d, d_inner, conv_dim, heads),
            'conv_w': conv_w[l], 'conv_b': conv_b[l],
            'dt_bias': dt_bias[l], 'a_log': a_log[l], 'd_skip': d_skip[l], 'g_ssm_norm': g_ssm_norm[l],
            'w_branch_a': w_branch_a[l].astype(BF16), 'w_branch_b': w_branch_b[l].astype(BF16),
            'w_out': w_out[l].astype(BF16),
            'w_router_t': w_router[l].T.astype(BF16), 'b_router': b_router[l],
            'w_exp_gate': w_exp_gate[l], 'w_exp_up': w_exp_up[l], 'w_exp_down': w_exp_down[l],
            'w_sh_gate': w_sh_gate[l].astype(BF16), 'w_sh_up': w_sh_up[l].astype(BF16),
            'w_sh_down': w_sh_down[l].astype(BF16),
        }
        mod = _modulation(jnp.concatenate([c_prompt, c_sample], axis=0), w_mod[l], b_mod[l])
        conv0 = jnp.zeros((bp, D_CONV - 1, conv_dim), F32)
        h0 = jnp.zeros((bp, heads, SSM_HEAD_DIM, D_STATE), F32)
        tms = (512, 256)
        x1_p, mod_p, *state_p = _token_mixer(
            y_p, mod[:bp], functools.partial(_prompt_attend, rel_bias=rel_bias), conv0, h0, lw, tms[0], BF16)
        x1_s, mod_s, *state_s = _token_mixer(
            y_s, mod[bp:],
            functools.partial(_sample_attend, buf_k=cache_win_k[l], buf_v=cache_win_v[l], rel_bias=rel_bias),
            state_conv[l], state_ssm[l], lw, tms[1], F32)
        for o, val in zip(outs, state_p + state_s):
            o.append(val)
        y_p, y_s = _channel_mixer((x1_p, x1_s), (mod_p, mod_s), tms, lw)
        y_p = y_p.reshape(bp, sp, d)
        y_s = y_s.reshape(bs, ss, d)
    return (y_p, y_s) + tuple(jnp.stack(o) for o in outs)


def _prompt_attend(q, k, v, rel_bias):
    b, seq, _ = k.shape
    o = _attn_prompt(q, k, v, rel_bias)
    n_keep = min(MAX_WINDOW, seq)
    shape = (b, n_keep, ATT_HEADS, HEAD_DIM)
    return o, k[:, seq - n_keep:].reshape(shape), v[:, seq - n_keep:].reshape(shape)


def _sample_attend(q, k, v, buf_k, buf_v, rel_bias):
    return _attn_sample(q, k, v, buf_k, buf_v, rel_bias)
```

```python
import functools
import math

import jax
import jax.numpy as jnp
from jax import lax
from jax.experimental import pallas as pl
from jax.experimental.pallas import tpu as pltpu
from jax.experimental.pallas import tpu_sc as plsc

F32 = jnp.float32
BF16 = jnp.bfloat16

DIL_PATTERNS = ((128, 1), (512, 4), (2048, 16))
N_DIL = len(DIL_PATTERNS)
ATT_HEADS = 8
HEAD_DIM = 128
ATT_WIDTH = ATT_HEADS * HEAD_DIM
ATT_SCALE = HEAD_DIM ** -0.5
MAX_WINDOW = 2048
NUM_BUCKETS = 32
MAX_DISTANCE = 2048
SSM_HEAD_DIM = 64
SSM_GROUPS = 4
D_STATE = 128
D_CONV = 4
SSD_CHUNK = 128
TOP_K = 8
N_EXPERT_GROUPS = 8
TOPK_GROUPS = 4
ROUTE_SCALE = 2.5
RMS_EPS = 1e-6

LANES = 128
SUBLANES = 8
VMEM_LIMIT = 56 * 1024 * 1024
EXPERT_ROWS = 256
SC_CORES = 2
SC_SUBCORES = 16
SC_WORKERS = SC_CORES * SC_SUBCORES
NT_DIMS = (((1,), (1,)), ((), ()))


def _params(*sem):
    return pltpu.CompilerParams(dimension_semantics=sem, vmem_limit_bytes=VMEM_LIMIT)


def _silu(x):
    return x * jax.nn.sigmoid(x)


def _rms_rows(x, g):
    return x * lax.rsqrt(jnp.mean(x * x, axis=-1, keepdims=True) + RMS_EPS) * g


def _pack_pairs(x):
    n = x.shape[1] // 2
    bits = pltpu.bitcast(x.astype(BF16).astype(F32), jnp.uint32)
    return bits[:, :n] | (bits[:, n:] >> jnp.uint32(16))


def _unpack_pairs(p):
    hi = pltpu.bitcast(p & jnp.uint32(0xFFFF0000), F32)
    lo = pltpu.bitcast(p << jnp.uint32(16), F32)
    return jnp.concatenate([hi, lo], axis=1)


def _mod_kernel(c_ref, w_ref, b_ref, o_ref):
    s = _silu(c_ref[...]).astype(BF16)
    o_ref[...] = jnp.dot(s, w_ref[...].astype(BF16), preferred_element_type=F32) + b_ref[...]


def _modulation(c, w_mod, b_mod):
    m, d = c.shape
    n = w_mod.shape[1]
    tn = n // 4
    return pl.pallas_call(
        _mod_kernel,
        grid=(n // tn,),
        in_specs=[pl.BlockSpec((m, d), lambda j: (0, 0)),
                  pl.BlockSpec((d, tn), lambda j: (0, j)),
                  pl.BlockSpec((1, tn), lambda j: (0, j))],
        out_specs=pl.BlockSpec((m, tn), lambda j: (0, j)),
        out_shape=jax.ShapeDtypeStruct((m, n), F32),
        compiler_params=_params("arbitrary"),
        name="modulation",
    )(c, w_mod, b_mod.reshape(1, n))


class _Mod:
    def __init__(self, mod, seq, tm, d):
        self.d = d
        b = mod.shape[0]
        if seq % tm == 0:
            per = seq // tm
            self.arr = mod.reshape(b, 1, mod.shape[1])
            self._spec = lambda col: pl.BlockSpec((None, 1, d), lambda i: (i // per, 0, col))
        else:
            self.arr = jnp.repeat(mod, seq, axis=0)
            self._spec = lambda col: pl.BlockSpec((tm, d), lambda i: (i, col))

    def spec(self, col):
        return self._spec(col)


def _prenorm_kernel(x_ref, g_ref, sc_ref, sh_ref, o_ref):
    y = _rms_rows(x_ref[...], g_ref[...])
    o_ref[...] = (y * (1 + sc_ref[...]) + sh_ref[...]).astype(o_ref.dtype)


def _prenorm(x2, g, mod, col_scale, col_shift, tm):
    t, d = x2.shape
    return pl.pallas_call(
        _prenorm_kernel,
        grid=(t // tm,),
        in_specs=[pl.BlockSpec((tm, d), lambda i: (i, 0)),
                  pl.BlockSpec((1, d), lambda i: (0, 0)),
                  mod.spec(col_scale), mod.spec(col_shift)],
        out_specs=pl.BlockSpec((tm, d), lambda i: (i, 0)),
        out_shape=jax.ShapeDtypeStruct((t, d), BF16),
        compiler_params=_params("parallel"),
        name="prenorm",
    )(x2, g.reshape(1, d), mod.arr, mod.arr)


def _mm_kernel(h_ref, w_ref, o_ref):
    o_ref[...] = jnp.dot(h_ref[...], w_ref[...], preferred_element_type=F32).astype(o_ref.dtype)


def _matmul(h, w, out_dtype, tm, name):
    t, k = h.shape
    n = w.shape[1]
    return pl.pallas_call(
        _mm_kernel,
        grid=(t // tm,),
        in_specs=[pl.BlockSpec((tm, k), lambda i: (i, 0)),
                  pl.BlockSpec((k, n), lambda i: (0, 0))],
        out_specs=pl.BlockSpec((tm, n), lambda i: (i, 0)),
        out_shape=jax.ShapeDtypeStruct((t, n), out_dtype),
        compiler_params=_params("parallel"),
        name=name,
    )(h, w)


def _bucket(dist):
    max_exact = NUM_BUCKETS // 2
    far = max_exact + (jnp.log(jnp.maximum(dist, 1).astype(F32) / max_exact)
                       / math.log(MAX_DISTANCE / max_exact) * (NUM_BUCKETS - max_exact)).astype(jnp.int32)
    return jnp.where(dist < max_exact, dist, jnp.minimum(far, NUM_BUCKETS - 1))


def _band_bias(rel_bias, w):
    qi = jnp.arange(w)[:, None]
    sj = jnp.arange(2 * w)[None, :]
    sub_dist = jnp.clip(qi + w - sj, 0, w)
    out = []
    for g, (_, dil) in enumerate(DIL_PATTERNS):
        tab = rel_bias[:, g * ATT_HEADS:(g + 1) * ATT_HEADS]
        out.append(jnp.transpose(tab[_bucket(sub_dist * dil)], (2, 0, 1)))
    return jnp.concatenate(out, axis=0).astype(F32)


def _attn_prompt_kernel(q0_ref, q1_ref, q2_ref, k_ref, v_ref, b0_ref, b1_ref, b2_ref, o_ref,
                        og_scr, lse_scr, *, seq, w):
    q_refs = (q0_ref, q1_ref, q2_ref)
    b_refs = (b0_ref, b1_ref, b2_ref)
    qi2 = lax.broadcasted_iota(jnp.int32, (w, 2 * w), 0)
    sj2 = lax.broadcasted_iota(jnp.int32, (w, 2 * w), 1)
    dist2 = qi2 + w - sj2
    band2 = jnp.logical_and(dist2 >= 0, dist2 <= w)
    qi1 = lax.broadcasted_iota(jnp.int32, (w, w), 0)
    sj1 = lax.broadcasted_iota(jnp.int32, (w, w), 1)
    band1 = qi1 >= sj1

    def rows(start, size, dil):
        return pl.ds(start, size) if dil == 1 else pl.ds(start, size, stride=dil)

    for g, (_, dil) in enumerate(DIL_PATTERNS):
        nb = seq // dil // w
        for r in range(dil):
            for n in range(nb):
                q_rows = rows(r + dil * n * w, w, dil)
                qb = q_refs[g][q_rows, :].astype(BF16)
                if n == 0:
                    k_rows = rows(r, w, dil)
                    bias = b_refs[g][:, w:]
                    band = band1
                else:
                    k_rows = rows(r + dil * (n - 1) * w, 2 * w, dil)
                    bias = b_refs[g][...]
                    band = band2
                kb = k_ref[k_rows, :].astype(BF16)
                vb = v_ref[k_rows, :].astype(BF16)
                s = lax.dot_general(qb, kb, NT_DIMS, preferred_element_type=F32) * ATT_SCALE + bias
                s = jnp.where(band, s, -jnp.inf)
                m = jnp.max(s, axis=-1, keepdims=True)
                p = jnp.exp(s - m)
                den = jnp.sum(p, axis=-1, keepdims=True)
                o = jnp.dot(p.astype(BF16), vb, preferred_element_type=F32) / den
                og_scr[g, q_rows, :] = o
                lse_scr[g, q_rows, :] = jnp.broadcast_to(m + jnp.log(den), (w, LANES))

    step = 256
    for c in range(seq // step):
        sl = pl.ds(c * step, step)
        l0, l1, l2 = lse_scr[0, sl, :], lse_scr[1, sl, :], lse_scr[2, sl, :]
        mm = jnp.maximum(jnp.maximum(l0, l1), l2)
        e0, e1, e2 = jnp.exp(l0 - mm), jnp.exp(l1 - mm), jnp.exp(l2 - mm)
        tot = e0 + e1 + e2
        o = (e0 / tot) * og_scr[0, sl, :] + (e1 / tot) * og_scr[1, sl, :] + (e2 / tot) * og_scr[2, sl, :]
        o_ref[sl, :] = o.astype(o_ref.dtype)


def _attn_prompt(q, k, v, rel_bias):
    b, seq, _ = k.shape
    w = DIL_PATTERNS[0][0] // DIL_PATTERNS[0][1]
    for win, dil in DIL_PATTERNS:
        assert win // dil == w and seq % (dil * w) == 0
    bias = _band_bias(rel_bias, w)
    hd = HEAD_DIM

    def q_spec(g):
        return pl.BlockSpec((None, seq, hd), lambda i, h: (i, 0, g * ATT_HEADS + h))

    def b_spec(g):
        return pl.BlockSpec((None, w, 2 * w), lambda i, h: (g * ATT_HEADS + h, 0, 0))

    kv_spec = pl.BlockSpec((None, seq, hd), lambda i, h: (i, 0, h))
    return pl.pallas_call(
        functools.partial(_attn_prompt_kernel, seq=seq, w=w),
        grid=(b, ATT_HEADS),
        in_specs=[q_spec(0), q_spec(1), q_spec(2), kv_spec, kv_spec, b_spec(0), b_spec(1), b_spec(2)],
        out_specs=pl.BlockSpec((None, seq, hd), lambda i, h: (i, 0, h)),
        out_shape=jax.ShapeDtypeStruct((b, seq, ATT_WIDTH), BF16),
        scratch_shapes=[pltpu.VMEM((N_DIL, seq, hd), F32), pltpu.VMEM((N_DIL, seq, LANES), F32)],
        compiler_params=_params("parallel", "parallel"),
        name="attn_prompt",
    )(q, q, q, k, v, bias, bias, bias)


SAMPLE_KEY_CHUNK = 512


def _attn_sample_kernel(q_ref, kn_ref, vn_ref, kn4_ref, vn4_ref, ck_ref, cv_ref, bm_ref, bn_ref,
                        o_ref, wk_ref, wv_ref, m_scr, l_scr, acc_scr, ck_scr, cv_scr, *, t, nch):
    c = pl.program_id(1)
    ch = ck_ref.shape[0]
    tr = t * ATT_HEADS
    rows = N_DIL * t

    @pl.when(c == 0)
    def _():
        m_scr[...] = jnp.full(m_scr.shape, -jnp.inf, F32)
        l_scr[...] = jnp.zeros(l_scr.shape, F32)
        acc_scr[...] = jnp.zeros(acc_scr.shape, F32)
        ck_scr[...] = kn4_ref[...]
        cv_scr[...] = vn4_ref[...]

    wk_ref[0:ch - tr, :] = ck_ref[tr:ch, :]
    wk_ref[ch - tr:ch, :] = ck_scr[...]
    ck_scr[...] = ck_ref[0:tr, :]
    wv_ref[0:ch - tr, :] = cv_ref[tr:ch, :]
    wv_ref[ch - tr:ch, :] = cv_scr[...]
    cv_scr[...] = cv_ref[0:tr, :]

    def head_q(h):
        return jnp.concatenate(
            [q_ref[:, g * ATT_WIDTH + h * HEAD_DIM:g * ATT_WIDTH + (h + 1) * HEAD_DIM] for g in range(N_DIL)], axis=0)

    for h in range(ATT_HEADS):
        qa = head_q(h).astype(BF16)
        head_rows = pl.ds(h, ch // ATT_HEADS, stride=ATT_HEADS)
        kh = ck_ref[head_rows, :].astype(BF16)
        vh = cv_ref[head_rows, :].astype(BF16)
        s = lax.dot_general(qa, kh, NT_DIMS, preferred_element_type=F32) * ATT_SCALE + bm_ref[h]
        m_old = m_scr[h]
        m_new = jnp.maximum(m_old, jnp.max(s, axis=-1, keepdims=True))
        alpha = jnp.exp(m_old - m_new)
        p = jnp.exp(s - m_new[:, 0:1])
        l_scr[h] = alpha * l_scr[h] + jnp.sum(p, axis=-1, keepdims=True)
        acc_scr[h] = alpha * acc_scr[h] + jnp.dot(p.astype(BF16), vh, preferred_element_type=F32)
        m_scr[h] = m_new

    @pl.when(c == nch - 1)
    def _():
        for h in range(ATT_HEADS):
            cs = slice(h * HEAD_DIM, (h + 1) * HEAD_DIM)
            qa = head_q(h)
            s_new = [jnp.sum(qa * kn_ref[j:j + 1, cs], axis=-1, keepdims=True) * ATT_SCALE + bn_ref[h, j]
                     for j in range(t)]
            m_old = m_scr[h]
            m_new = m_old
            for sj in s_new:
                m_new = jnp.maximum(m_new, sj)
            alpha = jnp.exp(m_old - m_new)
            den = alpha * l_scr[h]
            acc = alpha * acc_scr[h]
            for j, sj in enumerate(s_new):
                pj = jnp.exp(sj - m_new)
                den = den + pj
                acc = acc + pj * vn_ref[j:j + 1, cs]
            o = acc / den
            lse = m_new + jnp.log(den)
            l0, l1, l2 = lse[0:t], lse[t:2 * t], lse[2 * t:rows]
            mm = jnp.maximum(jnp.maximum(l0, l1), l2)
            e0, e1, e2 = jnp.exp(l0 - mm), jnp.exp(l1 - mm), jnp.exp(l2 - mm)
            tot = e0 + e1 + e2
            o_ref[:, cs] = (e0 / tot) * o[0:t] + (e1 / tot) * o[t:2 * t] + (e2 / tot) * o[2 * t:rows]


def _sample_bias(rel_bias, t, n_buf):
    tok = jnp.arange(t)[:, None]
    key = jnp.arange(n_buf + t)[None, :]
    dist = n_buf + tok - key
    full = []
    for g, (win, dil) in enumerate(DIL_PATTERNS):
        ok = (dist >= 0) & (dist % dil == 0) & (dist <= win)
        tab = rel_bias[:, g * ATT_HEADS:(g + 1) * ATT_HEADS]
        val = jnp.transpose(tab[_bucket(jnp.maximum(dist, 0))], (2, 0, 1))
        full.append(jnp.where(ok[None], val, -jnp.inf))
    full = jnp.concatenate(full, axis=1).astype(F32)
    new = jnp.transpose(full[:, :, n_buf:], (0, 2, 1))
    new = jnp.broadcast_to(new[..., None], new.shape + (LANES,))
    return full[:, :, :n_buf], new


def _attn_sample(q, k, v, cache_k, cache_v, rel_bias):
    b, t, _ = k.shape
    n_buf = cache_k.shape[1]
    ch = SAMPLE_KEY_CHUNK
    assert t == SUBLANES and n_buf % ch == 0
    assert all(win <= n_buf for win, _ in DIL_PATTERNS)
    nch = n_buf // ch
    bm, bn = _sample_bias(rel_bias, t, n_buf)
    flat = lambda a: a.reshape(b, -1, HEAD_DIM)
    rows = N_DIL * t
    tr = t * ATT_HEADS

    def per_b(*shape):
        return pl.BlockSpec((None,) + shape, lambda i, c: (i,) + (0,) * len(shape))

    buf_spec = pl.BlockSpec((None, ch * ATT_HEADS, HEAD_DIM), lambda i, c: (i, nch - 1 - c, 0))
    buf_shape = jax.ShapeDtypeStruct((b, n_buf * ATT_HEADS, HEAD_DIM), F32)
    o, win_k, win_v = pl.pallas_call(
        functools.partial(_attn_sample_kernel, t=t, nch=nch),
        grid=(b, nch),
        in_specs=[per_b(t, N_DIL * ATT_WIDTH), per_b(t, ATT_WIDTH), per_b(t, ATT_WIDTH),
                  per_b(tr, HEAD_DIM), per_b(tr, HEAD_DIM), buf_spec, buf_spec,
                  pl.BlockSpec((ATT_HEADS, rows, ch), lambda i, c: (0, 0, nch - 1 - c)),
                  pl.BlockSpec((ATT_HEADS, t, rows, LANES), lambda i, c: (0, 0, 0, 0))],
        out_specs=[per_b(t, ATT_WIDTH), buf_spec, buf_spec],
        out_shape=[jax.ShapeDtypeStruct((b, t, ATT_WIDTH), F32), buf_shape, buf_shape],
        scratch_shapes=[pltpu.VMEM((ATT_HEADS, rows, LANES), F32), pltpu.VMEM((ATT_HEADS, rows, LANES), F32),
                        pltpu.VMEM((ATT_HEADS, rows, HEAD_DIM), F32),
                        pltpu.VMEM((tr, HEAD_DIM), F32), pltpu.VMEM((tr, HEAD_DIM), F32)],
        compiler_params=_params("parallel", "arbitrary"),
        name="attn_sample",
    )(q, k, v, flat(k), flat(v), flat(cache_k), flat(cache_v), bm, bn)
    return o, win_k.reshape(cache_k.shape), win_v.reshape(cache_v.shape)


def _softplus(x):
    return jnp.maximum(x, 0.0) + jnp.log(1.0 + jnp.exp(-jnp.abs(x)))


def _ssd_kernel(xbc_ref, z_ref, dt_ref, cbuf_ref, h0_ref, cw_ref, cb_ref, dtb_ref, alog_ref, dsk_ref, gn_ref,
                y_ref, hl_ref, ext_scr, st_scr, y_scr, acst_scr, dtt_scr, wstt_scr, *, lv, nc, d_inner):
    q = SSD_CHUNK
    n = D_STATE
    c = pl.program_id(1)
    n_pairs = d_inner // LANES
    pairs_per_group = n_pairs // SSM_GROUPS

    @pl.when(c == 0)
    def _():
        ext_scr[0:SUBLANES, :] = cbuf_ref[...]
        for i in range(n_pairs):
            st_scr[:, i * LANES:(i + 1) * LANES] = h0_ref[i * LANES:(i + 1) * LANES, :].T

    ext_scr[SUBLANES:SUBLANES + lv, :] = xbc_ref[...]
    if lv < q:
        ext_scr[SUBLANES + lv:SUBLANES + q, :] = jnp.zeros((q - lv, ext_scr.shape[1]), F32)
    first = SUBLANES - (D_CONV - 1)
    u = cb_ref[...] + cw_ref[0:1, :] * ext_scr[first:first + q, :]
    for j in range(1, D_CONV):
        u = u + cw_ref[j:j + 1, :] * ext_scr[first + j:first + j + q, :]
    u = _silu(u)
    if nc > 1:
        ext_scr[0:SUBLANES, :] = ext_scr[q:q + SUBLANES, :]

    dtv = _softplus(dt_ref_rows(dt_ref, lv, q) + dtb_ref[...])
    if lv < q:
        row = lax.broadcasted_iota(jnp.int32, (q, LANES), 0)
        dtv = jnp.where(row < lv, dtv, 0.0)
    a = -jnp.exp(alog_ref[...])
    da = dtv * a
    li = lax.broadcasted_iota(jnp.int32, (q, q), 0)
    si = lax.broadcasted_iota(jnp.int32, (q, q), 1)
    tri = li >= si
    acs = jnp.dot(tri.astype(F32), da, preferred_element_type=F32, precision=lax.Precision.HIGHEST)
    acs_last = acs[q - 1:q, :]
    acst_scr[...] = acs.T
    dtt_scr[...] = dtv.T
    wstt_scr[...] = (jnp.exp(acs_last - acs) * dtv).T

    lane = lax.broadcasted_iota(jnp.int32, (1, LANES), 1)
    left = lane < SSM_HEAD_DIM
    for g in range(SSM_GROUPS):
        bg = u[:, d_inner + g * n:d_inner + (g + 1) * n]
        cg = u[:, d_inner + (SSM_GROUPS + g) * n:d_inner + (SSM_GROUPS + g + 1) * n].astype(BF16)
        cb = lax.dot_general(cg, bg.astype(BF16), NT_DIMS, preferred_element_type=F32)
        bgt = bg.T
        for jp in range(pairs_per_group):
            pair = g * pairs_per_group + jp
            cols = slice(pair * LANES, (pair + 1) * LANES)
            x_pair = u[:, cols]
            lhs, e_col, dec = [], [], []
            for h in (2 * pair, 2 * pair + 1):
                a_col = jnp.broadcast_to(acs[:, h:h + 1], (q, q))
                seg = a_col - acst_scr[h:h + 1, :]
                m_h = cb * jnp.exp(jnp.where(tri, seg, -jnp.inf)) * dtt_scr[h:h + 1, :]
                lhs.append(m_h.astype(BF16))
                e_col.append(jnp.exp(a_col))
                dec.append(jnp.exp(a_col[q - 1:q, :]))
            for h in (2 * pair, 2 * pair + 1):
                lhs.append((bgt * wstt_scr[h:h + 1, :]).astype(BF16))
            res = jnp.dot(jnp.concatenate(lhs, axis=0), x_pair.astype(BF16), preferred_element_type=F32)
            y_diag = jnp.where(left, res[0:q], res[q:2 * q])
            d_state = jnp.where(left, res[2 * q:2 * q + n], res[2 * q + n:2 * q + 2 * n])
            st = st_scr[:, cols]
            y_off = jnp.dot(cg, st.astype(BF16), preferred_element_type=F32) * jnp.where(left, e_col[0], e_col[1])
            st_scr[:, cols] = st * jnp.where(left, dec[0], dec[1]) + d_state
            y_scr[:, cols] = y_diag + y_off + dsk_ref[:, cols] * x_pair

    gw = d_inner // SSM_GROUPS
    for g in range(SSM_GROUPS):
        cols = slice(g * gw, (g + 1) * gw)
        yg = y_scr[0:lv, cols] * _silu(z_ref[:, cols])
        yg = yg * lax.rsqrt(jnp.mean(yg * yg, axis=-1, keepdims=True) + RMS_EPS)
        y_ref[:, cols] = (yg * gn_ref[:, cols]).astype(y_ref.dtype)

    @pl.when(c == nc - 1)
    def _():
        for i in range(n_pairs):
            hl_ref[i * LANES:(i + 1) * LANES, :] = st_scr[:, i * LANES:(i + 1) * LANES].T


def dt_ref_rows(dt_ref, lv, q):
    if lv == q:
        return dt_ref[...]
    return jnp.concatenate([dt_ref[...], jnp.zeros((q - lv, dt_ref.shape[1]), F32)], axis=0)


def _ssd_branch(xbc, z, dt, conv_buf, h0, conv_w, conv_b, dt_bias, a_log, d_skip, g_norm, out_dtype):
    b, seqlen, conv_dim = xbc.shape
    d_inner = z.shape[2]
    heads = d_inner // SSM_HEAD_DIM
    q = SSD_CHUNK
    lv = min(q, seqlen)
    assert seqlen % lv == 0 and lv % SUBLANES == 0 and heads <= LANES
    nc = seqlen // lv
    cbuf = jnp.pad(conv_buf, ((0, 0), (SUBLANES - (D_CONV - 1), 0), (0, 0)))
    h0f = h0.reshape(b, heads * SSM_HEAD_DIM, D_STATE)
    pad = LANES - heads
    row = lambda a: jnp.pad(a.astype(F32), (0, pad)).reshape(1, LANES)
    dsk = jnp.repeat(d_skip.astype(F32), SSM_HEAD_DIM).reshape(1, d_inner)

    def per_chunk(width):
        return pl.BlockSpec((None, lv, width), lambda i, c: (i, c, 0))

    def per_batch(r, width):
        return pl.BlockSpec((None, r, width), lambda i, c: (i, 0, 0))

    def const(r, width):
        return pl.BlockSpec((r, width), lambda i, c: (0, 0))

    y, h_last = pl.pallas_call(
        functools.partial(_ssd_kernel, lv=lv, nc=nc, d_inner=d_inner),
        grid=(b, nc),
        in_specs=[per_chunk(conv_dim), per_chunk(d_inner), per_chunk(LANES),
                  per_batch(SUBLANES, conv_dim), per_batch(heads * SSM_HEAD_DIM, D_STATE),
                  const(D_CONV, conv_dim), const(1, conv_dim), const(1, LANES), const(1, LANES),
                  const(1, d_inner), const(1, d_inner)],
        out_specs=[per_chunk(d_inner), per_batch(heads * SSM_HEAD_DIM, D_STATE)],
        out_shape=[jax.ShapeDtypeStruct((b, seqlen, d_inner), out_dtype),
                   jax.ShapeDtypeStruct((b, heads * SSM_HEAD_DIM, D_STATE), F32)],
        scratch_shapes=[pltpu.VMEM((q + 2 * SUBLANES, conv_dim), F32),
                        pltpu.VMEM((D_STATE, d_inner), F32),
                        pltpu.VMEM((q, d_inner), F32),
                        pltpu.VMEM((LANES, q), F32), pltpu.VMEM((LANES, q), F32), pltpu.VMEM((LANES, q), F32)],
        compiler_params=_params("parallel", "arbitrary"),
        name="ssd",
    )(xbc, z, dt, cbuf, h0f, conv_w, conv_b.reshape(1, conv_dim), row(dt_bias), row(a_log), dsk,
      g_norm.reshape(1, d_inner))
    return y, h_last.reshape(b, heads, SSM_HEAD_DIM, D_STATE)


def _mix_kernel(oa_ref, ys_ref, gate_ref, x_ref, gt_ref, wa_ref, wb_ref, wo_ref, g_ref, o_ref, *, d):
    ya = jnp.dot(oa_ref[...].astype(BF16), wa_ref[...], preferred_element_type=F32)
    yb = jnp.dot(ys_ref[...].astype(BF16), wb_ref[...], preferred_element_type=F32)
    mixed = jax.nn.sigmoid(gate_ref[:, 0:d]) * ya + jax.nn.sigmoid(gate_ref[:, d:2 * d]) * yb
    mix = jnp.dot(mixed.astype(BF16), wo_ref[...], preferred_element_type=F32)
    o_ref[...] = x_ref[...] + gt_ref[...] * _rms_rows(mix, g_ref[...])


def _mix(o_att, y_ssm, gates, x2, mod, wa, wb, wo, g_post, tm):
    t, d = x2.shape
    full = lambda a: pl.BlockSpec(a.shape, lambda i: (0, 0))
    rows = lambda a: pl.BlockSpec((tm, a.shape[1]), lambda i: (i, 0))
    g2 = g_post.reshape(1, d)
    return pl.pallas_call(
        functools.partial(_mix_kernel, d=d),
        grid=(t // tm,),
        in_specs=[rows(o_att), rows(y_ssm), rows(gates), rows(x2), mod.spec(2), full(wa), full(wb), full(wo),
                  full(g2)],
        out_specs=pl.BlockSpec((tm, d), lambda i: (i, 0)),
        out_shape=jax.ShapeDtypeStruct((t, d), F32),
        compiler_params=_params("parallel"),
        name="mix",
    )(o_att, y_ssm, gates, x2, mod.arr, wa, wb, wo, g2)


def _first_index(hit, iota, size):
    return jnp.min(jnp.where(hit, iota, size), axis=0, keepdims=True)


def _ffn_pre_kernel(x_ref, g_ref, sc_ref, sh_ref, wr_ref, br_ref, h_ref, idx_ref, w_ref, *, n_exp):
    y = _rms_rows(x_ref[...], g_ref[...])
    hf = y * (1 + sc_ref[...]) + sh_ref[...]
    h_ref[...] = _pack_pairs(hf)
    h = hf.astype(BF16)
    tm = h.shape[0]
    scores = jax.nn.sigmoid(lax.dot_general(wr_ref[...], h, NT_DIMS, preferred_element_type=F32))
    biased = scores + br_ref[...]
    gsz = n_exp // N_EXPERT_GROUPS
    riota = lax.broadcasted_iota(jnp.int32, (gsz, tm), 0)
    gs = []
    for g in range(N_EXPERT_GROUPS):
        xg = biased[g * gsz:(g + 1) * gsz, :]
        m1 = jnp.max(xg, axis=0, keepdims=True)
        i1 = _first_index(xg == m1, riota, gsz)
        m2 = jnp.max(jnp.where(riota == i1, -jnp.inf, xg), axis=0, keepdims=True)
        gs.append(m1 + m2)
    gs = jnp.concatenate(gs, axis=0)
    giota = lax.broadcasted_iota(jnp.int32, gs.shape, 0)
    keep = jnp.zeros(gs.shape, F32)
    for _ in range(TOPK_GROUPS):
        gi = _first_index(gs == jnp.max(gs, axis=0, keepdims=True), giota, N_EXPERT_GROUPS)
        sel = giota == gi
        keep = jnp.where(sel, 1.0, keep)
        gs = jnp.where(sel, -jnp.inf, gs)
    masked = jnp.concatenate(
        [jnp.where(keep[g:g + 1, :] > 0.5, biased[g * gsz:(g + 1) * gsz, :], -jnp.inf)
         for g in range(N_EXPERT_GROUPS)], axis=0)
    eiota = lax.broadcasted_iota(jnp.int32, (n_exp, tm), 0)
    idxs, ws = [], []
    for _ in range(TOP_K):
        ik = _first_index(masked == jnp.max(masked, axis=0, keepdims=True), eiota, n_exp)
        hit = eiota == ik
        idxs.append(ik)
        ws.append(jnp.sum(jnp.where(hit, scores, 0.0), axis=0, keepdims=True))
        masked = jnp.where(hit, -jnp.inf, masked)
    idx_ref[...] = jnp.concatenate(idxs, axis=0)
    w = jnp.concatenate(ws, axis=0)
    w_ref[...] = w / jnp.sum(w, axis=0, keepdims=True) * ROUTE_SCALE


def _ffn_pre(x1, g, mod, w_router_t, b_router, tm):
    t, d = x1.shape
    e = w_router_t.shape[0]
    br = jnp.broadcast_to(b_router.astype(F32)[:, None], (e, tm))
    return pl.pallas_call(
        functools.partial(_ffn_pre_kernel, n_exp=e),
        grid=(t // tm,),
        in_specs=[pl.BlockSpec((tm, d), lambda i: (i, 0)),
                  pl.BlockSpec((1, d), lambda i: (0, 0)),
                  mod.spec(4), mod.spec(3),
                  pl.BlockSpec((e, d), lambda i: (0, 0)),
                  pl.BlockSpec((e, tm), lambda i: (0, 0))],
        out_specs=[pl.BlockSpec((tm, d // 2), lambda i: (i, 0)),
                   pl.BlockSpec((TOP_K, tm), lambda i: (0, i)),
                   pl.BlockSpec((TOP_K, tm), lambda i: (0, i))],
        out_shape=[jax.ShapeDtypeStruct((t, d // 2), jnp.uint32), jax.ShapeDtypeStruct((TOP_K, t), jnp.int32),
                   jax.ShapeDtypeStruct((TOP_K, t), F32)],
        compiler_params=_params("parallel"),
        name="ffn_pre",
    )(x1, g.reshape(1, d), mod.arr, mod.arr, w_router_t, br)


def _rank_kernel(idx_ref, rank_ref, cnt_ref, base_scr, *, n_exp):
    i = pl.program_id(0)
    tm = idx_ref.shape[1]

    @pl.when(i == 0)
    def _():
        base_scr[...] = jnp.zeros(base_scr.shape, F32)

    eiota = lax.broadcasted_iota(jnp.int32, (n_exp, tm), 0)
    idx = idx_ref[...]
    hits = [eiota == idx[k:k + 1, :] for k in range(TOP_K)]
    multi = jnp.zeros((n_exp, tm), F32)
    for hit in hits:
        multi = multi + jnp.where(hit, 1.0, 0.0)
    multi = multi.astype(BF16)
    ti = lax.broadcasted_iota(jnp.int32, (tm, tm), 0)
    tj = lax.broadcasted_iota(jnp.int32, (tm, tm), 1)
    earlier = jnp.where(ti < tj, 1.0, 0.0).astype(BF16)
    before = jnp.dot(multi, earlier, preferred_element_type=F32) + base_scr[:, 0:1]
    ranks = [jnp.sum(jnp.where(hit, before, 0.0), axis=0, keepdims=True) for hit in hits]
    rank_ref[...] = jnp.concatenate(ranks, axis=0).astype(jnp.int32)
    base_scr[...] = base_scr[...] + jnp.dot(multi, jnp.ones((tm, LANES), BF16), preferred_element_type=F32)
    cnt_ref[...] = base_scr[...]


def _expert_ranks(idx_t, n_exp, tm):
    t = idx_t.shape[1]
    return pl.pallas_call(
        functools.partial(_rank_kernel, n_exp=n_exp),
        grid=(t // tm,),
        in_specs=[pl.BlockSpec((TOP_K, tm), lambda i: (0, i))],
        out_specs=[pl.BlockSpec((TOP_K, tm), lambda i: (0, i)), pl.BlockSpec((n_exp, LANES), lambda i: (0, 0))],
        out_shape=[jax.ShapeDtypeStruct((TOP_K, t), jnp.int32), jax.ShapeDtypeStruct((n_exp, LANES), F32)],
        scratch_shapes=[pltpu.VMEM((n_exp, LANES), F32)],
        compiler_params=_params("arbitrary"),
        name="expert_ranks",
    )(idx_t)


def _pos_kernel(idx_ref, rank_ref, start_ref, pos_ref, *, n_exp):
    tm = idx_ref.shape[1]
    eiota = lax.broadcasted_iota(jnp.int32, (n_exp, tm), 0)
    idx = idx_ref[...]
    start = start_ref[:, 0:1]
    offs = [jnp.sum(jnp.where(eiota == idx[k:k + 1, :], start, 0.0), axis=0, keepdims=True) for k in range(TOP_K)]
    pos_ref[...] = rank_ref[...] + jnp.concatenate(offs, axis=0).astype(jnp.int32)


def _positions(idx_t, rank_t, pad_start, tm):
    t = idx_t.shape[1]
    n_exp = pad_start.shape[0]
    start = jnp.broadcast_to(pad_start.astype(F32)[:, None], (n_exp, LANES))
    spec = pl.BlockSpec((TOP_K, tm), lambda i: (0, i))
    return pl.pallas_call(
        functools.partial(_pos_kernel, n_exp=n_exp),
        grid=(t // tm,),
        in_specs=[spec, spec, pl.BlockSpec((n_exp, LANES), lambda i: (0, 0))],
        out_specs=spec,
        out_shape=jax.ShapeDtypeStruct((TOP_K, t), jnp.int32),
        compiler_params=_params("parallel"),
        name="positions",
    )(idx_t, rank_t, start)


def _experts_kernel(be_ref, nu_ref, x_ref, wg_ref, wu_ref, wd_ref, o_ref, wg_scr, wu_scr, wd_scr):
    i = pl.program_id(0)
    prev = be_ref[jnp.maximum(i - 1, 0)]
    fresh = jnp.logical_or(i == 0, be_ref[i] != prev)

    @pl.when(jnp.logical_and(fresh, i < nu_ref[0]))
    def _():
        wg_scr[...] = wg_ref[...].astype(BF16)
        wu_scr[...] = wu_ref[...].astype(BF16)
        wd_scr[...] = wd_ref[...].astype(BF16)

    @pl.when(i < nu_ref[0])
    def _():
        xb = _unpack_pairs(x_ref[...]).astype(BF16)
        act = _silu(jnp.dot(xb, wg_scr[...], preferred_element_type=F32)) * jnp.dot(
            xb, wu_scr[...], preferred_element_type=F32)
        o_ref[...] = _pack_pairs(jnp.dot(act.astype(BF16), wd_scr[...], preferred_element_type=F32))

    @pl.when(i >= nu_ref[0])
    def _():
        o_ref[...] = jnp.zeros(o_ref.shape, o_ref.dtype)


def _sc_dispatch(h, pos_flat, rows):
    t, w = h.shape
    chunk, nbuf = 48, 2
    per_w = t // SC_WORKERS
    assert t % SC_WORKERS == 0 and per_w % (chunk * nbuf) == 0
    n = per_w // chunk
    mesh = plsc.VectorSubcoreMesh(core_axis_name="c", subcore_axis_name="s")

    @functools.partial(pl.kernel, mesh=mesh, out_type=jax.ShapeDtypeStruct((rows, w), h.dtype),
                       scratch_types=[pltpu.VMEM((nbuf, TOP_K, chunk), jnp.int32),
                                      pltpu.VMEM((nbuf, chunk, w), h.dtype),
                                      pltpu.SemaphoreType.DMA((nbuf,))])
    def scatter_rows(h_hbm, pos_hbm, x_hbm, idx_v, rows_v, sem):
        w0 = (lax.axis_index("s") * SC_CORES + lax.axis_index("c")) * per_w

        def scatter(slot, kk):
            return pltpu.make_async_copy(rows_v.at[slot], x_hbm.at[idx_v.at[slot, kk]], sem.at[slot])

        @pl.loop(0, n // nbuf)
        def _(g):
            for s in range(nbuf):
                @pl.when(g > 0)
                def _():
                    for kk in range(TOP_K):
                        scatter(s, kk).wait()
                base = pl.multiple_of(w0 + (g * nbuf + s) * chunk, SUBLANES)
                pltpu.sync_copy(h_hbm.at[pl.ds(base, chunk)], rows_v.at[s])
                for kk in range(TOP_K):
                    pltpu.sync_copy(pos_hbm.at[pl.ds(pl.multiple_of(kk * t + base, SUBLANES), chunk)],
                                    idx_v.at[s, kk])
                for kk in range(TOP_K):
                    scatter(s, kk).start()

        for s in range(nbuf):
            for kk in range(TOP_K):
                scatter(s, kk).wait()

    return scatter_rows(h, pos_flat)


def _sc_gather(table, idx_flat):
    n_idx = idx_flat.shape[0]
    w = table.shape[1]
    chunk, nbuf = 48, 4
    per_w = n_idx // SC_WORKERS
    assert n_idx % SC_WORKERS == 0 and per_w % (chunk * nbuf) == 0
    n = per_w // chunk
    mesh = plsc.VectorSubcoreMesh(core_axis_name="c", subcore_axis_name="s")

    @functools.partial(pl.kernel, mesh=mesh, out_type=jax.ShapeDtypeStruct((n_idx, w), table.dtype),
                       scratch_types=[pltpu.VMEM((nbuf, chunk), jnp.int32),
                                      pltpu.VMEM((nbuf, chunk, w), table.dtype),
                                      pltpu.SemaphoreType.DMA((nbuf,))])
    def gather_rows(t_hbm, idx_hbm, o_hbm, idx_v, rows_v, sem):
        w0 = (lax.axis_index("s") * SC_CORES + lax.axis_index("c")) * per_w

        def gather(slot):
            return pltpu.make_async_copy(t_hbm.at[idx_v.at[slot]], rows_v.at[slot], sem.at[slot])

        def issue(j, slot):
            base = pl.multiple_of(w0 + j * chunk, SUBLANES)
            pltpu.sync_copy(idx_hbm.at[pl.ds(base, chunk)], idx_v.at[slot])
            gather(slot).start()

        for s in range(nbuf):
            issue(s, s)

        @pl.loop(0, n // nbuf)
        def _(g):
            for s in range(nbuf):
                j = g * nbuf + s
                gather(s).wait()
                pltpu.sync_copy(rows_v.at[s], o_hbm.at[pl.ds(pl.multiple_of(w0 + j * chunk, SUBLANES), chunk)])

                @pl.when(j + nbuf < n)
                def _():
                    issue(j + nbuf, s)

    return gather_rows(table, idx_flat)


def _routed_experts(h, idx_t, wg, wu, wd):
    kk, t = idx_t.shape
    dp = h.shape[1]
    n_exp, d, de = wg.shape
    blk = EXPERT_ROWS
    n_assign = t * kk
    tm = 512
    assert t % tm == 0
    rank_t, cnt = _expert_ranks(idx_t, n_exp, tm)
    counts = cnt[:, 0].astype(jnp.int32)
    padded = (counts + blk - 1) // blk * blk
    pad_end = jnp.cumsum(padded)
    pad_start = pad_end - padded
    pos_flat = _positions(idx_t, rank_t, pad_start, tm).reshape(-1)
    n_blocks = (n_assign + n_exp * (blk - 1)) // blk
    rows = n_blocks * blk
    block_e = jnp.minimum(jnp.searchsorted(pad_end, jnp.arange(n_blocks, dtype=jnp.int32) * blk, side='right'),
                          n_exp - 1).astype(jnp.int32)
    n_used = (pad_end[-1] // blk).astype(jnp.int32).reshape(1)
    x_buf = _sc_dispatch(h, pos_flat, rows)
    grid_spec = pltpu.PrefetchScalarGridSpec(
        num_scalar_prefetch=2,
        grid=(n_blocks,),
        in_specs=[pl.BlockSpec((blk, dp), lambda i, be, nu: (i, 0)),
                  pl.BlockSpec((None, d, de), lambda i, be, nu: (be[i], 0, 0)),
                  pl.BlockSpec((None, d, de), lambda i, be, nu: (be[i], 0, 0)),
                  pl.BlockSpec((None, de, d), lambda i, be, nu: (be[i], 0, 0))],
        out_specs=pl.BlockSpec((blk, dp), lambda i, be, nu: (i, 0)),
        scratch_shapes=[pltpu.VMEM((d, de), BF16), pltpu.VMEM((d, de), BF16), pltpu.VMEM((de, d), BF16)],
    )
    out = pl.pallas_call(
        _experts_kernel,
        grid_spec=grid_spec,
        out_shape=jax.ShapeDtypeStruct((rows, dp), jnp.uint32),
        compiler_params=_params("arbitrary"),
        name="experts",
    )(block_e, n_used, x_buf, wg, wu, wd)
    return _sc_gather(out, pos_flat).reshape(kk, t, dp)


def _ffn_post_kernel(h_ref, r_ref, wt_ref, x_ref, gt_ref, wg_ref, wu_ref, wd_ref, g_ref, o_ref):
    h = _unpack_pairs(h_ref[...]).astype(BF16)
    act = _silu(jnp.dot(h, wg_ref[...], preferred_element_type=F32)) * jnp.dot(
        h, wu_ref[...], preferred_element_type=F32)
    f = jnp.dot(act.astype(BF16), wd_ref[...], preferred_element_type=F32)
    for k in range(TOP_K):
        f = f + _unpack_pairs(r_ref[k]) * wt_ref[:, k:k + 1]
    o_ref[...] = x_ref[...] + gt_ref[...] * _rms_rows(f, g_ref[...])


def _ffn_post(h2, routed, w_rows, row0, x1, mod, wg, wu, wd, g_post, tm):
    t, d = x1.shape
    assert row0 % tm == 0
    blk0 = row0 // tm
    full = lambda a: pl.BlockSpec(a.shape, lambda i: (0, 0))
    rows = lambda a: pl.BlockSpec((tm, a.shape[1]), lambda i: (i, 0))
    g2 = g_post.reshape(1, d)
    return pl.pallas_call(
        _ffn_post_kernel,
        grid=(t // tm,),
        in_specs=[rows(h2),
                  pl.BlockSpec((TOP_K, tm, d // 2), lambda i: (0, i + blk0, 0)),
                  pl.BlockSpec((tm, TOP_K), lambda i: (i + blk0, 0)),
                  rows(x1), mod.spec(5), full(wg), full(wu), full(wd), full(g2)],
        out_specs=pl.BlockSpec((tm, d), lambda i: (i, 0)),
        out_shape=jax.ShapeDtypeStruct((t, d), F32),
        compiler_params=_params("parallel"),
        name="ffn_post",
    )(h2, routed, w_rows, x1, mod.arr, wg, wu, wd, g2)


def _channel_mixer(x1s, mods, tms, lw):
    pre = [_ffn_pre(x1, lw['g_pre_ffn'], mod, lw['w_router_t'], lw['b_router'], tm)
           for x1, mod, tm in zip(x1s, mods, tms)]
    h2 = jnp.concatenate([p[0] for p in pre], axis=0)
    idx_t = jnp.concatenate([p[1] for p in pre], axis=1)
    w_rows = jnp.concatenate([p[2] for p in pre], axis=1).T
    routed = _routed_experts(h2, idx_t, lw['w_exp_gate'], lw['w_exp_up'], lw['w_exp_down'])
    outs, row0 = [], 0
    for x1, mod, tm, p in zip(x1s, mods, tms, pre):
        outs.append(_ffn_post(p[0], routed, w_rows, row0, x1, mod, lw['w_sh_gate'], lw['w_sh_up'],
                              lw['w_sh_down'], lw['g_post_ffn'], tm))
        row0 += x1.shape[0]
    return outs


def _token_mixer(x, mod_rows, attend, conv_buf, h0, lw, tm, ssm_dtype):
    b, seq, d = x.shape
    t = b * seq
    x2 = x.reshape(t, d)
    mod = _Mod(mod_rows, seq, tm, d)
    h = _prenorm(x2, lw['g_pre_mix'], mod, 1, 0, tm)
    proj = {name: _matmul(h, w, dt, tm, "proj_" + name) for name, (w, dt) in lw['w_in'].items()}
    q = proj['q'].reshape(b, seq, -1)
    k = proj['k'].reshape(b, seq, -1)
    v = proj['v'].reshape(b, seq, -1)
    o_att, new_k, new_v = attend(q, k, v)
    xbc = proj['xbc'].reshape(b, seq, -1)
    y_ssm, h_last = _ssd_branch(xbc, proj['z'].reshape(b, seq, -1), proj['dt'].reshape(b, seq, -1),
                                conv_buf, h0, lw['conv_w'], lw['conv_b'], lw['dt_bias'], lw['a_log'],
                                lw['d_skip'], lw['g_ssm_norm'], ssm_dtype)
    new_conv = jnp.concatenate([conv_buf, xbc], axis=1)[:, -(D_CONV - 1):] if seq < D_CONV - 1 \
        else xbc[:, seq - (D_CONV - 1):]
    x1 = _mix(o_att.reshape(t, -1), y_ssm.reshape(t, -1), proj['gate'], x2, mod,
              lw['w_branch_a'], lw['w_branch_b'], lw['w_out'], lw['g_post_mix'], tm)
    return x1, mod, new_k, new_v, new_conv, h_last


def _split_in_proj(w_in, d, d_inner, conv_dim, heads):
    sizes = (N_DIL * ATT_WIDTH, ATT_WIDTH, ATT_WIDTH, d_inner, conv_dim, heads, d, d)
    offs = [0]
    for s in sizes:
        offs.append(offs[-1] + s)
    part = lambda i, j=None: w_in[:, offs[i]:offs[(i if j is None else j) + 1]].astype(BF16)
    w_dt = jnp.pad(part(5), ((0, 0), (0, LANES - heads)))
    return {'q': (part(0), F32), 'k': (part(1), F32), 'v': (part(2), F32), 'z': (part(3), F32),
            'xbc': (part(4), F32), 'dt': (w_dt, F32), 'gate': (part(6, 7), F32)}


def kernel(x_prompt, x_sample, cache_win_k, cache_win_v, state_conv, state_ssm, c_prompt, c_sample, rel_bias, w_mod, b_mod, g_pre_mix, g_post_mix, g_pre_ffn, g_post_ffn, w_in, conv_w, conv_b, dt_bias, a_log, d_skip, g_ssm_norm, w_branch_a, w_branch_b, w_out, w_router, b_router, w_exp_gate, w_exp_up, w_exp_down, w_sh_gate, w_sh_up, w_sh_down):
    depth = w_mod.shape[0]
    bp, sp, d = x_prompt.shape
    bs, ss, _ = x_sample.shape
    d_inner = g_ssm_norm.shape[1]
    conv_dim = conv_w.shape[2]
    heads = dt_bias.shape[1]
    y_p, y_s = x_prompt, x_sample
    outs = [[] for _ in range(8)]
    for l in range(depth):
        lw = {
            'g_pre_mix': g_pre_mix[l], 'g_post_mix': g_post_mix[l],
            'g_pre_ffn': g_pre_ffn[l], 'g_post_ffn': g_post_ffn[l],
            'w_in': _split_in_proj(w_in[l], d, d_inner, conv_dim, heads),
            'conv_w': conv_w[l], 'conv_b': conv_b[l],
            'dt_bias': dt_bias[l], 'a_log': a_log[l], 'd_skip': d_skip[l], 'g_ssm_norm': g_ssm_norm[l],
            'w_branch_a': w_branch_a[l].astype(BF16), 'w_branch_b': w_branch_b[l].astype(BF16),
            'w_out': w_out[l].astype(BF16),
            'w_router_t': w_router[l].T.astype(BF16), 'b_router': b_router[l],
            'w_exp_gate': w_exp_gate[l], 'w_exp_up': w_exp_up[l], 'w_exp_down': w_exp_down[l],
            'w_sh_gate': w_sh_gate[l].astype(BF16), 'w_sh_up': w_sh_up[l].astype(BF16),
            'w_sh_down': w_sh_down[l].astype(BF16),
        }
        mod = _modulation(jnp.concatenate([c_prompt, c_sample], axis=0), w_mod[l], b_mod[l])
        conv0 = jnp.zeros((bp, D_CONV - 1, conv_dim), F32)
        h0 = jnp.zeros((bp, heads, SSM_HEAD_DIM, D_STATE), F32)
        tms = (512, 256)
        x1_p, mod_p, *state_p = _token_mixer(
            y_p, mod[:bp], functools.partial(_prompt_attend, rel_bias=rel_bias), conv0, h0, lw, tms[0], BF16)
        x1_s, mod_s, *state_s = _token_mixer(
            y_s, mod[bp:],
            functools.partial(_sample_attend, buf_k=cache_win_k[l], buf_v=cache_win_v[l], rel_bias=rel_bias),
            state_conv[l], state_ssm[l], lw, tms[1], F32)
        for o, val in zip(outs, state_p + state_s):
            o.append(val)
        y_p, y_s = _channel_mixer((x1_p, x1_s), (mod_p, mod_s), tms, lw)
        y_p = y_p.reshape(bp, sp, d)
        y_s = y_s.reshape(bs, ss, d)
    return (y_p, y_s) + tuple(jnp.stack(o) for o in outs)


def _prompt_attend(q, k, v, rel_bias):
    b, seq, _ = k.shape
    o = _attn_prompt(q, k, v, rel_bias)
    n_keep = min(MAX_WINDOW, seq)
    shape = (b, n_keep, ATT_HEADS, HEAD_DIM)
    return o, k[:, seq - n_keep:].reshape(shape), v[:, seq - n_keep:].reshape(shape)


def _sample_attend(q, k, v, buf_k, buf_v, rel_bias):
    return _attn_sample(q, k, v, buf_k, buf_v, rel_bias)
```

```python
import functools
import math

import jax
import jax.numpy as jnp
from jax import lax
from jax.experimental import pallas as pl
from jax.experimental.pallas import tpu as pltpu
from jax.experimental.pallas import tpu_sc as plsc

F32 = jnp.float32
BF16 = jnp.bfloat16

DIL_PATTERNS = ((128, 1), (512, 4), (2048, 16))
N_DIL = len(DIL_PATTERNS)
ATT_HEADS = 8
HEAD_DIM = 128
ATT_WIDTH = ATT_HEADS * HEAD_DIM
ATT_SCALE = HEAD_DIM ** -0.5
MAX_WINDOW = 2048
NUM_BUCKETS = 32
MAX_DISTANCE = 2048
SSM_HEAD_DIM = 64
SSM_GROUPS = 4
D_STATE = 128
D_CONV = 4
SSD_CHUNK = 128
TOP_K = 8
N_EXPERT_GROUPS = 8
TOPK_GROUPS = 4
ROUTE_SCALE = 2.5
RMS_EPS = 1e-6

LANES = 128
SUBLANES = 8
VMEM_LIMIT = 56 * 1024 * 1024
EXPERT_ROWS = 256
SC_CORES = 2
SC_SUBCORES = 16
SC_WORKERS = SC_CORES * SC_SUBCORES
NT_DIMS = (((1,), (1,)), ((), ()))


def _params(*sem):
    return pltpu.CompilerParams(dimension_semantics=sem, vmem_limit_bytes=VMEM_LIMIT)


def _silu(x):
    return x * jax.nn.sigmoid(x)


def _rms_rows(x, g):
    return x * lax.rsqrt(jnp.mean(x * x, axis=-1, keepdims=True) + RMS_EPS) * g


def _pack_pairs(x):
    n = x.shape[1] // 2
    bits = pltpu.bitcast(x.astype(BF16).astype(F32), jnp.uint32)
    return bits[:, :n] | (bits[:, n:] >> jnp.uint32(16))


def _unpack_pairs(p):
    hi = pltpu.bitcast(p & jnp.uint32(0xFFFF0000), F32)
    lo = pltpu.bitcast(p << jnp.uint32(16), F32)
    return jnp.concatenate([hi, lo], axis=1)


def _mod_kernel(c_ref, w_ref, b_ref, o_ref):
    s = _silu(c_ref[...]).astype(BF16)
    o_ref[...] = jnp.dot(s, w_ref[...].astype(BF16), preferred_element_type=F32) + b_ref[...]


def _modulation(c, w_mod, b_mod):
    m, d = c.shape
    n = w_mod.shape[1]
    tn = n // 4
    return pl.pallas_call(
        _mod_kernel,
        grid=(n // tn,),
        in_specs=[pl.BlockSpec((m, d), lambda j: (0, 0)),
                  pl.BlockSpec((d, tn), lambda j: (0, j)),
                  pl.BlockSpec((1, tn), lambda j: (0, j))],
        out_specs=pl.BlockSpec((m, tn), lambda j: (0, j)),
        out_shape=jax.ShapeDtypeStruct((m, n), F32),
        compiler_params=_params("arbitrary"),
        name="modulation",
    )(c, w_mod, b_mod.reshape(1, n))


class _Mod:
    def __init__(self, mod, seq, tm, d):
        self.d = d
        b = mod.shape[0]
        if seq % tm == 0:
            per = seq // tm
            self.arr = mod.reshape(b, 1, mod.shape[1])
            self._spec = lambda col: pl.BlockSpec((None, 1, d), lambda i: (i // per, 0, col))
        else:
            self.arr = jnp.repeat(mod, seq, axis=0)
            self._spec = lambda col: pl.BlockSpec((tm, d), lambda i: (i, col))

    def spec(self, col):
        return self._spec(col)


def _prenorm_kernel(x_ref, g_ref, sc_ref, sh_ref, o_ref):
    y = _rms_rows(x_ref[...], g_ref[...])
    o_ref[...] = (y * (1 + sc_ref[...]) + sh_ref[...]).astype(o_ref.dtype)


def _prenorm(x2, g, mod, col_scale, col_shift, tm):
    t, d = x2.shape
    return pl.pallas_call(
        _prenorm_kernel,
        grid=(t // tm,),
        in_specs=[pl.BlockSpec((tm, d), lambda i: (i, 0)),
                  pl.BlockSpec((1, d), lambda i: (0, 0)),
                  mod.spec(col_scale), mod.spec(col_shift)],
        out_specs=pl.BlockSpec((tm, d), lambda i: (i, 0)),
        out_shape=jax.ShapeDtypeStruct((t, d), BF16),
        compiler_params=_params("parallel"),
        name="prenorm",
    )(x2, g.reshape(1, d), mod.arr, mod.arr)


def _mm_kernel(h_ref, w_ref, o_ref):
    o_ref[...] = jnp.dot(h_ref[...], w_ref[...], preferred_element_type=F32).astype(o_ref.dtype)


def _matmul(h, w, out_dtype, tm, name):
    t, k = h.shape
    n = w.shape[1]
    return pl.pallas_call(
        _mm_kernel,
        grid=(t // tm,),
        in_specs=[pl.BlockSpec((tm, k), lambda i: (i, 0)),
                  pl.BlockSpec((k, n), lambda i: (0, 0))],
        out_specs=pl.BlockSpec((tm, n), lambda i: (i, 0)),
        out_shape=jax.ShapeDtypeStruct((t, n), out_dtype),
        compiler_params=_params("parallel"),
        name=name,
    )(h, w)


def _bucket(dist):
    max_exact = NUM_BUCKETS // 2
    far = max_exact + (jnp.log(jnp.maximum(dist, 1).astype(F32) / max_exact)
                       / math.log(MAX_DISTANCE / max_exact) * (NUM_BUCKETS - max_exact)).astype(jnp.int32)
    return jnp.where(dist < max_exact, dist, jnp.minimum(far, NUM_BUCKETS - 1))


def _bias_by_distance(rel_bias, n):
    hit = _bucket(jnp.arange(n, dtype=jnp.int32))[:, None] == jnp.arange(NUM_BUCKETS)[None, :]
    return jnp.sum(jnp.where(hit[:, :, None], rel_bias.astype(F32)[None], 0.0), axis=1)


def _band_bias(rel_bias, w):
    by_dist = _bias_by_distance(rel_bias, max(win for win, _ in DIL_PATTERNS) + 1)
    ext = 3 * w
    out = []
    for g, (_, dil) in enumerate(DIL_PATTERNS):
        vec = by_dist[::dil][:w + 1, g * ATT_HEADS:(g + 1) * ATT_HEADS].T
        v = jnp.concatenate([vec[:, ::-1], jnp.full((ATT_HEADS, ext - w - 1), -jnp.inf, F32)], axis=1)
        skew = jnp.tile(v, (1, w))[:, :w * (ext - 1)].reshape(ATT_HEADS, w, ext - 1)
        out.append(skew[:, :, :2 * w])
    return jnp.concatenate(out, axis=0)


def _attn_prompt_kernel(q0_ref, q1_ref, q2_ref, k_ref, v_ref, b0_ref, b1_ref, b2_ref, o_ref,
                        og_scr, lse_scr, *, seq, w):
    q_refs = (q0_ref, q1_ref, q2_ref)
    b_refs = (b0_ref, b1_ref, b2_ref)

    def rows(start, size, dil):
        return pl.ds(start, size) if dil == 1 else pl.ds(start, size, stride=dil)

    for g, (_, dil) in enumerate(DIL_PATTERNS):
        nb = seq // dil // w
        for r in range(dil):
            for n in range(nb):
                q_rows = rows(r + dil * n * w, w, dil)
                qb = q_refs[g][q_rows, :].astype(BF16)
                if n == 0:
                    k_rows = rows(r, w, dil)
                    bias = b_refs[g][:, w:]
                else:
                    k_rows = rows(r + dil * (n - 1) * w, 2 * w, dil)
                    bias = b_refs[g][...]
                kb = k_ref[k_rows, :].astype(BF16)
                vb = v_ref[k_rows, :].astype(BF16)
                s = lax.dot_general(qb, kb, NT_DIMS, preferred_element_type=F32) * ATT_SCALE + bias
                m = jnp.max(s, axis=-1, keepdims=True)
                p = jnp.exp(s - m)
                den = jnp.sum(p, axis=-1, keepdims=True)
                o = jnp.dot(p.astype(BF16), vb, preferred_element_type=F32) / den
                og_scr[g, q_rows, :] = o
                lse_scr[g, q_rows, :] = jnp.broadcast_to(m + jnp.log(den), (w, LANES))

    step = 256
    for c in range(seq // step):
        sl = pl.ds(c * step, step)
        l0, l1, l2 = lse_scr[0, sl, :], lse_scr[1, sl, :], lse_scr[2, sl, :]
        mm = jnp.maximum(jnp.maximum(l0, l1), l2)
        e0, e1, e2 = jnp.exp(l0 - mm), jnp.exp(l1 - mm), jnp.exp(l2 - mm)
        tot = e0 + e1 + e2
        o = (e0 / tot) * og_scr[0, sl, :] + (e1 / tot) * og_scr[1, sl, :] + (e2 / tot) * og_scr[2, sl, :]
        o_ref[sl, :] = o.astype(o_ref.dtype)


def _attn_prompt(q, k, v, rel_bias):
    b, seq, _ = k.shape
    w = DIL_PATTERNS[0][0] // DIL_PATTERNS[0][1]
    for win, dil in DIL_PATTERNS:
        assert win // dil == w and seq % (dil * w) == 0
    bias = _band_bias(rel_bias, w)
    hd = HEAD_DIM

    def q_spec(g):
        return pl.BlockSpec((None, seq, hd), lambda i, h: (i, 0, g * ATT_HEADS + h))

    def b_spec(g):
        return pl.BlockSpec((None, w, 2 * w), lambda i, h: (g * ATT_HEADS + h, 0, 0))

    kv_spec = pl.BlockSpec((None, seq, hd), lambda i, h: (i, 0, h))
    return pl.pallas_call(
        functools.partial(_attn_prompt_kernel, seq=seq, w=w),
        grid=(b, ATT_HEADS),
        in_specs=[q_spec(0), q_spec(1), q_spec(2), kv_spec, kv_spec, b_spec(0), b_spec(1), b_spec(2)],
        out_specs=pl.BlockSpec((None, seq, hd), lambda i, h: (i, 0, h)),
        out_shape=jax.ShapeDtypeStruct((b, seq, ATT_WIDTH), BF16),
        scratch_shapes=[pltpu.VMEM((N_DIL, seq, hd), F32), pltpu.VMEM((N_DIL, seq, LANES), F32)],
        compiler_params=_params("parallel", "parallel"),
        name="attn_prompt",
    )(q, q, q, k, v, bias, bias, bias)


SAMPLE_KEY_CHUNK = 512


def _attn_sample_kernel(q_ref, kn_ref, vn_ref, kn4_ref, vn4_ref, ck_ref, cv_ref, bm_ref, bn_ref,
                        o_ref, wk_ref, wv_ref, m_scr, l_scr, acc_scr, ck_scr, cv_scr, *, t, nch, n_full):
    c = pl.program_id(1)
    ch = ck_ref.shape[0]
    tr = t * ATT_HEADS
    rows = N_DIL * t

    @pl.when(c == 0)
    def _():
        m_scr[...] = jnp.full(m_scr.shape, -jnp.inf, F32)
        l_scr[...] = jnp.zeros(l_scr.shape, F32)
        acc_scr[...] = jnp.zeros(acc_scr.shape, F32)
        ck_scr[...] = kn4_ref[...]
        cv_scr[...] = vn4_ref[...]

    wk_ref[0:ch - tr, :] = ck_ref[tr:ch, :]
    wk_ref[ch - tr:ch, :] = ck_scr[...]
    ck_scr[...] = ck_ref[0:tr, :]
    wv_ref[0:ch - tr, :] = cv_ref[tr:ch, :]
    wv_ref[ch - tr:ch, :] = cv_scr[...]
    cv_scr[...] = cv_ref[0:tr, :]

    def head_q(h):
        return jnp.concatenate(
            [q_ref[:, g * ATT_WIDTH + h * HEAD_DIM:g * ATT_WIDTH + (h + 1) * HEAD_DIM] for g in range(N_DIL)], axis=0)

    def attend(h, r0):
        rs = slice(r0, rows)
        qa = head_q(h)[r0:].astype(BF16)
        head_rows = pl.ds(h, ch // ATT_HEADS, stride=ATT_HEADS)
        kh = ck_ref[head_rows, :].astype(BF16)
        vh = cv_ref[head_rows, :].astype(BF16)
        s = lax.dot_general(qa, kh, NT_DIMS, preferred_element_type=F32) * ATT_SCALE + bm_ref[h, rs, :]
        m_old = m_scr[h, rs, :]
        m_new = jnp.maximum(m_old, jnp.max(s, axis=-1, keepdims=True))
        alpha = jnp.exp(m_old - m_new)
        p = jnp.exp(s - m_new[:, 0:1])
        l_scr[h, rs, :] = alpha * l_scr[h, rs, :] + jnp.sum(p, axis=-1, keepdims=True)
        acc_scr[h, rs, :] = alpha * acc_scr[h, rs, :] + jnp.dot(p.astype(BF16), vh, preferred_element_type=F32)
        m_scr[h, rs, :] = m_new

    @pl.when(c < n_full)
    def _():
        for h in range(ATT_HEADS):
            attend(h, 0)

    @pl.when(c >= n_full)
    def _():
        for h in range(ATT_HEADS):
            attend(h, rows - t)

    @pl.when(c == nch - 1)
    def _():
        for h in range(ATT_HEADS):
            cs = slice(h * HEAD_DIM, (h + 1) * HEAD_DIM)
            qa = head_q(h)
            s_new = [jnp.sum(qa * kn_ref[j:j + 1, cs], axis=-1, keepdims=True) * ATT_SCALE + bn_ref[h, j]
                     for j in range(t)]
            m_old = m_scr[h]
            m_new = m_old
            for sj in s_new:
                m_new = jnp.maximum(m_new, sj)
            alpha = jnp.exp(m_old - m_new)
            den = alpha * l_scr[h]
            acc = alpha * acc_scr[h]
            for j, sj in enumerate(s_new):
                pj = jnp.exp(sj - m_new)
                den = den + pj
                acc = acc + pj * vn_ref[j:j + 1, cs]
            o = acc / den
            lse = m_new + jnp.log(den)
            l0, l1, l2 = lse[0:t], lse[t:2 * t], lse[2 * t:rows]
            mm = jnp.maximum(jnp.maximum(l0, l1), l2)
            e0, e1, e2 = jnp.exp(l0 - mm), jnp.exp(l1 - mm), jnp.exp(l2 - mm)
            tot = e0 + e1 + e2
            o_ref[:, cs] = (e0 / tot) * o[0:t] + (e1 / tot) * o[t:2 * t] + (e2 / tot) * o[2 * t:rows]


def _sample_bias(rel_bias, t, n_buf):
    n_key = n_buf + t
    by_dist = _bias_by_distance(rel_bias, n_key)
    dist = jnp.arange(n_key)
    full = []
    for g, (win, dil) in enumerate(DIL_PATTERNS):
        ok = (dist % dil == 0) & (dist <= win)
        vec = jnp.where(ok[:, None], by_dist[:, g * ATT_HEADS:(g + 1) * ATT_HEADS], -jnp.inf)
        rev = jnp.concatenate([vec[::-1], jnp.full((t - 1, ATT_HEADS), -jnp.inf, F32)], axis=0)
        full.append(jnp.stack([rev[t - 1 - tok:t - 1 - tok + n_key] for tok in range(t)], axis=0))
    full = jnp.transpose(jnp.concatenate(full, axis=0), (2, 0, 1))
    new = jnp.transpose(full[:, :, n_buf:], (0, 2, 1))
    new = jnp.broadcast_to(new[..., None], new.shape + (LANES,))
    return full[:, :, :n_buf], new


def _attn_sample(q, k, v, cache_k, cache_v, rel_bias):
    b, t, _ = k.shape
    n_buf = cache_k.shape[1]
    ch = SAMPLE_KEY_CHUNK
    assert t == SUBLANES and n_buf % ch == 0
    assert all(win <= n_buf for win, _ in DIL_PATTERNS)
    nch = n_buf // ch
    assert all(win <= DIL_PATTERNS[-1][0] for win, _ in DIL_PATTERNS)
    n_full = min(nch, max(-(-win // ch) for win, _ in DIL_PATTERNS[:-1]))
    bm, bn = _sample_bias(rel_bias, t, n_buf)
    flat = lambda a: a.reshape(b, -1, HEAD_DIM)
    rows = N_DIL * t
    tr = t * ATT_HEADS

    def per_b(*shape):
        return pl.BlockSpec((None,) + shape, lambda i, c: (i,) + (0,) * len(shape))

    buf_spec = pl.BlockSpec((None, ch * ATT_HEADS, HEAD_DIM), lambda i, c: (i, nch - 1 - c, 0))
    buf_shape = jax.ShapeDtypeStruct((b, n_buf * ATT_HEADS, HEAD_DIM), F32)
    o, win_k, win_v = pl.pallas_call(
        functools.partial(_attn_sample_kernel, t=t, nch=nch, n_full=n_full),
        grid=(b, nch),
        in_specs=[per_b(t, N_DIL * ATT_WIDTH), per_b(t, ATT_WIDTH), per_b(t, ATT_WIDTH),
                  per_b(tr, HEAD_DIM), per_b(tr, HEAD_DIM), buf_spec, buf_spec,
                  pl.BlockSpec((ATT_HEADS, rows, ch), lambda i, c: (0, 0, nch - 1 - c)),
                  pl.BlockSpec((ATT_HEADS, t, rows, LANES), lambda i, c: (0, 0, 0, 0))],
        out_specs=[per_b(t, ATT_WIDTH), buf_spec, buf_spec],
        out_shape=[jax.ShapeDtypeStruct((b, t, ATT_WIDTH), F32), buf_shape, buf_shape],
        scratch_shapes=[pltpu.VMEM((ATT_HEADS, rows, LANES), F32), pltpu.VMEM((ATT_HEADS, rows, LANES), F32),
                        pltpu.VMEM((ATT_HEADS, rows, HEAD_DIM), F32),
                        pltpu.VMEM((tr, HEAD_DIM), F32), pltpu.VMEM((tr, HEAD_DIM), F32)],
        compiler_params=_params("parallel", "arbitrary"),
        name="attn_sample",
    )(q, k, v, flat(k), flat(v), flat(cache_k), flat(cache_v), bm, bn)
    return o, win_k.reshape(cache_k.shape), win_v.reshape(cache_v.shape)


def _softplus(x):
    return jnp.maximum(x, 0.0) + jnp.log(1.0 + jnp.exp(-jnp.abs(x)))


def _ssd_kernel(xbc_ref, z_ref, dt_ref, cbuf_ref, h0_ref, cw_ref, cb_ref, dtb_ref, alog_ref, dsk_ref, gn_ref,
                y_ref, hl_ref, ext_scr, st_scr, y_scr, acst_scr, dtt_scr, wstt_scr, *, lv, nc, d_inner):
    q = SSD_CHUNK
    n = D_STATE
    c = pl.program_id(1)
    n_pairs = d_inner // LANES
    pairs_per_group = n_pairs // SSM_GROUPS

    @pl.when(c == 0)
    def _():
        ext_scr[0:SUBLANES, :] = cbuf_ref[...]
        for i in range(n_pairs):
            st_scr[:, i * LANES:(i + 1) * LANES] = h0_ref[i * LANES:(i + 1) * LANES, :].T

    ext_scr[SUBLANES:SUBLANES + lv, :] = xbc_ref[...]
    if lv < q:
        ext_scr[SUBLANES + lv:SUBLANES + q, :] = jnp.zeros((q - lv, ext_scr.shape[1]), F32)
    first = SUBLANES - (D_CONV - 1)
    u = cb_ref[...] + cw_ref[0:1, :] * ext_scr[first:first + q, :]
    for j in range(1, D_CONV):
        u = u + cw_ref[j:j + 1, :] * ext_scr[first + j:first + j + q, :]
    u = _silu(u)
    if nc > 1:
        ext_scr[0:SUBLANES, :] = ext_scr[q:q + SUBLANES, :]

    dtv = _softplus(dt_ref_rows(dt_ref, lv, q) + dtb_ref[...])
    if lv < q:
        row = lax.broadcasted_iota(jnp.int32, (q, LANES), 0)
        dtv = jnp.where(row < lv, dtv, 0.0)
    a = -jnp.exp(alog_ref[...])
    da = dtv * a
    li = lax.broadcasted_iota(jnp.int32, (q, q), 0)
    si = lax.broadcasted_iota(jnp.int32, (q, q), 1)
    tri = li >= si
    acs = jnp.dot(tri.astype(F32), da, preferred_element_type=F32, precision=lax.Precision.HIGHEST)
    acs_last = acs[q - 1:q, :]
    acst_scr[...] = acs.T
    dtt_scr[...] = dtv.T
    wstt_scr[...] = (jnp.exp(acs_last - acs) * dtv).T

    lane = lax.broadcasted_iota(jnp.int32, (1, LANES), 1)
    left = lane < SSM_HEAD_DIM
    for g in range(SSM_GROUPS):
        bg = u[:, d_inner + g * n:d_inner + (g + 1) * n]
        cg = u[:, d_inner + (SSM_GROUPS + g) * n:d_inner + (SSM_GROUPS + g + 1) * n].astype(BF16)
        cb = lax.dot_general(cg, bg.astype(BF16), NT_DIMS, preferred_element_type=F32)
        bgt = bg.T
        for jp in range(pairs_per_group):
            pair = g * pairs_per_group + jp
            cols = slice(pair * LANES, (pair + 1) * LANES)
            x_pair = u[:, cols]
            lhs, e_col, dec = [], [], []
            for h in (2 * pair, 2 * pair + 1):
                a_col = jnp.broadcast_to(acs[:, h:h + 1], (q, q))
                seg = a_col - acst_scr[h:h + 1, :]
                m_h = cb * jnp.exp(jnp.where(tri, seg, -jnp.inf)) * dtt_scr[h:h + 1, :]
                lhs.append(m_h.astype(BF16))
                e_col.append(jnp.exp(a_col))
                dec.append(jnp.exp(a_col[q - 1:q, :]))
            for h in (2 * pair, 2 * pair + 1):
                lhs.append((bgt * wstt_scr[h:h + 1, :]).astype(BF16))
            res = jnp.dot(jnp.concatenate(lhs, axis=0), x_pair.astype(BF16), preferred_element_type=F32)
            y_diag = jnp.where(left, res[0:q], res[q:2 * q])
            d_state = jnp.where(left, res[2 * q:2 * q + n], res[2 * q + n:2 * q + 2 * n])
            st = st_scr[:, cols]
            y_off = jnp.dot(cg, st.astype(BF16), preferred_element_type=F32) * jnp.where(left, e_col[0], e_col[1])
            st_scr[:, cols] = st * jnp.where(left, dec[0], dec[1]) + d_state
            y_scr[:, cols] = y_diag + y_off + dsk_ref[:, cols] * x_pair

    gw = d_inner // SSM_GROUPS
    for g in range(SSM_GROUPS):
        cols = slice(g * gw, (g + 1) * gw)
        yg = y_scr[0:lv, cols] * _silu(z_ref[:, cols])
        yg = yg * lax.rsqrt(jnp.mean(yg * yg, axis=-1, keepdims=True) + RMS_EPS)
        y_ref[:, cols] = (yg * gn_ref[:, cols]).astype(y_ref.dtype)

    @pl.when(c == nc - 1)
    def _():
        for i in range(n_pairs):
            hl_ref[i * LANES:(i + 1) * LANES, :] = st_scr[:, i * LANES:(i + 1) * LANES].T


def dt_ref_rows(dt_ref, lv, q):
    if lv == q:
        return dt_ref[...]
    return jnp.concatenate([dt_ref[...], jnp.zeros((q - lv, dt_ref.shape[1]), F32)], axis=0)


def _ssd_branch(xbc, z, dt, conv_buf, h0, conv_w, conv_b, dt_bias, a_log, d_skip, g_norm, out_dtype):
    b, seqlen, conv_dim = xbc.shape
    d_inner = z.shape[2]
    heads = d_inner // SSM_HEAD_DIM
    q = SSD_CHUNK
    lv = min(q, seqlen)
    assert seqlen % lv == 0 and lv % SUBLANES == 0 and heads <= LANES
    nc = seqlen // lv
    cbuf = jnp.pad(conv_buf, ((0, 0), (SUBLANES - (D_CONV - 1), 0), (0, 0)))
    h0f = h0.reshape(b, heads * SSM_HEAD_DIM, D_STATE)
    pad = LANES - heads
    row = lambda a: jnp.pad(a.astype(F32), (0, pad)).reshape(1, LANES)
    dsk = jnp.repeat(d_skip.astype(F32), SSM_HEAD_DIM).reshape(1, d_inner)

    def per_chunk(width):
        return pl.BlockSpec((None, lv, width), lambda i, c: (i, c, 0))

    def per_batch(r, width):
        return pl.BlockSpec((None, r, width), lambda i, c: (i, 0, 0))

    def const(r, width):
        return pl.BlockSpec((r, width), lambda i, c: (0, 0))

    y, h_last = pl.pallas_call(
        functools.partial(_ssd_kernel, lv=lv, nc=nc, d_inner=d_inner),
        grid=(b, nc),
        in_specs=[per_chunk(conv_dim), per_chunk(d_inner), per_chunk(LANES),
                  per_batch(SUBLANES, conv_dim), per_batch(heads * SSM_HEAD_DIM, D_STATE),
                  const(D_CONV, conv_dim), const(1, conv_dim), const(1, LANES), const(1, LANES),
                  const(1, d_inner), const(1, d_inner)],
        out_specs=[per_chunk(d_inner), per_batch(heads * SSM_HEAD_DIM, D_STATE)],
        out_shape=[jax.ShapeDtypeStruct((b, seqlen, d_inner), out_dtype),
                   jax.ShapeDtypeStruct((b, heads * SSM_HEAD_DIM, D_STATE), F32)],
        scratch_shapes=[pltpu.VMEM((q + 2 * SUBLANES, conv_dim), F32),
                        pltpu.VMEM((D_STATE, d_inner), F32),
                        pltpu.VMEM((q, d_inner), F32),
                        pltpu.VMEM((LANES, q), F32), pltpu.VMEM((LANES, q), F32), pltpu.VMEM((LANES, q), F32)],
        compiler_params=_params("parallel", "arbitrary"),
        name="ssd",
    )(xbc, z, dt, cbuf, h0f, conv_w, conv_b.reshape(1, conv_dim), row(dt_bias), row(a_log), dsk,
      g_norm.reshape(1, d_inner))
    return y, h_last.reshape(b, heads, SSM_HEAD_DIM, D_STATE)


def _mix_kernel(oa_ref, ys_ref, gate_ref, x_ref, gt_ref, wa_ref, wb_ref, wo_ref, g_ref, o_ref, *, d):
    ya = jnp.dot(oa_ref[...].astype(BF16), wa_ref[...], preferred_element_type=F32)
    yb = jnp.dot(ys_ref[...].astype(BF16), wb_ref[...], preferred_element_type=F32)
    mixed = jax.nn.sigmoid(gate_ref[:, 0:d]) * ya + jax.nn.sigmoid(gate_ref[:, d:2 * d]) * yb
    mix = jnp.dot(mixed.astype(BF16), wo_ref[...], preferred_element_type=F32)
    o_ref[...] = x_ref[...] + gt_ref[...] * _rms_rows(mix, g_ref[...])


def _mix(o_att, y_ssm, gates, x2, mod, wa, wb, wo, g_post, tm):
    t, d = x2.shape
    full = lambda a: pl.BlockSpec(a.shape, lambda i: (0, 0))
    rows = lambda a: pl.BlockSpec((tm, a.shape[1]), lambda i: (i, 0))
    g2 = g_post.reshape(1, d)
    return pl.pallas_call(
        functools.partial(_mix_kernel, d=d),
        grid=(t // tm,),
        in_specs=[rows(o_att), rows(y_ssm), rows(gates), rows(x2), mod.spec(2), full(wa), full(wb), full(wo),
                  full(g2)],
        out_specs=pl.BlockSpec((tm, d), lambda i: (i, 0)),
        out_shape=jax.ShapeDtypeStruct((t, d), F32),
        compiler_params=_params("parallel"),
        name="mix",
    )(o_att, y_ssm, gates, x2, mod.arr, wa, wb, wo, g2)


def _first_index(hit, iota, size):
    return jnp.min(jnp.where(hit, iota, size), axis=0, keepdims=True)


def _ffn_pre_kernel(x_ref, g_ref, sc_ref, sh_ref, wr_ref, br_ref, h_ref, idx_ref, w_ref, *, n_exp):
    y = _rms_rows(x_ref[...], g_ref[...])
    hf = y * (1 + sc_ref[...]) + sh_ref[...]
    h_ref[...] = _pack_pairs(hf)
    h = hf.astype(BF16)
    tm = h.shape[0]
    scores = jax.nn.sigmoid(lax.dot_general(wr_ref[...], h, NT_DIMS, preferred_element_type=F32))
    biased = scores + br_ref[...]
    gsz = n_exp // N_EXPERT_GROUPS
    riota = lax.broadcasted_iota(jnp.int32, (gsz, tm), 0)
    gs = []
    for g in range(N_EXPERT_GROUPS):
        xg = biased[g * gsz:(g + 1) * gsz, :]
        m1 = jnp.max(xg, axis=0, keepdims=True)
        i1 = _first_index(xg == m1, riota, gsz)
        m2 = jnp.max(jnp.where(riota == i1, -jnp.inf, xg), axis=0, keepdims=True)
        gs.append(m1 + m2)
    gs = jnp.concatenate(gs, axis=0)
    giota = lax.broadcasted_iota(jnp.int32, gs.shape, 0)
    keep = jnp.zeros(gs.shape, F32)
    for _ in range(TOPK_GROUPS):
        gi = _first_index(gs == jnp.max(gs, axis=0, keepdims=True), giota, N_EXPERT_GROUPS)
        sel = giota == gi
        keep = jnp.where(sel, 1.0, keep)
        gs = jnp.where(sel, -jnp.inf, gs)
    masked = jnp.concatenate(
        [jnp.where(keep[g:g + 1, :] > 0.5, biased[g * gsz:(g + 1) * gsz, :], -jnp.inf)
         for g in range(N_EXPERT_GROUPS)], axis=0)
    eiota = lax.broadcasted_iota(jnp.int32, (n_exp, tm), 0)
    idxs, ws = [], []
    for _ in range(TOP_K):
        ik = _first_index(masked == jnp.max(masked, axis=0, keepdims=True), eiota, n_exp)
        hit = eiota == ik
        idxs.append(ik)
        ws.append(jnp.sum(jnp.where(hit, scores, 0.0), axis=0, keepdims=True))
        masked = jnp.where(hit, -jnp.inf, masked)
    idx_ref[...] = jnp.concatenate(idxs, axis=0)
    w = jnp.concatenate(ws, axis=0)
    w_ref[...] = w / jnp.sum(w, axis=0, keepdims=True) * ROUTE_SCALE


def _ffn_pre(x1, g, mod, w_router_t, b_router, tm):
    t, d = x1.shape
    e = w_router_t.shape[0]
    br = jnp.broadcast_to(b_router.astype(F32)[:, None], (e, tm))
    return pl.pallas_call(
        functools.partial(_ffn_pre_kernel, n_exp=e),
        grid=(t // tm,),
        in_specs=[pl.BlockSpec((tm, d), lambda i: (i, 0)),
                  pl.BlockSpec((1, d), lambda i: (0, 0)),
                  mod.spec(4), mod.spec(3),
                  pl.BlockSpec((e, d), lambda i: (0, 0)),
                  pl.BlockSpec((e, tm), lambda i: (0, 0))],
        out_specs=[pl.BlockSpec((tm, d // 2), lambda i: (i, 0)),
                   pl.BlockSpec((TOP_K, tm), lambda i: (0, i)),
                   pl.BlockSpec((TOP_K, tm), lambda i: (0, i))],
        out_shape=[jax.ShapeDtypeStruct((t, d // 2), jnp.uint32), jax.ShapeDtypeStruct((TOP_K, t), jnp.int32),
                   jax.ShapeDtypeStruct((TOP_K, t), F32)],
        compiler_params=_params("parallel"),
        name="ffn_pre",
    )(x1, g.reshape(1, d), mod.arr, mod.arr, w_router_t, br)


def _rank_kernel(idx_ref, rank_ref, cnt_ref, base_scr, *, n_exp):
    i = pl.program_id(0)
    tm = idx_ref.shape[1]

    @pl.when(i == 0)
    def _():
        base_scr[...] = jnp.zeros(base_scr.shape, F32)

    eiota = lax.broadcasted_iota(jnp.int32, (n_exp, tm), 0)
    idx = idx_ref[...]
    hits = [eiota == idx[k:k + 1, :] for k in range(TOP_K)]
    multi = jnp.zeros((n_exp, tm), F32)
    for hit in hits:
        multi = multi + jnp.where(hit, 1.0, 0.0)
    multi = multi.astype(BF16)
    ti = lax.broadcasted_iota(jnp.int32, (tm, tm), 0)
    tj = lax.broadcasted_iota(jnp.int32, (tm, tm), 1)
    earlier = jnp.where(ti < tj, 1.0, 0.0).astype(BF16)
    before = jnp.dot(multi, earlier, preferred_element_type=F32) + base_scr[:, 0:1]
    ranks = [jnp.sum(jnp.where(hit, before, 0.0), axis=0, keepdims=True) for hit in hits]
    rank_ref[...] = jnp.concatenate(ranks, axis=0).astype(jnp.int32)
    base_scr[...] = base_scr[...] + jnp.dot(multi, jnp.ones((tm, LANES), BF16), preferred_element_type=F32)
    cnt_ref[...] = base_scr[...]


def _expert_ranks(idx_t, n_exp, tm):
    t = idx_t.shape[1]
    return pl.pallas_call(
        functools.partial(_rank_kernel, n_exp=n_exp),
        grid=(t // tm,),
        in_specs=[pl.BlockSpec((TOP_K, tm), lambda i: (0, i))],
        out_specs=[pl.BlockSpec((TOP_K, tm), lambda i: (0, i)), pl.BlockSpec((n_exp, LANES), lambda i: (0, 0))],
        out_shape=[jax.ShapeDtypeStruct((TOP_K, t), jnp.int32), jax.ShapeDtypeStruct((n_exp, LANES), F32)],
        scratch_shapes=[pltpu.VMEM((n_exp, LANES), F32)],
        compiler_params=_params("arbitrary"),
        name="expert_ranks",
    )(idx_t)


def _pos_kernel(idx_ref, rank_ref, start_ref, pos_ref, *, n_exp):
    tm = idx_ref.shape[1]
    eiota = lax.broadcasted_iota(jnp.int32, (n_exp, tm), 0)
    idx = idx_ref[...]
    start = start_ref[:, 0:1]
    offs = [jnp.sum(jnp.where(eiota == idx[k:k + 1, :], start, 0.0), axis=0, keepdims=True) for k in range(TOP_K)]
    pos_ref[...] = rank_ref[...] + jnp.concatenate(offs, axis=0).astype(jnp.int32)


def _positions(idx_t, rank_t, pad_start, tm):
    t = idx_t.shape[1]
    n_exp = pad_start.shape[0]
    start = jnp.broadcast_to(pad_start.astype(F32)[:, None], (n_exp, LANES))
    spec = pl.BlockSpec((TOP_K, tm), lambda i: (0, i))
    return pl.pallas_call(
        functools.partial(_pos_kernel, n_exp=n_exp),
        grid=(t // tm,),
        in_specs=[spec, spec, pl.BlockSpec((n_exp, LANES), lambda i: (0, 0))],
        out_specs=spec,
        out_shape=jax.ShapeDtypeStruct((TOP_K, t), jnp.int32),
        compiler_params=_params("parallel"),
        name="positions",
    )(idx_t, rank_t, start)


def _experts_kernel(first_ref, count_ref, used_ref, x_hbm, wg_ref, wu_ref, wd_ref, o_hbm,
                    wg_scr, wu_scr, wd_scr, x_vmem, o_vmem, x_sem, o_sem):
    e = pl.program_id(0)
    blk = x_vmem.shape[1]
    first, count, used = first_ref[e], count_ref[e], used_ref[0]

    def block_rows(g):
        return pl.ds(pl.multiple_of(g * blk, blk), blk)

    def x_copy(g, slot):
        return pltpu.make_async_copy(x_hbm.at[block_rows(g), :], x_vmem.at[slot], x_sem.at[slot])

    def o_copy(g, slot):
        return pltpu.make_async_copy(o_vmem.at[slot], o_hbm.at[block_rows(g), :], o_sem.at[slot])

    @pl.when(jnp.logical_and(e == 0, used > 0))
    def _():
        x_copy(0, 0).start()

    @pl.when(count > 0)
    def _():
        wg_scr[...] = wg_ref[...].astype(BF16)
        wu_scr[...] = wu_ref[...].astype(BF16)
        wd_scr[...] = wd_ref[...].astype(BF16)

    def one_block(j, carry):
        g = first + j
        slot = lax.rem(g, 2)
        x_copy(g, slot).wait()

        @pl.when(g + 1 < used)
        def _():
            x_copy(g + 1, 1 - slot).start()

        xb = _unpack_pairs(x_vmem[slot]).astype(BF16)
        act = _silu(jnp.dot(xb, wg_scr[...], preferred_element_type=F32)) * jnp.dot(
            xb, wu_scr[...], preferred_element_type=F32)
        res = _pack_pairs(jnp.dot(act.astype(BF16), wd_scr[...], preferred_element_type=F32))

        @pl.when(g >= 2)
        def _():
            o_copy(g - 2, slot).wait()

        o_vmem[slot] = res
        o_copy(g, slot).start()
        return carry

    lax.fori_loop(0, count, one_block, 0)

    @pl.when(e == pl.num_programs(0) - 1)
    def _():
        for back in (2, 1):
            @pl.when(used >= back)
            def _():
                o_copy(used - back, lax.rem(used - back, 2)).wait()


def _sc_dispatch(h, pos_flat, rows):
    t, w = h.shape
    chunk, nbuf = 48, 2
    per_w = t // SC_WORKERS
    assert t % SC_WORKERS == 0 and per_w % (chunk * nbuf) == 0
    n = per_w // chunk
    mesh = plsc.VectorSubcoreMesh(core_axis_name="c", subcore_axis_name="s")

    @functools.partial(pl.kernel, mesh=mesh, out_type=jax.ShapeDtypeStruct((rows, w), h.dtype),
                       scratch_types=[pltpu.VMEM((nbuf, TOP_K, chunk), jnp.int32),
                                      pltpu.VMEM((nbuf, chunk, w), h.dtype),
                                      pltpu.SemaphoreType.DMA((nbuf,))])
    def scatter_rows(h_hbm, pos_hbm, x_hbm, idx_v, rows_v, sem):
        w0 = (lax.axis_index("s") * SC_CORES + lax.axis_index("c")) * per_w

        def scatter(slot, kk):
            return pltpu.make_async_copy(rows_v.at[slot], x_hbm.at[idx_v.at[slot, kk]], sem.at[slot])

        @pl.loop(0, n // nbuf)
        def _(g):
            for s in range(nbuf):
                @pl.when(g > 0)
                def _():
                    for kk in range(TOP_K):
                        scatter(s, kk).wait()
                base = pl.multiple_of(w0 + (g * nbuf + s) * chunk, SUBLANES)
                pltpu.sync_copy(h_hbm.at[pl.ds(base, chunk)], rows_v.at[s])
                for kk in range(TOP_K):
                    pltpu.sync_copy(pos_hbm.at[pl.ds(pl.multiple_of(kk * t + base, SUBLANES), chunk)],
                                    idx_v.at[s, kk])
                for kk in range(TOP_K):
                    scatter(s, kk).start()

        for s in range(nbuf):
            for kk in range(TOP_K):
                scatter(s, kk).wait()

    return scatter_rows(h, pos_flat)


def _sc_gather(table, idx_flat):
    n_idx = idx_flat.shape[0]
    w = table.shape[1]
    chunk, nbuf = 48, 4
    per_w = n_idx // SC_WORKERS
    assert n_idx % SC_WORKERS == 0 and per_w % (chunk * nbuf) == 0
    n = per_w // chunk
    mesh = plsc.VectorSubcoreMesh(core_axis_name="c", subcore_axis_name="s")

    @functools.partial(pl.kernel, mesh=mesh, out_type=jax.ShapeDtypeStruct((n_idx, w), table.dtype),
                       scratch_types=[pltpu.VMEM((nbuf, chunk), jnp.int32),
                                      pltpu.VMEM((nbuf, chunk, w), table.dtype),
                                      pltpu.SemaphoreType.DMA((nbuf,))])
    def gather_rows(t_hbm, idx_hbm, o_hbm, idx_v, rows_v, sem):
        w0 = (lax.axis_index("s") * SC_CORES + lax.axis_index("c")) * per_w

        def gather(slot):
            return pltpu.make_async_copy(t_hbm.at[idx_v.at[slot]], rows_v.at[slot], sem.at[slot])

        def issue(j, slot):
            base = pl.multiple_of(w0 + j * chunk, SUBLANES)
            pltpu.sync_copy(idx_hbm.at[pl.ds(base, chunk)], idx_v.at[slot])
            gather(slot).start()

        for s in range(nbuf):
            issue(s, s)

        @pl.loop(0, n // nbuf)
        def _(g):
            for s in range(nbuf):
                j = g * nbuf + s
                gather(s).wait()
                pltpu.sync_copy(rows_v.at[s], o_hbm.at[pl.ds(pl.multiple_of(w0 + j * chunk, SUBLANES), chunk)])

                @pl.when(j + nbuf < n)
                def _():
                    issue(j + nbuf, s)

    return gather_rows(table, idx_flat)


def _routed_experts(h, idx_t, wg, wu, wd):
    kk, t = idx_t.shape
    dp = h.shape[1]
    n_exp, d, de = wg.shape
    blk = EXPERT_ROWS
    n_assign = t * kk
    tm = 512
    assert t % tm == 0
    rank_t, cnt = _expert_ranks(idx_t, n_exp, tm)
    counts = cnt[:, 0].astype(jnp.int32)
    padded = (counts + blk - 1) // blk * blk
    pad_end = jnp.cumsum(padded)
    pad_start = pad_end - padded
    pos_flat = _positions(idx_t, rank_t, pad_start, tm).reshape(-1)
    n_blocks = (n_assign + n_exp * (blk - 1)) // blk
    rows = n_blocks * blk
    first_blk = (pad_start // blk).astype(jnp.int32)
    count_blk = (padded // blk).astype(jnp.int32)
    n_used = (pad_end[-1] // blk).astype(jnp.int32).reshape(1)
    x_buf = _sc_dispatch(h, pos_flat, rows)
    w_in_spec = pl.BlockSpec((None, d, de), lambda e, *_: (e, 0, 0))
    grid_spec = pltpu.PrefetchScalarGridSpec(
        num_scalar_prefetch=3,
        grid=(n_exp,),
        in_specs=[pl.BlockSpec(memory_space=pl.ANY), w_in_spec, w_in_spec,
                  pl.BlockSpec((None, de, d), lambda e, *_: (e, 0, 0))],
        out_specs=pl.BlockSpec(memory_space=pl.ANY),
        scratch_shapes=[pltpu.VMEM((d, de), BF16), pltpu.VMEM((d, de), BF16), pltpu.VMEM((de, d), BF16),
                        pltpu.VMEM((2, blk, dp), jnp.uint32), pltpu.VMEM((2, blk, dp), jnp.uint32),
                        pltpu.SemaphoreType.DMA((2,)), pltpu.SemaphoreType.DMA((2,))],
    )
    out = pl.pallas_call(
        _experts_kernel,
        grid_spec=grid_spec,
        out_shape=jax.ShapeDtypeStruct((rows, dp), jnp.uint32),
        compiler_params=_params("arbitrary"),
        name="experts",
    )(first_blk, count_blk, n_used, x_buf, wg, wu, wd)
    return _sc_gather(out, pos_flat).reshape(kk, t, dp)


def _ffn_post_kernel(h_ref, r_ref, wt_ref, x_ref, gt_ref, wg_ref, wu_ref, wd_ref, g_ref, o_ref):
    h = _unpack_pairs(h_ref[...]).astype(BF16)
    act = _silu(jnp.dot(h, wg_ref[...], preferred_element_type=F32)) * jnp.dot(
        h, wu_ref[...], preferred_element_type=F32)
    f = jnp.dot(act.astype(BF16), wd_ref[...], preferred_element_type=F32)
    for k in range(TOP_K):
        f = f + _unpack_pairs(r_ref[k]) * wt_ref[:, k:k + 1]
    o_ref[...] = x_ref[...] + gt_ref[...] * _rms_rows(f, g_ref[...])


def _ffn_post(h2, routed, w_rows, row0, x1, mod, wg, wu, wd, g_post, tm):
    t, d = x1.shape
    assert row0 % tm == 0
    blk0 = row0 // tm
    full = lambda a: pl.BlockSpec(a.shape, lambda i: (0, 0))
    rows = lambda a: pl.BlockSpec((tm, a.shape[1]), lambda i: (i, 0))
    g2 = g_post.reshape(1, d)
    return pl.pallas_call(
        _ffn_post_kernel,
        grid=(t // tm,),
        in_specs=[rows(h2),
                  pl.BlockSpec((TOP_K, tm, d // 2), lambda i: (0, i + blk0, 0)),
                  pl.BlockSpec((tm, TOP_K), lambda i: (i + blk0, 0)),
                  rows(x1), mod.spec(5), full(wg), full(wu), full(wd), full(g2)],
        out_specs=pl.BlockSpec((tm, d), lambda i: (i, 0)),
        out_shape=jax.ShapeDtypeStruct((t, d), F32),
        compiler_params=_params("parallel"),
        name="ffn_post",
    )(h2, routed, w_rows, x1, mod.arr, wg, wu, wd, g2)


def _channel_mixer(x1s, mods, tms, lw):
    pre = [_ffn_pre(x1, lw['g_pre_ffn'], mod, lw['w_router_t'], lw['b_router'], tm)
           for x1, mod, tm in zip(x1s, mods, tms)]
    h2 = jnp.concatenate([p[0] for p in pre], axis=0)
    idx_t = jnp.concatenate([p[1] for p in pre], axis=1)
    w_rows = jnp.concatenate([p[2] for p in pre], axis=1).T
    routed = _routed_experts(h2, idx_t, lw['w_exp_gate'], lw['w_exp_up'], lw['w_exp_down'])
    outs, row0 = [], 0
    for x1, mod, tm, p in zip(x1s, mods, tms, pre):
        outs.append(_ffn_post(p[0], routed, w_rows, row0, x1, mod, lw['w_sh_gate'], lw['w_sh_up'],
                              lw['w_sh_down'], lw['g_post_ffn'], tm))
        row0 += x1.shape[0]
    return outs


def _token_mixer(x, mod_rows, attend, conv_buf, h0, lw, tm, ssm_dtype):
    b, seq, d = x.shape
    t = b * seq
    x2 = x.reshape(t, d)
    mod = _Mod(mod_rows, seq, tm, d)
    h = _prenorm(x2, lw['g_pre_mix'], mod, 1, 0, tm)
    proj = {name: _matmul(h, w, dt, tm, "proj_" + name) for name, (w, dt) in lw['w_in'].items()}
    q = proj['q'].reshape(b, seq, -1)
    k = proj['k'].reshape(b, seq, -1)
    v = proj['v'].reshape(b, seq, -1)
    o_att, new_k, new_v = attend(q, k, v)
    xbc = proj['xbc'].reshape(b, seq, -1)
    y_ssm, h_last = _ssd_branch(xbc, proj['z'].reshape(b, seq, -1), proj['dt'].reshape(b, seq, -1),
                                conv_buf, h0, lw['conv_w'], lw['conv_b'], lw['dt_bias'], lw['a_log'],
                                lw['d_skip'], lw['g_ssm_norm'], ssm_dtype)
    new_conv = jnp.concatenate([conv_buf, xbc], axis=1)[:, -(D_CONV - 1):] if seq < D_CONV - 1 \
        else xbc[:, seq - (D_CONV - 1):]
    x1 = _mix(o_att.reshape(t, -1), y_ssm.reshape(t, -1), proj['gate'], x2, mod,
              lw['w_branch_a'], lw['w_branch_b'], lw['w_out'], lw['g_post_mix'], tm)
    return x1, mod, new_k, new_v, new_conv, h_last


def _split_in_proj(w_in, d, d_inner, conv_dim, heads):
    sizes = (N_DIL * ATT_WIDTH, ATT_WIDTH, ATT_WIDTH, d_inner, conv_dim, heads, d, d)
    offs = [0]
    for s in sizes:
        offs.append(offs[-1] + s)
    part = lambda i, j=None: w_in[:, offs[i]:offs[(i if j is None else j) + 1]].astype(BF16)
    w_dt = jnp.pad(part(5), ((0, 0), (0, LANES - heads)))
    return {'q': (part(0), F32), 'k': (part(1), F32), 'v': (part(2), F32), 'z': (part(3), F32),
            'xbc': (part(4), F32), 'dt': (w_dt, F32), 'gate': (part(6, 7), F32)}


def kernel(x_prompt, x_sample, cache_win_k, cache_win_v, state_conv, state_ssm, c_prompt, c_sample, rel_bias, w_mod, b_mod, g_pre_mix, g_post_mix, g_pre_ffn, g_post_ffn, w_in, conv_w, conv_b, dt_bias, a_log, d_skip, g_ssm_norm, w_branch_a, w_branch_b, w_out, w_router, b_router, w_exp_gate, w_exp_up, w_exp_down, w_sh_gate, w_sh_up, w_sh_down):
    depth = w_mod.shape[0]
    bp, sp, d = x_prompt.shape
    bs, ss, _ = x_sample.shape
    d_inner = g_ssm_norm.shape[1]
    conv_dim = conv_w.shape[2]
    heads = dt_bias.shape[1]
    y_p, y_s = x_prompt, x_sample
    outs = [[] for _ in range(8)]
    for l in range(depth):
        lw = {
            'g_pre_mix': g_pre_mix[l], 'g_post_mix': g_post_mix[l],
            'g_pre_ffn': g_pre_ffn[l], 'g_post_ffn': g_post_ffn[l],
            'w_in': _split_in_proj(w_in[l], d, d_inner, conv_dim, heads),
            'conv_w': conv_w[l], 'conv_b': conv_b[l],
            'dt_bias': dt_bias[l], 'a_log': a_log[l], 'd_skip': d_skip[l], 'g_ssm_norm': g_ssm_norm[l],
            'w_branch_a': w_branch_a[l].astype(BF16), 'w_branch_b': w_branch_b[l].astype(BF16),
            'w_out': w_out[l].astype(BF16),
            'w_router_t': w_router[l].T.astype(BF16), 'b_router': b_router[l],
            'w_exp_gate': w_exp_gate[l], 'w_exp_up': w_exp_up[l], 'w_exp_down': w_exp_down[l],
            'w_sh_gate': w_sh_gate[l].astype(BF16), 'w_sh_up': w_sh_up[l].astype(BF16),
            'w_sh_down': w_sh_down[l].astype(BF16),
        }
        mod = _modulation(jnp.concatenate([c_prompt, c_sample], axis=0), w_mod[l], b_mod[l])
        conv0 = jnp.zeros((bp, D_CONV - 1, conv_dim), F32)
        h0 = jnp.zeros((bp, heads, SSM_HEAD_DIM, D_STATE), F32)
        tms = (512, 256)
        x1_p, mod_p, *state_p = _token_mixer(
            y_p, mod[:bp], functools.partial(_prompt_attend, rel_bias=rel_bias), conv0, h0, lw, tms[0], BF16)
        x1_s, mod_s, *state_s = _token_mixer(
            y_s, mod[bp:],
            functools.partial(_sample_attend, buf_k=cache_win_k[l], buf_v=cache_win_v[l], rel_bias=rel_bias),
            state_conv[l], state_ssm[l], lw, tms[1], F32)
        for o, val in zip(outs, state_p + state_s):
            o.append(val)
        y_p, y_s = _channel_mixer((x1_p, x1_s), (mod_p, mod_s), tms, lw)
        y_p = y_p.reshape(bp, sp, d)
        y_s = y_s.reshape(bs, ss, d)
    return (y_p, y_s) + tuple(jnp.stack(o) for o in outs)


def _prompt_attend(q, k, v, rel_bias):
    b, seq, _ = k.shape
    o = _attn_prompt(q, k, v, rel_bias)
    n_keep = min(MAX_WINDOW, seq)
    shape = (b, n_keep, ATT_HEADS, HEAD_DIM)
    return o, k[:, seq - n_keep:].reshape(shape), v[:, seq - n_keep:].reshape(shape)


def _sample_attend(q, k, v, buf_k, buf_v, rel_bias):
    return _attn_sample(q, k, v, buf_k, buf_v, rel_bias)
```

```python
import functools
import math

import jax
import jax.numpy as jnp
from jax import lax
from jax.experimental import pallas as pl
from jax.experimental.pallas import tpu as pltpu
from jax.experimental.pallas import tpu_sc as plsc

F32 = jnp.float32
BF16 = jnp.bfloat16

DIL_PATTERNS = ((128, 1), (512, 4), (2048, 16))
N_DIL = len(DIL_PATTERNS)
ATT_HEADS = 8
HEAD_DIM = 128
ATT_WIDTH = ATT_HEADS * HEAD_DIM
ATT_SCALE = HEAD_DIM ** -0.5
MAX_WINDOW = 2048
NUM_BUCKETS = 32
MAX_DISTANCE = 2048
SSM_HEAD_DIM = 64
SSM_GROUPS = 4
D_STATE = 128
D_CONV = 4
SSD_CHUNK = 128
TOP_K = 8
N_EXPERT_GROUPS = 8
TOPK_GROUPS = 4
ROUTE_SCALE = 2.5
RMS_EPS = 1e-6

LANES = 128
SUBLANES = 8
VMEM_LIMIT = 56 * 1024 * 1024
EXPERT_ROWS = 256
EXPERT_BUFFERS = 4
SC_CORES = 2
SC_SUBCORES = 16
SC_WORKERS = SC_CORES * SC_SUBCORES
NT_DIMS = (((1,), (1,)), ((), ()))


def _params(*sem):
    return pltpu.CompilerParams(dimension_semantics=sem, vmem_limit_bytes=VMEM_LIMIT)


def _sigmoid(x):
    return 0.5 * jnp.tanh(0.5 * x) + 0.5


def _silu(x):
    return x * _sigmoid(x)


def _rms_rows(x, g):
    return x * lax.rsqrt(jnp.mean(x * x, axis=-1, keepdims=True) + RMS_EPS) * g


def _pack_pairs(x):
    n = x.shape[1] // 2
    bits = pltpu.bitcast(x.astype(BF16).astype(F32), jnp.uint32)
    return bits[:, :n] | (bits[:, n:] >> jnp.uint32(16))


def _unpack_pairs(p):
    hi = pltpu.bitcast(p & jnp.uint32(0xFFFF0000), F32)
    lo = pltpu.bitcast(p << jnp.uint32(16), F32)
    return jnp.concatenate([hi, lo], axis=1)


def _mod_kernel(c_ref, w_ref, b_ref, o_ref):
    s = _silu(c_ref[...]).astype(BF16)
    o_ref[...] = jnp.dot(s, w_ref[...].astype(BF16), preferred_element_type=F32) + b_ref[...]


def _modulation(c, w_mod, b_mod):
    m, d = c.shape
    n = w_mod.shape[1]
    tn = n // 4
    return pl.pallas_call(
        _mod_kernel,
        grid=(n // tn,),
        in_specs=[pl.BlockSpec((m, d), lambda j: (0, 0)),
                  pl.BlockSpec((d, tn), lambda j: (0, j)),
                  pl.BlockSpec((1, tn), lambda j: (0, j))],
        out_specs=pl.BlockSpec((m, tn), lambda j: (0, j)),
        out_shape=jax.ShapeDtypeStruct((m, n), F32),
        compiler_params=_params("arbitrary"),
        name="modulation",
    )(c, w_mod, b_mod.reshape(1, n))


class _Mod:
    def __init__(self, mod, seq, tm, d):
        self.d = d
        b = mod.shape[0]
        if seq % tm == 0:
            per = seq // tm
            self.arr = mod.reshape(b, 1, mod.shape[1])
            self._spec = lambda col: pl.BlockSpec((None, 1, d), lambda i: (i // per, 0, col))
        else:
            self.arr = jnp.repeat(mod, seq, axis=0)
            self._spec = lambda col: pl.BlockSpec((tm, d), lambda i: (i, col))

    def spec(self, col):
        return self._spec(col)


def _prenorm_kernel(x_ref, g_ref, sc_ref, sh_ref, o_ref):
    y = _rms_rows(x_ref[...], g_ref[...])
    o_ref[...] = (y * (1 + sc_ref[...]) + sh_ref[...]).astype(o_ref.dtype)


def _prenorm(x2, g, mod, col_scale, col_shift, tm):
    t, d = x2.shape
    return pl.pallas_call(
        _prenorm_kernel,
        grid=(t // tm,),
        in_specs=[pl.BlockSpec((tm, d), lambda i: (i, 0)),
                  pl.BlockSpec((1, d), lambda i: (0, 0)),
                  mod.spec(col_scale), mod.spec(col_shift)],
        out_specs=pl.BlockSpec((tm, d), lambda i: (i, 0)),
        out_shape=jax.ShapeDtypeStruct((t, d), BF16),
        compiler_params=_params("parallel"),
        name="prenorm",
    )(x2, g.reshape(1, d), mod.arr, mod.arr)


def _mm_kernel(h_ref, w_ref, o_ref):
    o_ref[...] = jnp.dot(h_ref[...], w_ref[...], preferred_element_type=F32).astype(o_ref.dtype)


def _matmul(h, w, out_dtype, tm, name):
    t, k = h.shape
    n = w.shape[1]
    return pl.pallas_call(
        _mm_kernel,
        grid=(t // tm,),
        in_specs=[pl.BlockSpec((tm, k), lambda i: (i, 0)),
                  pl.BlockSpec((k, n), lambda i: (0, 0))],
        out_specs=pl.BlockSpec((tm, n), lambda i: (i, 0)),
        out_shape=jax.ShapeDtypeStruct((t, n), out_dtype),
        compiler_params=_params("parallel"),
        name=name,
    )(h, w)


def _bucket(dist):
    max_exact = NUM_BUCKETS // 2
    far = max_exact + (jnp.log(jnp.maximum(dist, 1).astype(F32) / max_exact)
                       / math.log(MAX_DISTANCE / max_exact) * (NUM_BUCKETS - max_exact)).astype(jnp.int32)
    return jnp.where(dist < max_exact, dist, jnp.minimum(far, NUM_BUCKETS - 1))


def _bias_by_distance(rel_bias, n):
    hit = _bucket(jnp.arange(n, dtype=jnp.int32))[:, None] == jnp.arange(NUM_BUCKETS)[None, :]
    return jnp.sum(jnp.where(hit[:, :, None], rel_bias.astype(F32)[None], 0.0), axis=1)


def _band_bias(rel_bias, w):
    by_dist = _bias_by_distance(rel_bias, max(win for win, _ in DIL_PATTERNS) + 1)
    ext = 3 * w
    out = []
    for g, (_, dil) in enumerate(DIL_PATTERNS):
        vec = by_dist[::dil][:w + 1, g * ATT_HEADS:(g + 1) * ATT_HEADS].T
        v = jnp.concatenate([vec[:, ::-1], jnp.full((ATT_HEADS, ext - w - 1), -jnp.inf, F32)], axis=1)
        skew = jnp.tile(v, (1, w))[:, :w * (ext - 1)].reshape(ATT_HEADS, w, ext - 1)
        out.append(skew[:, :, :2 * w])
    return jnp.concatenate(out, axis=0)


def _attn_prompt_kernel(q0_ref, q1_ref, q2_ref, k_ref, v_ref, b0_ref, b1_ref, b2_ref, o_ref,
                        og_scr, lse_scr, *, seq, w):
    q_refs = (q0_ref, q1_ref, q2_ref)
    b_refs = (b0_ref, b1_ref, b2_ref)

    def rows(start, size, dil):
        return pl.ds(start, size) if dil == 1 else pl.ds(start, size, stride=dil)

    for g, (_, dil) in enumerate(DIL_PATTERNS):
        nb = seq // dil // w
        for r in range(dil):
            for n in range(nb):
                q_rows = rows(r + dil * n * w, w, dil)
                qb = q_refs[g][q_rows, :].astype(BF16)
                if n == 0:
                    k_rows = rows(r, w, dil)
                    bias = b_refs[g][:, w:]
                else:
                    k_rows = rows(r + dil * (n - 1) * w, 2 * w, dil)
                    bias = b_refs[g][...]
                kb = k_ref[k_rows, :].astype(BF16)
                vb = v_ref[k_rows, :].astype(BF16)
                s = lax.dot_general(qb, kb, NT_DIMS, preferred_element_type=F32) * ATT_SCALE + bias
                m = jnp.max(s, axis=-1, keepdims=True)
                p = jnp.exp(s - m)
                den = jnp.sum(p, axis=-1, keepdims=True)
                o = jnp.dot(p.astype(BF16), vb, preferred_element_type=F32) / den
                og_scr[g, q_rows, :] = o
                lse_scr[g, q_rows, :] = jnp.broadcast_to(m + jnp.log(den), (w, LANES))

    step = 256
    for c in range(seq // step):
        sl = pl.ds(c * step, step)
        l0, l1, l2 = lse_scr[0, sl, :], lse_scr[1, sl, :], lse_scr[2, sl, :]
        mm = jnp.maximum(jnp.maximum(l0, l1), l2)
        e0, e1, e2 = jnp.exp(l0 - mm), jnp.exp(l1 - mm), jnp.exp(l2 - mm)
        tot = e0 + e1 + e2
        o = (e0 / tot) * og_scr[0, sl, :] + (e1 / tot) * og_scr[1, sl, :] + (e2 / tot) * og_scr[2, sl, :]
        o_ref[sl, :] = o.astype(o_ref.dtype)


def _attn_prompt(q, k, v, rel_bias):
    b, seq, _ = k.shape
    w = DIL_PATTERNS[0][0] // DIL_PATTERNS[0][1]
    for win, dil in DIL_PATTERNS:
        assert win // dil == w and seq % (dil * w) == 0
    bias = _band_bias(rel_bias, w)
    hd = HEAD_DIM

    def q_spec(g):
        return pl.BlockSpec((None, seq, hd), lambda i, h: (i, 0, g * ATT_HEADS + h))

    def b_spec(g):
        return pl.BlockSpec((None, w, 2 * w), lambda i, h: (g * ATT_HEADS + h, 0, 0))

    kv_spec = pl.BlockSpec((None, seq, hd), lambda i, h: (i, 0, h))
    return pl.pallas_call(
        functools.partial(_attn_prompt_kernel, seq=seq, w=w),
        grid=(b, ATT_HEADS),
        in_specs=[q_spec(0), q_spec(1), q_spec(2), kv_spec, kv_spec, b_spec(0), b_spec(1), b_spec(2)],
        out_specs=pl.BlockSpec((None, seq, hd), lambda i, h: (i, 0, h)),
        out_shape=jax.ShapeDtypeStruct((b, seq, ATT_WIDTH), BF16),
        scratch_shapes=[pltpu.VMEM((N_DIL, seq, hd), F32), pltpu.VMEM((N_DIL, seq, LANES), F32)],
        compiler_params=_params("parallel", "parallel"),
        name="attn_prompt",
    )(q, q, q, k, v, bias, bias, bias)


SAMPLE_KEY_CHUNK = 512


def _attn_sample_kernel(q_ref, kn_ref, vn_ref, kn4_ref, vn4_ref, ck_ref, cv_ref, bm_ref, bn_ref,
                        o_ref, wk_ref, wv_ref, m_scr, l_scr, acc_scr, ck_scr, cv_scr, *, t, nch, n_full):
    c = pl.program_id(1)
    ch = ck_ref.shape[0]
    tr = t * ATT_HEADS
    rows = N_DIL * t

    @pl.when(c == 0)
    def _():
        m_scr[...] = jnp.full(m_scr.shape, -jnp.inf, F32)
        l_scr[...] = jnp.zeros(l_scr.shape, F32)
        acc_scr[...] = jnp.zeros(acc_scr.shape, F32)
        ck_scr[...] = kn4_ref[...]
        cv_scr[...] = vn4_ref[...]

    wk_ref[0:ch - tr, :] = ck_ref[tr:ch, :]
    wk_ref[ch - tr:ch, :] = ck_scr[...]
    ck_scr[...] = ck_ref[0:tr, :]
    wv_ref[0:ch - tr, :] = cv_ref[tr:ch, :]
    wv_ref[ch - tr:ch, :] = cv_scr[...]
    cv_scr[...] = cv_ref[0:tr, :]

    def head_q(h):
        return jnp.concatenate(
            [q_ref[:, g * ATT_WIDTH + h * HEAD_DIM:g * ATT_WIDTH + (h + 1) * HEAD_DIM] for g in range(N_DIL)], axis=0)

    def attend(h, r0):
        rs = slice(r0, rows)
        qa = head_q(h)[r0:].astype(BF16)
        head_rows = pl.ds(h, ch // ATT_HEADS, stride=ATT_HEADS)
        kh = ck_ref[head_rows, :].astype(BF16)
        vh = cv_ref[head_rows, :].astype(BF16)
        s = lax.dot_general(qa, kh, NT_DIMS, preferred_element_type=F32) * ATT_SCALE + bm_ref[h, rs, :]
        m_old = m_scr[h, rs, :]
        m_new = jnp.maximum(m_old, jnp.max(s, axis=-1, keepdims=True))
        alpha = jnp.exp(m_old - m_new)
        p = jnp.exp(s - m_new[:, 0:1])
        l_scr[h, rs, :] = alpha * l_scr[h, rs, :] + jnp.sum(p, axis=-1, keepdims=True)
        acc_scr[h, rs, :] = alpha * acc_scr[h, rs, :] + jnp.dot(p.astype(BF16), vh, preferred_element_type=F32)
        m_scr[h, rs, :] = m_new

    @pl.when(c < n_full)
    def _():
        for h in range(ATT_HEADS):
            attend(h, 0)

    @pl.when(c >= n_full)
    def _():
        for h in range(ATT_HEADS):
            attend(h, rows - t)

    @pl.when(c == nch - 1)
    def _():
        for h in range(ATT_HEADS):
            cs = slice(h * HEAD_DIM, (h + 1) * HEAD_DIM)
            qa = head_q(h)
            s_new = [jnp.sum(qa * kn_ref[j:j + 1, cs], axis=-1, keepdims=True) * ATT_SCALE + bn_ref[h, j]
                     for j in range(t)]
            m_old = m_scr[h]
            m_new = m_old
            for sj in s_new:
                m_new = jnp.maximum(m_new, sj)
            alpha = jnp.exp(m_old - m_new)
            den = alpha * l_scr[h]
            acc = alpha * acc_scr[h]
            for j, sj in enumerate(s_new):
                pj = jnp.exp(sj - m_new)
                den = den + pj
                acc = acc + pj * vn_ref[j:j + 1, cs]
            o = acc / den
            lse = m_new + jnp.log(den)
            l0, l1, l2 = lse[0:t], lse[t:2 * t], lse[2 * t:rows]
            mm = jnp.maximum(jnp.maximum(l0, l1), l2)
            e0, e1, e2 = jnp.exp(l0 - mm), jnp.exp(l1 - mm), jnp.exp(l2 - mm)
            tot = e0 + e1 + e2
            o_ref[:, cs] = (e0 / tot) * o[0:t] + (e1 / tot) * o[t:2 * t] + (e2 / tot) * o[2 * t:rows]


def _sample_bias(rel_bias, t, n_buf):
    n_key = n_buf + t
    by_dist = _bias_by_distance(rel_bias, n_key)
    dist = jnp.arange(n_key)
    full = []
    for g, (win, dil) in enumerate(DIL_PATTERNS):
        ok = (dist % dil == 0) & (dist <= win)
        vec = jnp.where(ok[:, None], by_dist[:, g * ATT_HEADS:(g + 1) * ATT_HEADS], -jnp.inf)
        rev = jnp.concatenate([vec[::-1], jnp.full((t - 1, ATT_HEADS), -jnp.inf, F32)], axis=0)
        full.append(jnp.stack([rev[t - 1 - tok:t - 1 - tok + n_key] for tok in range(t)], axis=0))
    full = jnp.transpose(jnp.concatenate(full, axis=0), (2, 0, 1))
    new = jnp.transpose(full[:, :, n_buf:], (0, 2, 1))
    new = jnp.broadcast_to(new[..., None], new.shape + (LANES,))
    return full[:, :, :n_buf], new


def _attn_sample(q, k, v, cache_k, cache_v, rel_bias):
    b, t, _ = k.shape
    n_buf = cache_k.shape[1]
    ch = SAMPLE_KEY_CHUNK
    assert t == SUBLANES and n_buf % ch == 0
    assert all(win <= n_buf for win, _ in DIL_PATTERNS)
    nch = n_buf // ch
    assert all(win <= DIL_PATTERNS[-1][0] for win, _ in DIL_PATTERNS)
    n_full = min(nch, max(-(-win // ch) for win, _ in DIL_PATTERNS[:-1]))
    bm, bn = _sample_bias(rel_bias, t, n_buf)
    flat = lambda a: a.reshape(b, -1, HEAD_DIM)
    rows = N_DIL * t
    tr = t * ATT_HEADS

    def per_b(*shape):
        return pl.BlockSpec((None,) + shape, lambda i, c: (i,) + (0,) * len(shape))

    buf_spec = pl.BlockSpec((None, ch * ATT_HEADS, HEAD_DIM), lambda i, c: (i, nch - 1 - c, 0))
    buf_shape = jax.ShapeDtypeStruct((b, n_buf * ATT_HEADS, HEAD_DIM), F32)
    o, win_k, win_v = pl.pallas_call(
        functools.partial(_attn_sample_kernel, t=t, nch=nch, n_full=n_full),
        grid=(b, nch),
        in_specs=[per_b(t, N_DIL * ATT_WIDTH), per_b(t, ATT_WIDTH), per_b(t, ATT_WIDTH),
                  per_b(tr, HEAD_DIM), per_b(tr, HEAD_DIM), buf_spec, buf_spec,
                  pl.BlockSpec((ATT_HEADS, rows, ch), lambda i, c: (0, 0, nch - 1 - c)),
                  pl.BlockSpec((ATT_HEADS, t, rows, LANES), lambda i, c: (0, 0, 0, 0))],
        out_specs=[per_b(t, ATT_WIDTH), buf_spec, buf_spec],
        out_shape=[jax.ShapeDtypeStruct((b, t, ATT_WIDTH), F32), buf_shape, buf_shape],
        scratch_shapes=[pltpu.VMEM((ATT_HEADS, rows, LANES), F32), pltpu.VMEM((ATT_HEADS, rows, LANES), F32),
                        pltpu.VMEM((ATT_HEADS, rows, HEAD_DIM), F32),
                        pltpu.VMEM((tr, HEAD_DIM), F32), pltpu.VMEM((tr, HEAD_DIM), F32)],
        compiler_params=_params("parallel", "arbitrary"),
        name="attn_sample",
    )(q, k, v, flat(k), flat(v), flat(cache_k), flat(cache_v), bm, bn)
    return o, win_k.reshape(cache_k.shape), win_v.reshape(cache_v.shape)


def _softplus(x):
    return jnp.maximum(x, 0.0) + jnp.log(1.0 + jnp.exp(-jnp.abs(x)))


def _ssd_kernel(xbc_ref, z_ref, dt_ref, cbuf_ref, h0_ref, cw_ref, cb_ref, dtb_ref, alog_ref, dsk_ref, gn_ref,
                y_ref, hl_ref, ext_scr, st_scr, y_scr, acst_scr, dtt_scr, wstt_scr, *, lv, nc, d_inner):
    q = SSD_CHUNK
    n = D_STATE
    c = pl.program_id(1)
    n_pairs = d_inner // LANES
    pairs_per_group = n_pairs // SSM_GROUPS

    @pl.when(c == 0)
    def _():
        ext_scr[0:SUBLANES, :] = cbuf_ref[...]
        for i in range(n_pairs):
            st_scr[:, i * LANES:(i + 1) * LANES] = h0_ref[i * LANES:(i + 1) * LANES, :].T

    ext_scr[SUBLANES:SUBLANES + lv, :] = xbc_ref[...]
    if lv < q:
        ext_scr[SUBLANES + lv:SUBLANES + q, :] = jnp.zeros((q - lv, ext_scr.shape[1]), F32)
    first = SUBLANES - (D_CONV - 1)
    u = cb_ref[...] + cw_ref[0:1, :] * ext_scr[first:first + q, :]
    for j in range(1, D_CONV):
        u = u + cw_ref[j:j + 1, :] * ext_scr[first + j:first + j + q, :]
    u = _silu(u)
    if nc > 1:
        ext_scr[0:SUBLANES, :] = ext_scr[q:q + SUBLANES, :]

    dtv = _softplus(dt_ref_rows(dt_ref, lv, q) + dtb_ref[...])
    if lv < q:
        row = lax.broadcasted_iota(jnp.int32, (q, LANES), 0)
        dtv = jnp.where(row < lv, dtv, 0.0)
    a = -jnp.exp(alog_ref[...])
    da = dtv * a
    li = lax.broadcasted_iota(jnp.int32, (q, q), 0)
    si = lax.broadcasted_iota(jnp.int32, (q, q), 1)
    tri = li >= si
    acs = jnp.dot(tri.astype(F32), da, preferred_element_type=F32, precision=lax.Precision.HIGHEST)
    acs_last = acs[q - 1:q, :]
    acst_scr[...] = acs.T
    dtt_scr[...] = dtv.T
    wstt_scr[...] = (jnp.exp(acs_last - acs) * dtv).T

    lane = lax.broadcasted_iota(jnp.int32, (1, LANES), 1)
    left = lane < SSM_HEAD_DIM
    for g in range(SSM_GROUPS):
        bg = u[:, d_inner + g * n:d_inner + (g + 1) * n]
        cg = u[:, d_inner + (SSM_GROUPS + g) * n:d_inner + (SSM_GROUPS + g + 1) * n].astype(BF16)
        cb = lax.dot_general(cg, bg.astype(BF16), NT_DIMS, preferred_element_type=F32)
        bgt = bg.T
        for jp in range(pairs_per_group):
            pair = g * pairs_per_group + jp
            cols = slice(pair * LANES, (pair + 1) * LANES)
            x_pair = u[:, cols]
            lhs, e_col, dec = [], [], []
            for h in (2 * pair, 2 * pair + 1):
                a_col = jnp.broadcast_to(acs[:, h:h + 1], (q, q))
                seg = a_col - acst_scr[h:h + 1, :]
                m_h = cb * jnp.exp(jnp.where(tri, seg, -jnp.inf)) * dtt_scr[h:h + 1, :]
                lhs.append(m_h.astype(BF16))
                e_col.append(jnp.exp(a_col))
                dec.append(jnp.exp(a_col[q - 1:q, :]))
            for h in (2 * pair, 2 * pair + 1):
                lhs.append((bgt * wstt_scr[h:h + 1, :]).astype(BF16))
            res = jnp.dot(jnp.concatenate(lhs, axis=0), x_pair.astype(BF16), preferred_element_type=F32)
            y_diag = jnp.where(left, res[0:q], res[q:2 * q])
            d_state = jnp.where(left, res[2 * q:2 * q + n], res[2 * q + n:2 * q + 2 * n])
            st = st_scr[:, cols]
            y_off = jnp.dot(cg, st.astype(BF16), preferred_element_type=F32) * jnp.where(left, e_col[0], e_col[1])
            st_scr[:, cols] = st * jnp.where(left, dec[0], dec[1]) + d_state
            y_scr[:, cols] = y_diag + y_off + dsk_ref[:, cols] * x_pair

    gw = d_inner // SSM_GROUPS
    for g in range(SSM_GROUPS):
        cols = slice(g * gw, (g + 1) * gw)
        yg = y_scr[0:lv, cols] * _silu(z_ref[:, cols])
        yg = yg * lax.rsqrt(jnp.mean(yg * yg, axis=-1, keepdims=True) + RMS_EPS)
        y_ref[:, cols] = (yg * gn_ref[:, cols]).astype(y_ref.dtype)

    @pl.when(c == nc - 1)
    def _():
        for i in range(n_pairs):
            hl_ref[i * LANES:(i + 1) * LANES, :] = st_scr[:, i * LANES:(i + 1) * LANES].T


def dt_ref_rows(dt_ref, lv, q):
    if lv == q:
        return dt_ref[...]
    return jnp.concatenate([dt_ref[...], jnp.zeros((q - lv, dt_ref.shape[1]), F32)], axis=0)


def _ssd_branch(xbc, z, dt, conv_buf, h0, conv_w, conv_b, dt_bias, a_log, d_skip, g_norm, out_dtype):
    b, seqlen, conv_dim = xbc.shape
    d_inner = z.shape[2]
    heads = d_inner // SSM_HEAD_DIM
    q = SSD_CHUNK
    lv = min(q, seqlen)
    assert seqlen % lv == 0 and lv % SUBLANES == 0 and heads <= LANES
    nc = seqlen // lv
    cbuf = jnp.pad(conv_buf, ((0, 0), (SUBLANES - (D_CONV - 1), 0), (0, 0)))
    h0f = h0.reshape(b, heads * SSM_HEAD_DIM, D_STATE)
    pad = LANES - heads
    row = lambda a: jnp.pad(a.astype(F32), (0, pad)).reshape(1, LANES)
    dsk = jnp.repeat(d_skip.astype(F32), SSM_HEAD_DIM).reshape(1, d_inner)

    def per_chunk(width):
        return pl.BlockSpec((None, lv, width), lambda i, c: (i, c, 0))

    def per_batch(r, width):
        return pl.BlockSpec((None, r, width), lambda i, c: (i, 0, 0))

    def const(r, width):
        return pl.BlockSpec((r, width), lambda i, c: (0, 0))

    y, h_last = pl.pallas_call(
        functools.partial(_ssd_kernel, lv=lv, nc=nc, d_inner=d_inner),
        grid=(b, nc),
        in_specs=[per_chunk(conv_dim), per_chunk(d_inner), per_chunk(LANES),
                  per_batch(SUBLANES, conv_dim), per_batch(heads * SSM_HEAD_DIM, D_STATE),
                  const(D_CONV, conv_dim), const(1, conv_dim), const(1, LANES), const(1, LANES),
                  const(1, d_inner), const(1, d_inner)],
        out_specs=[per_chunk(d_inner), per_batch(heads * SSM_HEAD_DIM, D_STATE)],
        out_shape=[jax.ShapeDtypeStruct((b, seqlen, d_inner), out_dtype),
                   jax.ShapeDtypeStruct((b, heads * SSM_HEAD_DIM, D_STATE), F32)],
        scratch_shapes=[pltpu.VMEM((q + 2 * SUBLANES, conv_dim), F32),
                        pltpu.VMEM((D_STATE, d_inner), F32),
                        pltpu.VMEM((q, d_inner), F32),
                        pltpu.VMEM((LANES, q), F32), pltpu.VMEM((LANES, q), F32), pltpu.VMEM((LANES, q), F32)],
        compiler_params=_params("parallel", "arbitrary"),
        name="ssd",
    )(xbc, z, dt, cbuf, h0f, conv_w, conv_b.reshape(1, conv_dim), row(dt_bias), row(a_log), dsk,
      g_norm.reshape(1, d_inner))
    return y, h_last.reshape(b, heads, SSM_HEAD_DIM, D_STATE)


def _mix_kernel(oa_ref, ys_ref, gate_ref, x_ref, gt_ref, wa_ref, wb_ref, wo_ref, g_ref, o_ref, *, d):
    ya = jnp.dot(oa_ref[...].astype(BF16), wa_ref[...], preferred_element_type=F32)
    yb = jnp.dot(ys_ref[...].astype(BF16), wb_ref[...], preferred_element_type=F32)
    mixed = _sigmoid(gate_ref[:, 0:d]) * ya + _sigmoid(gate_ref[:, d:2 * d]) * yb
    mix = jnp.dot(mixed.astype(BF16), wo_ref[...], preferred_element_type=F32)
    o_ref[...] = x_ref[...] + gt_ref[...] * _rms_rows(mix, g_ref[...])


def _mix(o_att, y_ssm, gates, x2, mod, wa, wb, wo, g_post, tm):
    t, d = x2.shape
    full = lambda a: pl.BlockSpec(a.shape, lambda i: (0, 0))
    rows = lambda a: pl.BlockSpec((tm, a.shape[1]), lambda i: (i, 0))
    g2 = g_post.reshape(1, d)
    return pl.pallas_call(
        functools.partial(_mix_kernel, d=d),
        grid=(t // tm,),
        in_specs=[rows(o_att), rows(y_ssm), rows(gates), rows(x2), mod.spec(2), full(wa), full(wb), full(wo),
                  full(g2)],
        out_specs=pl.BlockSpec((tm, d), lambda i: (i, 0)),
        out_shape=jax.ShapeDtypeStruct((t, d), F32),
        compiler_params=_params("parallel"),
        name="mix",
    )(o_att, y_ssm, gates, x2, mod.arr, wa, wb, wo, g2)


def _first_index(hit, iota, size):
    return jnp.min(jnp.where(hit, iota, size), axis=0, keepdims=True)


def _ffn_pre_kernel(x_ref, g_ref, sc_ref, sh_ref, wr_ref, br_ref, h_ref, idx_ref, w_ref, *, n_exp):
    y = _rms_rows(x_ref[...], g_ref[...])
    hf = y * (1 + sc_ref[...]) + sh_ref[...]
    h_ref[...] = _pack_pairs(hf)
    h = hf.astype(BF16)
    tm = h.shape[0]
    scores = jax.nn.sigmoid(lax.dot_general(wr_ref[...], h, NT_DIMS, preferred_element_type=F32))
    biased = scores + br_ref[...]
    gsz = n_exp // N_EXPERT_GROUPS
    riota = lax.broadcasted_iota(jnp.int32, (gsz, tm), 0)
    gs = []
    for g in range(N_EXPERT_GROUPS):
        xg = biased[g * gsz:(g + 1) * gsz, :]
        m1 = jnp.max(xg, axis=0, keepdims=True)
        i1 = _first_index(xg == m1, riota, gsz)
        m2 = jnp.max(jnp.where(riota == i1, -jnp.inf, xg), axis=0, keepdims=True)
        gs.append(m1 + m2)
    gs = jnp.concatenate(gs, axis=0)
    giota = lax.broadcasted_iota(jnp.int32, gs.shape, 0)
    keep = jnp.zeros(gs.shape, F32)
    for _ in range(TOPK_GROUPS):
        gi = _first_index(gs == jnp.max(gs, axis=0, keepdims=True), giota, N_EXPERT_GROUPS)
        sel = giota == gi
        keep = jnp.where(sel, 1.0, keep)
        gs = jnp.where(sel, -jnp.inf, gs)
    masked = jnp.concatenate(
        [jnp.where(keep[g:g + 1, :] > 0.5, biased[g * gsz:(g + 1) * gsz, :], -jnp.inf)
         for g in range(N_EXPERT_GROUPS)], axis=0)
    eiota = lax.broadcasted_iota(jnp.int32, (n_exp, tm), 0)
    idxs, ws = [], []
    for _ in range(TOP_K):
        ik = _first_index(masked == jnp.max(masked, axis=0, keepdims=True), eiota, n_exp)
        hit = eiota == ik
        idxs.append(ik)
        ws.append(jnp.sum(jnp.where(hit, scores, 0.0), axis=0, keepdims=True))
        masked = jnp.where(hit, -jnp.inf, masked)
    idx_ref[...] = jnp.concatenate(idxs, axis=0)
    w = jnp.concatenate(ws, axis=0)
    w_ref[...] = w / jnp.sum(w, axis=0, keepdims=True) * ROUTE_SCALE


def _ffn_pre(x1, g, mod, w_router_t, b_router, tm):
    t, d = x1.shape
    e = w_router_t.shape[0]
    br = jnp.broadcast_to(b_router.astype(F32)[:, None], (e, tm))
    return pl.pallas_call(
        functools.partial(_ffn_pre_kernel, n_exp=e),
        grid=(t // tm,),
        in_specs=[pl.BlockSpec((tm, d), lambda i: (i, 0)),
                  pl.BlockSpec((1, d), lambda i: (0, 0)),
                  mod.spec(4), mod.spec(3),
                  pl.BlockSpec((e, d), lambda i: (0, 0)),
                  pl.BlockSpec((e, tm), lambda i: (0, 0))],
        out_specs=[pl.BlockSpec((tm, d // 2), lambda i: (i, 0)),
                   pl.BlockSpec((TOP_K, tm), lambda i: (0, i)),
                   pl.BlockSpec((TOP_K, tm), lambda i: (0, i))],
        out_shape=[jax.ShapeDtypeStruct((t, d // 2), jnp.uint32), jax.ShapeDtypeStruct((TOP_K, t), jnp.int32),
                   jax.ShapeDtypeStruct((TOP_K, t), F32)],
        compiler_params=_params("parallel"),
        name="ffn_pre",
    )(x1, g.reshape(1, d), mod.arr, mod.arr, w_router_t, br)


def _rank_kernel(idx_ref, rank_ref, cnt_ref, base_scr, *, n_exp):
    i = pl.program_id(0)
    tm = idx_ref.shape[1]

    @pl.when(i == 0)
    def _():
        base_scr[...] = jnp.zeros(base_scr.shape, F32)

    eiota = lax.broadcasted_iota(jnp.int32, (n_exp, tm), 0)
    idx = idx_ref[...]
    hits = [eiota == idx[k:k + 1, :] for k in range(TOP_K)]
    multi = jnp.zeros((n_exp, tm), F32)
    for hit in hits:
        multi = multi + jnp.where(hit, 1.0, 0.0)
    multi = multi.astype(BF16)
    ti = lax.broadcasted_iota(jnp.int32, (tm, tm), 0)
    tj = lax.broadcasted_iota(jnp.int32, (tm, tm), 1)
    earlier = jnp.where(ti < tj, 1.0, 0.0).astype(BF16)
    before = jnp.dot(multi, earlier, preferred_element_type=F32) + base_scr[:, 0:1]
    ranks = [jnp.sum(jnp.where(hit, before, 0.0), axis=0, keepdims=True) for hit in hits]
    rank_ref[...] = jnp.concatenate(ranks, axis=0).astype(jnp.int32)
    base_scr[...] = base_scr[...] + jnp.dot(multi, jnp.ones((tm, LANES), BF16), preferred_element_type=F32)
    cnt_ref[...] = base_scr[...]


def _expert_ranks(idx_t, n_exp, tm):
    t = idx_t.shape[1]
    return pl.pallas_call(
        functools.partial(_rank_kernel, n_exp=n_exp),
        grid=(t // tm,),
        in_specs=[pl.BlockSpec((TOP_K, tm), lambda i: (0, i))],
        out_specs=[pl.BlockSpec((TOP_K, tm), lambda i: (0, i)), pl.BlockSpec((n_exp, LANES), lambda i: (0, 0))],
        out_shape=[jax.ShapeDtypeStruct((TOP_K, t), jnp.int32), jax.ShapeDtypeStruct((n_exp, LANES), F32)],
        scratch_shapes=[pltpu.VMEM((n_exp, LANES), F32)],
        compiler_params=_params("arbitrary"),
        name="expert_ranks",
    )(idx_t)


def _pos_kernel(idx_ref, rank_ref, start_ref, pos_ref, *, n_exp):
    tm = idx_ref.shape[1]
    eiota = lax.broadcasted_iota(jnp.int32, (n_exp, tm), 0)
    idx = idx_ref[...]
    start = start_ref[:, 0:1]
    offs = [jnp.sum(jnp.where(eiota == idx[k:k + 1, :], start, 0.0), axis=0, keepdims=True) for k in range(TOP_K)]
    pos_ref[...] = rank_ref[...] + jnp.concatenate(offs, axis=0).astype(jnp.int32)


def _positions(idx_t, rank_t, pad_start, tm):
    t = idx_t.shape[1]
    n_exp = pad_start.shape[0]
    start = jnp.broadcast_to(pad_start.astype(F32)[:, None], (n_exp, LANES))
    spec = pl.BlockSpec((TOP_K, tm), lambda i: (0, i))
    return pl.pallas_call(
        functools.partial(_pos_kernel, n_exp=n_exp),
        grid=(t // tm,),
        in_specs=[spec, spec, pl.BlockSpec((n_exp, LANES), lambda i: (0, 0))],
        out_specs=spec,
        out_shape=jax.ShapeDtypeStruct((TOP_K, t), jnp.int32),
        compiler_params=_params("parallel"),
        name="positions",
    )(idx_t, rank_t, start)


def _experts_kernel(first_ref, count_ref, used_ref, x_hbm, wg_ref, wu_ref, wd_ref, o_hbm,
                    wg_scr, wu_scr, wd_scr, x_vmem, o_vmem, x_sem, o_sem):
    e = pl.program_id(0)
    nbuf, blk = x_vmem.shape[0], x_vmem.shape[1]
    first, count, used = first_ref[e], count_ref[e], used_ref[0]

    def block_rows(g):
        return pl.ds(pl.multiple_of(g * blk, blk), blk)

    def x_copy(g):
        slot = lax.rem(g, nbuf)
        return pltpu.make_async_copy(x_hbm.at[block_rows(g), :], x_vmem.at[slot], x_sem.at[slot])

    def o_copy(g):
        slot = lax.rem(g, nbuf)
        return pltpu.make_async_copy(o_vmem.at[slot], o_hbm.at[block_rows(g), :], o_sem.at[slot])

    @pl.when(e == 0)
    def _():
        for g in range(nbuf - 1):
            @pl.when(g < used)
            def _():
                x_copy(g).start()

    @pl.when(count > 0)
    def _():
        wg_scr[...] = wg_ref[...].astype(BF16)
        wu_scr[...] = wu_ref[...].astype(BF16)
        wd_scr[...] = wd_ref[...].astype(BF16)

    def one_block(j, carry):
        g = first + j
        slot = lax.rem(g, nbuf)
        x_copy(g).wait()

        @pl.when(g + nbuf - 1 < used)
        def _():
            x_copy(g + nbuf - 1).start()

        xb = _unpack_pairs(x_vmem[slot]).astype(BF16)
        act = _silu(jnp.dot(xb, wg_scr[...], preferred_element_type=F32)) * jnp.dot(
            xb, wu_scr[...], preferred_element_type=F32)
        res = _pack_pairs(jnp.dot(act.astype(BF16), wd_scr[...], preferred_element_type=F32))

        @pl.when(g >= nbuf)
        def _():
            o_copy(g - nbuf).wait()

        o_vmem[slot] = res
        o_copy(g).start()
        return carry

    lax.fori_loop(0, count, one_block, 0)

    @pl.when(e == pl.num_programs(0) - 1)
    def _():
        for back in range(nbuf, 0, -1):
            @pl.when(used >= back)
            def _():
                o_copy(used - back).wait()


def _sc_dispatch(h, pos_flat, rows):
    t, w = h.shape
    chunk, nbuf = 48, 2
    per_w = t // SC_WORKERS
    assert t % SC_WORKERS == 0 and per_w % (chunk * nbuf) == 0
    n = per_w // chunk
    mesh = plsc.VectorSubcoreMesh(core_axis_name="c", subcore_axis_name="s")

    @functools.partial(pl.kernel, mesh=mesh, out_type=jax.ShapeDtypeStruct((rows, w), h.dtype),
                       scratch_types=[pltpu.VMEM((nbuf, TOP_K, chunk), jnp.int32),
                                      pltpu.VMEM((nbuf, chunk, w), h.dtype),
                                      pltpu.SemaphoreType.DMA((nbuf,))])
    def scatter_rows(h_hbm, pos_hbm, x_hbm, idx_v, rows_v, sem):
        w0 = (lax.axis_index("s") * SC_CORES + lax.axis_index("c")) * per_w

        def scatter(slot, kk):
            return pltpu.make_async_copy(rows_v.at[slot], x_hbm.at[idx_v.at[slot, kk]], sem.at[slot])

        @pl.loop(0, n // nbuf)
        def _(g):
            for s in range(nbuf):
                @pl.when(g > 0)
                def _():
                    for kk in range(TOP_K):
                        scatter(s, kk).wait()
                base = pl.multiple_of(w0 + (g * nbuf + s) * chunk, SUBLANES)
                pltpu.sync_copy(h_hbm.at[pl.ds(base, chunk)], rows_v.at[s])
                for kk in range(TOP_K):
                    pltpu.sync_copy(pos_hbm.at[pl.ds(pl.multiple_of(kk * t + base, SUBLANES), chunk)],
                                    idx_v.at[s, kk])
                for kk in range(TOP_K):
                    scatter(s, kk).start()

        for s in range(nbuf):
            for kk in range(TOP_K):
                scatter(s, kk).wait()

    return scatter_rows(h, pos_flat)


def _sc_gather(table, idx_flat):
    n_idx = idx_flat.shape[0]
    w = table.shape[1]
    chunk, nbuf = 48, 4
    per_w = n_idx // SC_WORKERS
    assert n_idx % SC_WORKERS == 0 and per_w % (chunk * nbuf) == 0
    n = per_w // chunk
    mesh = plsc.VectorSubcoreMesh(core_axis_name="c", subcore_axis_name="s")

    @functools.partial(pl.kernel, mesh=mesh, out_type=jax.ShapeDtypeStruct((n_idx, w), table.dtype),
                       scratch_types=[pltpu.VMEM((nbuf, chunk), jnp.int32),
                                      pltpu.VMEM((nbuf, chunk, w), table.dtype),
                                      pltpu.SemaphoreType.DMA((nbuf,))])
    def gather_rows(t_hbm, idx_hbm, o_hbm, idx_v, rows_v, sem):
        w0 = (lax.axis_index("s") * SC_CORES + lax.axis_index("c")) * per_w

        def gather(slot):
            return pltpu.make_async_copy(t_hbm.at[idx_v.at[slot]], rows_v.at[slot], sem.at[slot])

        def issue(j, slot):
            base = pl.multiple_of(w0 + j * chunk, SUBLANES)
            pltpu.sync_copy(idx_hbm.at[pl.ds(base, chunk)], idx_v.at[slot])
            gather(slot).start()

        for s in range(nbuf):
            issue(s, s)

        @pl.loop(0, n // nbuf)
        def _(g):
            for s in range(nbuf):
                j = g * nbuf + s
                gather(s).wait()
                pltpu.sync_copy(rows_v.at[s], o_hbm.at[pl.ds(pl.multiple_of(w0 + j * chunk, SUBLANES), chunk)])

                @pl.when(j + nbuf < n)
                def _():
                    issue(j + nbuf, s)

    return gather_rows(table, idx_flat)


def _routed_experts(h, idx_t, wg, wu, wd):
    kk, t = idx_t.shape
    dp = h.shape[1]
    n_exp, d, de = wg.shape
    blk = EXPERT_ROWS
    n_assign = t * kk
    tm = 512
    assert t % tm == 0
    rank_t, cnt = _expert_ranks(idx_t, n_exp, tm)
    counts = cnt[:, 0].astype(jnp.int32)
    padded = (counts + blk - 1) // blk * blk
    pad_end = jnp.cumsum(padded)
    pad_start = pad_end - padded
    pos_flat = _positions(idx_t, rank_t, pad_start, tm).reshape(-1)
    n_blocks = (n_assign + n_exp * (blk - 1)) // blk
    rows = n_blocks * blk
    first_blk = (pad_start // blk).astype(jnp.int32)
    count_blk = (padded // blk).astype(jnp.int32)
    n_used = (pad_end[-1] // blk).astype(jnp.int32).reshape(1)
    x_buf = _sc_dispatch(h, pos_flat, rows)
    w_in_spec = pl.BlockSpec((None, d, de), lambda e, *_: (e, 0, 0))
    grid_spec = pltpu.PrefetchScalarGridSpec(
        num_scalar_prefetch=3,
        grid=(n_exp,),
        in_specs=[pl.BlockSpec(memory_space=pl.ANY), w_in_spec, w_in_spec,
                  pl.BlockSpec((None, de, d), lambda e, *_: (e, 0, 0))],
        out_specs=pl.BlockSpec(memory_space=pl.ANY),
        scratch_shapes=[pltpu.VMEM((d, de), BF16), pltpu.VMEM((d, de), BF16), pltpu.VMEM((de, d), BF16),
                        pltpu.VMEM((EXPERT_BUFFERS, blk, dp), jnp.uint32),
                        pltpu.VMEM((EXPERT_BUFFERS, blk, dp), jnp.uint32),
                        pltpu.SemaphoreType.DMA((EXPERT_BUFFERS,)), pltpu.SemaphoreType.DMA((EXPERT_BUFFERS,))],
    )
    out = pl.pallas_call(
        _experts_kernel,
        grid_spec=grid_spec,
        out_shape=jax.ShapeDtypeStruct((rows, dp), jnp.uint32),
        compiler_params=_params("arbitrary"),
        name="experts",
    )(first_blk, count_blk, n_used, x_buf, wg, wu, wd)
    return _sc_gather(out, pos_flat).reshape(kk, t, dp)


def _ffn_post_kernel(h_ref, r_ref, wt_ref, x_ref, gt_ref, wg_ref, wu_ref, wd_ref, g_ref, o_ref):
    h = _unpack_pairs(h_ref[...]).astype(BF16)
    act = _silu(jnp.dot(h, wg_ref[...], preferred_element_type=F32)) * jnp.dot(
        h, wu_ref[...], preferred_element_type=F32)
    f = jnp.dot(act.astype(BF16), wd_ref[...], preferred_element_type=F32)
    for k in range(TOP_K):
        f = f + _unpack_pairs(r_ref[k]) * wt_ref[:, k:k + 1]
    o_ref[...] = x_ref[...] + gt_ref[...] * _rms_rows(f, g_ref[...])


def _ffn_post(h2, routed, w_rows, row0, x1, mod, wg, wu, wd, g_post, tm):
    t, d = x1.shape
    assert row0 % tm == 0
    blk0 = row0 // tm
    full = lambda a: pl.BlockSpec(a.shape, lambda i: (0, 0))
    rows = lambda a: pl.BlockSpec((tm, a.shape[1]), lambda i: (i, 0))
    g2 = g_post.reshape(1, d)
    return pl.pallas_call(
        _ffn_post_kernel,
        grid=(t // tm,),
        in_specs=[rows(h2),
                  pl.BlockSpec((TOP_K, tm, d // 2), lambda i: (0, i + blk0, 0)),
                  pl.BlockSpec((tm, TOP_K), lambda i: (i + blk0, 0)),
                  rows(x1), mod.spec(5), full(wg), full(wu), full(wd), full(g2)],
        out_specs=pl.BlockSpec((tm, d), lambda i: (i, 0)),
        out_shape=jax.ShapeDtypeStruct((t, d), F32),
        compiler_params=_params("parallel"),
        name="ffn_post",
    )(h2, routed, w_rows, x1, mod.arr, wg, wu, wd, g2)


def _channel_mixer(x1s, mods, tms, lw):
    pre = [_ffn_pre(x1, lw['g_pre_ffn'], mod, lw['w_router_t'], lw['b_router'], tm)
           for x1, mod, tm in zip(x1s, mods, tms)]
    h2 = jnp.concatenate([p[0] for p in pre], axis=0)
    idx_t = jnp.concatenate([p[1] for p in pre], axis=1)
    w_rows = jnp.concatenate([p[2] for p in pre], axis=1).T
    routed = _routed_experts(h2, idx_t, lw['w_exp_gate'], lw['w_exp_up'], lw['w_exp_down'])
    outs, row0 = [], 0
    for x1, mod, tm, p in zip(x1s, mods, tms, pre):
        outs.append(_ffn_post(p[0], routed, w_rows, row0, x1, mod, lw['w_sh_gate'], lw['w_sh_up'],
                              lw['w_sh_down'], lw['g_post_ffn'], tm))
        row0 += x1.shape[0]
    return outs


def _token_mixer(x, mod_rows, attend, conv_buf, h0, lw, tm, ssm_dtype):
    b, seq, d = x.shape
    t = b * seq
    x2 = x.reshape(t, d)
    mod = _Mod(mod_rows, seq, tm, d)
    h = _prenorm(x2, lw['g_pre_mix'], mod, 1, 0, tm)
    proj = {name: _matmul(h, w, dt, tm, "proj_" + name) for name, (w, dt) in lw['w_in'].items()}
    q = proj['q'].reshape(b, seq, -1)
    k = proj['k'].reshape(b, seq, -1)
    v = proj['v'].reshape(b, seq, -1)
    o_att, new_k, new_v = attend(q, k, v)
    xbc = proj['xbc'].reshape(b, seq, -1)
    y_ssm, h_last = _ssd_branch(xbc, proj['z'].reshape(b, seq, -1), proj['dt'].reshape(b, seq, -1),
                                conv_buf, h0, lw['conv_w'], lw['conv_b'], lw['dt_bias'], lw['a_log'],
                                lw['d_skip'], lw['g_ssm_norm'], ssm_dtype)
    new_conv = jnp.concatenate([conv_buf, xbc], axis=1)[:, -(D_CONV - 1):] if seq < D_CONV - 1 \
        else xbc[:, seq - (D_CONV - 1):]
    x1 = _mix(o_att.reshape(t, -1), y_ssm.reshape(t, -1), proj['gate'], x2, mod,
              lw['w_branch_a'], lw['w_branch_b'], lw['w_out'], lw['g_post_mix'], tm)
    return x1, mod, new_k, new_v, new_conv, h_last


def _split_in_proj(w_in, d, d_inner, conv_dim, heads):
    sizes = (N_DIL * ATT_WIDTH, ATT_WIDTH, ATT_WIDTH, d_inner, conv_dim, heads, d, d)
    offs = [0]
    for s in sizes:
        offs.append(offs[-1] + s)
    part = lambda i, j=None: w_in[:, offs[i]:offs[(i if j is None else j) + 1]].astype(BF16)
    w_dt = jnp.pad(part(5), ((0, 0), (0, LANES - heads)))
    return {'q': (part(0), F32), 'k': (part(1), F32), 'v': (part(2), F32), 'z': (part(3), F32),
            'xbc': (part(4), F32), 'dt': (w_dt, F32), 'gate': (part(6, 7), F32)}


def kernel(x_prompt, x_sample, cache_win_k, cache_win_v, state_conv, state_ssm, c_prompt, c_sample, rel_bias, w_mod, b_mod, g_pre_mix, g_post_mix, g_pre_ffn, g_post_ffn, w_in, conv_w, conv_b, dt_bias, a_log, d_skip, g_ssm_norm, w_branch_a, w_branch_b, w_out, w_router, b_router, w_exp_gate, w_exp_up, w_exp_down, w_sh_gate, w_sh_up, w_sh_down):
    depth = w_mod.shape[0]
    bp, sp, d = x_prompt.shape
    bs, ss, _ = x_sample.shape
    d_inner = g_ssm_norm.shape[1]
    conv_dim = conv_w.shape[2]
    heads = dt_bias.shape[1]
    y_p, y_s = x_prompt, x_sample
    outs = [[] for _ in range(8)]
    for l in range(depth):
        lw = {
            'g_pre_mix': g_pre_mix[l], 'g_post_mix': g_post_mix[l],
            'g_pre_ffn': g_pre_ffn[l], 'g_post_ffn': g_post_ffn[l],
            'w_in': _split_in_proj(w_in[l], d, d_inner, conv_dim, heads),
            'conv_w': conv_w[l], 'conv_b': conv_b[l],
            'dt_bias': dt_bias[l], 'a_log': a_log[l], 'd_skip': d_skip[l], 'g_ssm_norm': g_ssm_norm[l],
            'w_branch_a': w_branch_a[l].astype(BF16), 'w_branch_b': w_branch_b[l].astype(BF16),
            'w_out': w_out[l].astype(BF16),
            'w_router_t': w_router[l].T.astype(BF16), 'b_router': b_router[l],
            'w_exp_gate': w_exp_gate[l], 'w_exp_up': w_exp_up[l], 'w_exp_down': w_exp_down[l],
            'w_sh_gate': w_sh_gate[l].astype(BF16), 'w_sh_up': w_sh_up[l].astype(BF16),
            'w_sh_down': w_sh_down[l].astype(BF16),
        }
        mod = _modulation(jnp.concatenate([c_prompt, c_sample], axis=0), w_mod[l], b_mod[l])
        conv0 = jnp.zeros((bp, D_CONV - 1, conv_dim), F32)
        h0 = jnp.zeros((bp, heads, SSM_HEAD_DIM, D_STATE), F32)
        tms = (512, 256)
        x1_p, mod_p, *state_p = _token_mixer(
            y_p, mod[:bp], functools.partial(_prompt_attend, rel_bias=rel_bias), conv0, h0, lw, tms[0], BF16)
        x1_s, mod_s, *state_s = _token_mixer(
            y_s, mod[bp:],
            functools.partial(_sample_attend, buf_k=cache_win_k[l], buf_v=cache_win_v[l], rel_bias=rel_bias),
            state_conv[l], state_ssm[l], lw, tms[1], F32)
        for o, val in zip(outs, state_p + state_s):
            o.append(val)
        y_p, y_s = _channel_mixer((x1_p, x1_s), (mod_p, mod_s), tms, lw)
        y_p = y_p.reshape(bp, sp, d)
        y_s = y_s.reshape(bs, ss, d)
    return (y_p, y_s) + tuple(jnp.stack(o) for o in outs)


def _prompt_attend(q, k, v, rel_bias):
    b, seq, _ = k.shape
    o = _attn_prompt(q, k, v, rel_bias)
    n_keep = min(MAX_WINDOW, seq)
    shape = (b, n_keep, ATT_HEADS, HEAD_DIM)
    return o, k[:, seq - n_keep:].reshape(shape), v[:, seq - n_keep:].reshape(shape)


def _sample_attend(q, k, v, buf_k, buf_v, rel_bias):
    return _attn_sample(q, k, v, buf_k, buf_v, rel_bias)
```

```python
import functools
import math

import jax
import jax.numpy as jnp
from jax import lax
from jax.experimental import pallas as pl
from jax.experimental.pallas import tpu as pltpu
from jax.experimental.pallas import tpu_sc as plsc

F32 = jnp.float32
BF16 = jnp.bfloat16

DIL_PATTERNS = ((128, 1), (512, 4), (2048, 16))
N_DIL = len(DIL_PATTERNS)
ATT_HEADS = 8
HEAD_DIM = 128
ATT_WIDTH = ATT_HEADS * HEAD_DIM
ATT_SCALE = HEAD_DIM ** -0.5
MAX_WINDOW = 2048
NUM_BUCKETS = 32
MAX_DISTANCE = 2048
SSM_HEAD_DIM = 64
SSM_GROUPS = 4
D_STATE = 128
D_CONV = 4
SSD_CHUNK = 128
TOP_K = 8
N_EXPERT_GROUPS = 8
TOPK_GROUPS = 4
ROUTE_SCALE = 2.5
RMS_EPS = 1e-6
LOG2_E = math.log2(math.e)

LANES = 128
SUBLANES = 8
VMEM_LIMIT = 56 * 1024 * 1024
EXPERT_ROWS = 256
EXPERT_BUFFERS = 4
SC_CORES = 2
SC_SUBCORES = 16
SC_WORKERS = SC_CORES * SC_SUBCORES
NT_DIMS = (((1,), (1,)), ((), ()))


def _params(*sem):
    return pltpu.CompilerParams(dimension_semantics=sem, vmem_limit_bytes=VMEM_LIMIT)


def _sigmoid(x):
    return 0.5 * jnp.tanh(0.5 * x) + 0.5


def _silu(x):
    return x * _sigmoid(x)


def _rms_rows(x, g):
    return x * lax.rsqrt(jnp.mean(x * x, axis=-1, keepdims=True) + RMS_EPS) * g


def _pack_pairs(x):
    n = x.shape[1] // 2
    bits = pltpu.bitcast(x.astype(BF16).astype(F32), jnp.uint32)
    return bits[:, :n] | (bits[:, n:] >> jnp.uint32(16))


def _unpack_pairs(p):
    hi = pltpu.bitcast(p & jnp.uint32(0xFFFF0000), F32)
    lo = pltpu.bitcast(p << jnp.uint32(16), F32)
    return jnp.concatenate([hi, lo], axis=1)


def _mod_kernel(c_ref, w_ref, b_ref, o_ref):
    s = _silu(c_ref[...]).astype(BF16)
    o_ref[...] = jnp.dot(s, w_ref[...].astype(BF16), preferred_element_type=F32) + b_ref[...]


def _modulation(c, w_mod, b_mod):
    m, d = c.shape
    n = w_mod.shape[1]
    tn = n // 4
    return pl.pallas_call(
        _mod_kernel,
        grid=(n // tn,),
        in_specs=[pl.BlockSpec((m, d), lambda j: (0, 0)),
                  pl.BlockSpec((d, tn), lambda j: (0, j)),
                  pl.BlockSpec((1, tn), lambda j: (0, j))],
        out_specs=pl.BlockSpec((m, tn), lambda j: (0, j)),
        out_shape=jax.ShapeDtypeStruct((m, n), F32),
        compiler_params=_params("arbitrary"),
        name="modulation",
    )(c, w_mod, b_mod.reshape(1, n))


class _Mod:
    def __init__(self, mod, seq, tm, d):
        self.d = d
        b = mod.shape[0]
        if seq % tm == 0:
            per = seq // tm
            self.arr = mod.reshape(b, 1, mod.shape[1])
            self._spec = lambda col: pl.BlockSpec((None, 1, d), lambda i: (i // per, 0, col))
        else:
            self.arr = jnp.repeat(mod, seq, axis=0)
            self._spec = lambda col: pl.BlockSpec((tm, d), lambda i: (i, col))

    def spec(self, col):
        return self._spec(col)


def _prenorm_kernel(x_ref, g_ref, sc_ref, sh_ref, o_ref):
    y = _rms_rows(x_ref[...], g_ref[...])
    o_ref[...] = (y * (1 + sc_ref[...]) + sh_ref[...]).astype(o_ref.dtype)


def _prenorm(x2, g, mod, col_scale, col_shift, tm):
    t, d = x2.shape
    return pl.pallas_call(
        _prenorm_kernel,
        grid=(t // tm,),
        in_specs=[pl.BlockSpec((tm, d), lambda i: (i, 0)),
                  pl.BlockSpec((1, d), lambda i: (0, 0)),
                  mod.spec(col_scale), mod.spec(col_shift)],
        out_specs=pl.BlockSpec((tm, d), lambda i: (i, 0)),
        out_shape=jax.ShapeDtypeStruct((t, d), BF16),
        compiler_params=_params("parallel"),
        name="prenorm",
    )(x2, g.reshape(1, d), mod.arr, mod.arr)


def _mm_kernel(h_ref, w_ref, o_ref):
    o_ref[...] = jnp.dot(h_ref[...], w_ref[...], preferred_element_type=F32).astype(o_ref.dtype)


def _matmul(h, w, out_dtype, tm, name):
    t, k = h.shape
    n = w.shape[1]
    return pl.pallas_call(
        _mm_kernel,
        grid=(t // tm,),
        in_specs=[pl.BlockSpec((tm, k), lambda i: (i, 0)),
                  pl.BlockSpec((k, n), lambda i: (0, 0))],
        out_specs=pl.BlockSpec((tm, n), lambda i: (i, 0)),
        out_shape=jax.ShapeDtypeStruct((t, n), out_dtype),
        compiler_params=_params("parallel"),
        name=name,
    )(h, w)


def _bucket(dist):
    max_exact = NUM_BUCKETS // 2
    far = max_exact + (jnp.log(jnp.maximum(dist, 1).astype(F32) / max_exact)
                       / math.log(MAX_DISTANCE / max_exact) * (NUM_BUCKETS - max_exact)).astype(jnp.int32)
    return jnp.where(dist < max_exact, dist, jnp.minimum(far, NUM_BUCKETS - 1))


def _bias_by_distance(rel_bias, n):
    hit = _bucket(jnp.arange(n, dtype=jnp.int32))[:, None] == jnp.arange(NUM_BUCKETS)[None, :]
    return jnp.sum(jnp.where(hit[:, :, None], rel_bias.astype(F32)[None], 0.0), axis=1)


def _band_bias(rel_bias, w):
    by_dist = _bias_by_distance(rel_bias, max(win for win, _ in DIL_PATTERNS) + 1)
    ext = 3 * w
    out = []
    for g, (_, dil) in enumerate(DIL_PATTERNS):
        vec = by_dist[::dil][:w + 1, g * ATT_HEADS:(g + 1) * ATT_HEADS].T
        v = jnp.concatenate([vec[:, ::-1], jnp.full((ATT_HEADS, ext - w - 1), -jnp.inf, F32)], axis=1)
        skew = jnp.tile(v, (1, w))[:, :w * (ext - 1)].reshape(ATT_HEADS, w, ext - 1)
        out.append(skew[:, :, :2 * w])
    return jnp.concatenate(out, axis=0)


def _attn_prompt_kernel(q0_ref, q1_ref, q2_ref, k_ref, v_ref, b0_ref, b1_ref, b2_ref, o_ref,
                        og_scr, lse_scr, *, seq, w):
    q_refs = (q0_ref, q1_ref, q2_ref)
    b_refs = (b0_ref, b1_ref, b2_ref)

    def rows(start, size, dil):
        return pl.ds(start, size) if dil == 1 else pl.ds(start, size, stride=dil)

    for g, (_, dil) in enumerate(DIL_PATTERNS):
        nb = seq // dil // w
        for r in range(dil):
            for n in range(nb):
                q_rows = rows(r + dil * n * w, w, dil)
                qb = q_refs[g][q_rows, :].astype(BF16)
                if n == 0:
                    k_rows = rows(r, w, dil)
                    bias = b_refs[g][:, w:]
                else:
                    k_rows = rows(r + dil * (n - 1) * w, 2 * w, dil)
                    bias = b_refs[g][...]
                kb = k_ref[k_rows, :].astype(BF16)
                vb = v_ref[k_rows, :].astype(BF16)
                s = lax.dot_general(qb, kb, NT_DIMS, preferred_element_type=F32) * (ATT_SCALE * LOG2_E) + bias
                m = jnp.max(s, axis=-1, keepdims=True)
                p = jnp.exp2(s - m)
                den = jnp.sum(p, axis=-1, keepdims=True)
                o = jnp.dot(p.astype(BF16), vb, preferred_element_type=F32) / den
                og_scr[g, q_rows, :] = o
                lse_scr[g, q_rows, :] = jnp.broadcast_to(m + jnp.log2(den), (w, LANES))

    step = 256
    for c in range(seq // step):
        sl = pl.ds(c * step, step)
        l0, l1, l2 = lse_scr[0, sl, :], lse_scr[1, sl, :], lse_scr[2, sl, :]
        mm = jnp.maximum(jnp.maximum(l0, l1), l2)
        e0, e1, e2 = jnp.exp2(l0 - mm), jnp.exp2(l1 - mm), jnp.exp2(l2 - mm)
        tot = e0 + e1 + e2
        o = (e0 / tot) * og_scr[0, sl, :] + (e1 / tot) * og_scr[1, sl, :] + (e2 / tot) * og_scr[2, sl, :]
        o_ref[sl, :] = o.astype(o_ref.dtype)


def _attn_prompt(q, k, v, rel_bias):
    b, seq, _ = k.shape
    w = DIL_PATTERNS[0][0] // DIL_PATTERNS[0][1]
    for win, dil in DIL_PATTERNS:
        assert win // dil == w and seq % (dil * w) == 0
    bias = _band_bias(rel_bias, w) * LOG2_E
    hd = HEAD_DIM

    def q_spec(g):
        return pl.BlockSpec((None, seq, hd), lambda i, h: (i, 0, g * ATT_HEADS + h))

    def b_spec(g):
        return pl.BlockSpec((None, w, 2 * w), lambda i, h: (g * ATT_HEADS + h, 0, 0))

    kv_spec = pl.BlockSpec((None, seq, hd), lambda i, h: (i, 0, h))
    return pl.pallas_call(
        functools.partial(_attn_prompt_kernel, seq=seq, w=w),
        grid=(b, ATT_HEADS),
        in_specs=[q_spec(0), q_spec(1), q_spec(2), kv_spec, kv_spec, b_spec(0), b_spec(1), b_spec(2)],
        out_specs=pl.BlockSpec((None, seq, hd), lambda i, h: (i, 0, h)),
        out_shape=jax.ShapeDtypeStruct((b, seq, ATT_WIDTH), BF16),
        scratch_shapes=[pltpu.VMEM((N_DIL, seq, hd), F32), pltpu.VMEM((N_DIL, seq, LANES), F32)],
        compiler_params=_params("parallel", "parallel"),
        name="attn_prompt",
    )(q, q, q, k, v, bias, bias, bias)


SAMPLE_KEY_CHUNK = 1024


def _attn_sample_kernel(q_ref, kn_ref, vn_ref, kn4_ref, vn4_ref, ck_ref, cv_ref, bm_ref, bn_ref,
                        o_ref, wk_ref, wv_ref, m_scr, l_scr, acc_scr, ck_scr, cv_scr, *, t, nch, n_full):
    c = pl.program_id(1)
    ch = ck_ref.shape[0]
    tr = t * ATT_HEADS
    rows = N_DIL * t

    @pl.when(c == 0)
    def _():
        m_scr[...] = jnp.full(m_scr.shape, -jnp.inf, F32)
        l_scr[...] = jnp.zeros(l_scr.shape, F32)
        acc_scr[...] = jnp.zeros(acc_scr.shape, F32)
        ck_scr[...] = kn4_ref[...]
        cv_scr[...] = vn4_ref[...]

    wk_ref[0:ch - tr, :] = ck_ref[tr:ch, :]
    wk_ref[ch - tr:ch, :] = ck_scr[...]
    ck_scr[...] = ck_ref[0:tr, :]
    wv_ref[0:ch - tr, :] = cv_ref[tr:ch, :]
    wv_ref[ch - tr:ch, :] = cv_scr[...]
    cv_scr[...] = cv_ref[0:tr, :]

    def head_q(h):
        return jnp.concatenate(
            [q_ref[:, g * ATT_WIDTH + h * HEAD_DIM:g * ATT_WIDTH + (h + 1) * HEAD_DIM] for g in range(N_DIL)], axis=0)

    def attend(h, r0):
        rs = slice(r0, rows)
        qa = head_q(h)[r0:].astype(BF16)
        head_rows = pl.ds(h, ch // ATT_HEADS, stride=ATT_HEADS)
        kh = ck_ref[head_rows, :].astype(BF16)
        vh = cv_ref[head_rows, :].astype(BF16)
        keys = ch // ATT_HEADS
        bias = bm_ref[h, rs, pl.ds(pl.multiple_of((nch - 1 - c) * keys, LANES), keys)]
        s = lax.dot_general(qa, kh, NT_DIMS, preferred_element_type=F32) * ATT_SCALE + bias
        m_old = m_scr[h, rs, :]
        m_new = jnp.maximum(m_old, jnp.max(s, axis=-1, keepdims=True))
        alpha = jnp.exp(m_old - m_new)
        p = jnp.exp(s - m_new[:, 0:1])
        l_scr[h, rs, :] = alpha * l_scr[h, rs, :] + jnp.sum(p, axis=-1, keepdims=True)
        acc_scr[h, rs, :] = alpha * acc_scr[h, rs, :] + jnp.dot(p.astype(BF16), vh, preferred_element_type=F32)
        m_scr[h, rs, :] = m_new

    @pl.when(c < n_full)
    def _():
        for h in range(ATT_HEADS):
            attend(h, 0)

    @pl.when(c >= n_full)
    def _():
        for h in range(ATT_HEADS):
            attend(h, rows - t)

    @pl.when(c == nch - 1)
    def _():
        for h in range(ATT_HEADS):
            cs = slice(h * HEAD_DIM, (h + 1) * HEAD_DIM)
            qa = head_q(h)
            s_new = [jnp.sum(qa * kn_ref[j:j + 1, cs], axis=-1, keepdims=True) * ATT_SCALE + bn_ref[h, j]
                     for j in range(t)]
            m_old = m_scr[h]
            m_new = m_old
            for sj in s_new:
                m_new = jnp.maximum(m_new, sj)
            alpha = jnp.exp(m_old - m_new)
            den = alpha * l_scr[h]
            acc = alpha * acc_scr[h]
            for j, sj in enumerate(s_new):
                pj = jnp.exp(sj - m_new)
                den = den + pj
                acc = acc + pj * vn_ref[j:j + 1, cs]
            o = acc / den
            lse = m_new + jnp.log(den)
            l0, l1, l2 = lse[0:t], lse[t:2 * t], lse[2 * t:rows]
            mm = jnp.maximum(jnp.maximum(l0, l1), l2)
            e0, e1, e2 = jnp.exp(l0 - mm), jnp.exp(l1 - mm), jnp.exp(l2 - mm)
            tot = e0 + e1 + e2
            o_ref[:, cs] = (e0 / tot) * o[0:t] + (e1 / tot) * o[t:2 * t] + (e2 / tot) * o[2 * t:rows]


def _sample_bias(rel_bias, t, n_buf):
    n_key = n_buf + t
    by_dist = _bias_by_distance(rel_bias, n_key)
    dist = jnp.arange(n_key)
    full = []
    for g, (win, dil) in enumerate(DIL_PATTERNS):
        ok = (dist % dil == 0) & (dist <= win)
        vec = jnp.where(ok[:, None], by_dist[:, g * ATT_HEADS:(g + 1) * ATT_HEADS], -jnp.inf)
        rev = jnp.concatenate([vec[::-1], jnp.full((t - 1, ATT_HEADS), -jnp.inf, F32)], axis=0)
        full.append(jnp.stack([rev[t - 1 - tok:t - 1 - tok + n_key] for tok in range(t)], axis=0))
    full = jnp.transpose(jnp.concatenate(full, axis=0), (2, 0, 1))
    new = jnp.transpose(full[:, :, n_buf:], (0, 2, 1))
    new = jnp.broadcast_to(new[..., None], new.shape + (LANES,))
    return full[:, :, :n_buf], new


def _attn_sample(q, k, v, cache_k, cache_v, rel_bias):
    b, t, _ = k.shape
    n_buf = cache_k.shape[1]
    ch = SAMPLE_KEY_CHUNK
    assert t == SUBLANES and n_buf % ch == 0
    assert all(win <= n_buf for win, _ in DIL_PATTERNS)
    nch = n_buf // ch
    assert all(win <= DIL_PATTERNS[-1][0] for win, _ in DIL_PATTERNS)
    n_full = min(nch, max(-(-win // ch) for win, _ in DIL_PATTERNS[:-1]))
    bm, bn = _sample_bias(rel_bias, t, n_buf)
    flat = lambda a: a.reshape(b, -1, HEAD_DIM)
    rows = N_DIL * t
    tr = t * ATT_HEADS

    def per_b(*shape):
        return pl.BlockSpec((None,) + shape, lambda i, c: (i,) + (0,) * len(shape))

    buf_spec = pl.BlockSpec((None, ch * ATT_HEADS, HEAD_DIM), lambda i, c: (i, nch - 1 - c, 0))
    buf_shape = jax.ShapeDtypeStruct((b, n_buf * ATT_HEADS, HEAD_DIM), F32)
    o, win_k, win_v = pl.pallas_call(
        functools.partial(_attn_sample_kernel, t=t, nch=nch, n_full=n_full),
        grid=(b, nch),
        in_specs=[per_b(t, N_DIL * ATT_WIDTH), per_b(t, ATT_WIDTH), per_b(t, ATT_WIDTH),
                  per_b(tr, HEAD_DIM), per_b(tr, HEAD_DIM), buf_spec, buf_spec,
                  pl.BlockSpec((ATT_HEADS, rows, n_buf), lambda i, c: (0, 0, 0)),
                  pl.BlockSpec((ATT_HEADS, t, rows, LANES), lambda i, c: (0, 0, 0, 0))],
        out_specs=[per_b(t, ATT_WIDTH), buf_spec, buf_spec],
        out_shape=[jax.ShapeDtypeStruct((b, t, ATT_WIDTH), F32), buf_shape, buf_shape],
        scratch_shapes=[pltpu.VMEM((ATT_HEADS, rows, LANES), F32), pltpu.VMEM((ATT_HEADS, rows, LANES), F32),
                        pltpu.VMEM((ATT_HEADS, rows, HEAD_DIM), F32),
                        pltpu.VMEM((tr, HEAD_DIM), F32), pltpu.VMEM((tr, HEAD_DIM), F32)],
        compiler_params=_params("parallel", "arbitrary"),
        name="attn_sample",
    )(q, k, v, flat(k), flat(v), flat(cache_k), flat(cache_v), bm, bn)
    return o, win_k.reshape(cache_k.shape), win_v.reshape(cache_v.shape)


def _softplus(x):
    return jnp.maximum(x, 0.0) + jnp.log(1.0 + jnp.exp(-jnp.abs(x)))


def _ssd_kernel(xbc_ref, z_ref, dt_ref, cbuf_ref, h0_ref, cw_ref, cb_ref, dtb_ref, alog_ref, dsk_ref, gn_ref,
                y_ref, hl_ref, ext_scr, st_scr, y_scr, acst_scr, dtt_scr, wstt_scr, *, lv, nc, d_inner):
    q = SSD_CHUNK
    n = D_STATE
    c = pl.program_id(1)
    n_pairs = d_inner // LANES
    pairs_per_group = n_pairs // SSM_GROUPS

    @pl.when(c == 0)
    def _():
        ext_scr[0:SUBLANES, :] = cbuf_ref[...]
        for i in range(n_pairs):
            st_scr[:, i * LANES:(i + 1) * LANES] = h0_ref[i * LANES:(i + 1) * LANES, :].T

    ext_scr[SUBLANES:SUBLANES + lv, :] = xbc_ref[...]
    if lv < q:
        ext_scr[SUBLANES + lv:SUBLANES + q, :] = jnp.zeros((q - lv, ext_scr.shape[1]), F32)
    first = SUBLANES - (D_CONV - 1)
    u = cb_ref[...] + cw_ref[0:1, :] * ext_scr[first:first + q, :]
    for j in range(1, D_CONV):
        u = u + cw_ref[j:j + 1, :] * ext_scr[first + j:first + j + q, :]
    u = _silu(u)
    if nc > 1:
        ext_scr[0:SUBLANES, :] = ext_scr[q:q + SUBLANES, :]

    dtv = _softplus(dt_ref_rows(dt_ref, lv, q) + dtb_ref[...])
    if lv < q:
        row = lax.broadcasted_iota(jnp.int32, (q, LANES), 0)
        dtv = jnp.where(row < lv, dtv, 0.0)
    a = -jnp.exp(alog_ref[...])
    da = dtv * a
    li = lax.broadcasted_iota(jnp.int32, (q, q), 0)
    si = lax.broadcasted_iota(jnp.int32, (q, q), 1)
    tri = li >= si
    acs = jnp.dot(tri.astype(F32), da, preferred_element_type=F32, precision=lax.Precision.HIGHEST) * LOG2_E
    acs_last = acs[q - 1:q, :]
    acst_scr[...] = acs.T
    dtt_scr[...] = dtv.T
    wstt_scr[...] = (jnp.exp2(acs_last - acs) * dtv).T

    lane = lax.broadcasted_iota(jnp.int32, (1, LANES), 1)
    left = lane < SSM_HEAD_DIM
    for g in range(SSM_GROUPS):
        bg = u[:, d_inner + g * n:d_inner + (g + 1) * n]
        cg = u[:, d_inner + (SSM_GROUPS + g) * n:d_inner + (SSM_GROUPS + g + 1) * n].astype(BF16)
        cb = lax.dot_general(cg, bg.astype(BF16), NT_DIMS, preferred_element_type=F32)
        bgt = bg.T
        for jp in range(pairs_per_group):
            pair = g * pairs_per_group + jp
            cols = slice(pair * LANES, (pair + 1) * LANES)
            x_pair = u[:, cols]
            lhs, e_col, dec = [], [], []
            for h in (2 * pair, 2 * pair + 1):
                a_col = jnp.broadcast_to(acs[:, h:h + 1], (q, q))
                seg = a_col - acst_scr[h:h + 1, :]
                m_h = cb * jnp.exp2(jnp.where(tri, seg, -jnp.inf)) * dtt_scr[h:h + 1, :]
                lhs.append(m_h.astype(BF16))
                e_col.append(jnp.exp2(a_col))
                dec.append(jnp.exp2(a_col[q - 1:q, :]))
            for h in (2 * pair, 2 * pair + 1):
                lhs.append((bgt * wstt_scr[h:h + 1, :]).astype(BF16))
            res = jnp.dot(jnp.concatenate(lhs, axis=0), x_pair.astype(BF16), preferred_element_type=F32)
            y_diag = jnp.where(left, res[0:q], res[q:2 * q])
            d_state = jnp.where(left, res[2 * q:2 * q + n], res[2 * q + n:2 * q + 2 * n])
            st = st_scr[:, cols]
            y_off = jnp.dot(cg, st.astype(BF16), preferred_element_type=F32) * jnp.where(left, e_col[0], e_col[1])
            st_scr[:, cols] = st * jnp.where(left, dec[0], dec[1]) + d_state
            y_scr[:, cols] = y_diag + y_off + dsk_ref[:, cols] * x_pair

    gw = d_inner // SSM_GROUPS
    for g in range(SSM_GROUPS):
        cols = slice(g * gw, (g + 1) * gw)
        yg = y_scr[0:lv, cols] * _silu(z_ref[:, cols])
        yg = yg * lax.rsqrt(jnp.mean(yg * yg, axis=-1, keepdims=True) + RMS_EPS)
        y_ref[:, cols] = (yg * gn_ref[:, cols]).astype(y_ref.dtype)

    @pl.when(c == nc - 1)
    def _():
        for i in range(n_pairs):
            hl_ref[i * LANES:(i + 1) * LANES, :] = st_scr[:, i * LANES:(i + 1) * LANES].T


def dt_ref_rows(dt_ref, lv, q):
    if lv == q:
        return dt_ref[...]
    return jnp.concatenate([dt_ref[...], jnp.zeros((q - lv, dt_ref.shape[1]), F32)], axis=0)


def _ssd_branch(xbc, z, dt, conv_buf, h0, conv_w, conv_b, dt_bias, a_log, d_skip, g_norm, out_dtype):
    b, seqlen, conv_dim = xbc.shape
    d_inner = z.shape[2]
    heads = d_inner // SSM_HEAD_DIM
    q = SSD_CHUNK
    lv = min(q, seqlen)
    assert seqlen % lv == 0 and lv % SUBLANES == 0 and heads <= LANES
    nc = seqlen // lv
    cbuf = jnp.pad(conv_buf, ((0, 0), (SUBLANES - (D_CONV - 1), 0), (0, 0)))
    h0f = h0.reshape(b, heads * SSM_HEAD_DIM, D_STATE)
    pad = LANES - heads
    row = lambda a: jnp.pad(a.astype(F32), (0, pad)).reshape(1, LANES)
    dsk = jnp.repeat(d_skip.astype(F32), SSM_HEAD_DIM).reshape(1, d_inner)

    def per_chunk(width):
        return pl.BlockSpec((None, lv, width), lambda i, c: (i, c, 0))

    def per_batch(r, width):
        return pl.BlockSpec((None, r, width), lambda i, c: (i, 0, 0))

    def const(r, width):
        return pl.BlockSpec((r, width), lambda i, c: (0, 0))

    y, h_last = pl.pallas_call(
        functools.partial(_ssd_kernel, lv=lv, nc=nc, d_inner=d_inner),
        grid=(b, nc),
        in_specs=[per_chunk(conv_dim), per_chunk(d_inner), per_chunk(LANES),
                  per_batch(SUBLANES, conv_dim), per_batch(heads * SSM_HEAD_DIM, D_STATE),
                  const(D_CONV, conv_dim), const(1, conv_dim), const(1, LANES), const(1, LANES),
                  const(1, d_inner), const(1, d_inner)],
        out_specs=[per_chunk(d_inner), per_batch(heads * SSM_HEAD_DIM, D_STATE)],
        out_shape=[jax.ShapeDtypeStruct((b, seqlen, d_inner), out_dtype),
                   jax.ShapeDtypeStruct((b, heads * SSM_HEAD_DIM, D_STATE), F32)],
        scratch_shapes=[pltpu.VMEM((q + 2 * SUBLANES, conv_dim), F32),
                        pltpu.VMEM((D_STATE, d_inner), F32),
                        pltpu.VMEM((q, d_inner), F32),
                        pltpu.VMEM((LANES, q), F32), pltpu.VMEM((LANES, q), F32), pltpu.VMEM((LANES, q), F32)],
        compiler_params=_params("parallel", "arbitrary"),
        name="ssd",
    )(xbc, z, dt, cbuf, h0f, conv_w, conv_b.reshape(1, conv_dim), row(dt_bias), row(a_log), dsk,
      g_norm.reshape(1, d_inner))
    return y, h_last.reshape(b, heads, SSM_HEAD_DIM, D_STATE)


def _mix_kernel(oa_ref, ys_ref, gate_ref, x_ref, gt_ref, wa_ref, wb_ref, wo_ref, g_ref, o_ref, *, d):
    ya = jnp.dot(oa_ref[...].astype(BF16), wa_ref[...], preferred_element_type=F32)
    yb = jnp.dot(ys_ref[...].astype(BF16), wb_ref[...], preferred_element_type=F32)
    mixed = _sigmoid(gate_ref[:, 0:d]) * ya + _sigmoid(gate_ref[:, d:2 * d]) * yb
    mix = jnp.dot(mixed.astype(BF16), wo_ref[...], preferred_element_type=F32)
    o_ref[...] = x_ref[...] + gt_ref[...] * _rms_rows(mix, g_ref[...])


def _mix(o_att, y_ssm, gates, x2, mod, wa, wb, wo, g_post, tm):
    t, d = x2.shape
    full = lambda a: pl.BlockSpec(a.shape, lambda i: (0, 0))
    rows = lambda a: pl.BlockSpec((tm, a.shape[1]), lambda i: (i, 0))
    g2 = g_post.reshape(1, d)
    return pl.pallas_call(
        functools.partial(_mix_kernel, d=d),
        grid=(t // tm,),
        in_specs=[rows(o_att), rows(y_ssm), rows(gates), rows(x2), mod.spec(2), full(wa), full(wb), full(wo),
                  full(g2)],
        out_specs=pl.BlockSpec((tm, d), lambda i: (i, 0)),
        out_shape=jax.ShapeDtypeStruct((t, d), F32),
        compiler_params=_params("parallel"),
        name="mix",
    )(o_att, y_ssm, gates, x2, mod.arr, wa, wb, wo, g2)


def _first_index(hit, iota, size):
    return jnp.min(jnp.where(hit, iota, size), axis=0, keepdims=True)


def _ffn_pre_kernel(x_ref, g_ref, sc_ref, sh_ref, wr_ref, br_ref, h_ref, idx_ref, w_ref, *, n_exp):
    y = _rms_rows(x_ref[...], g_ref[...])
    hf = y * (1 + sc_ref[...]) + sh_ref[...]
    h_ref[...] = _pack_pairs(hf)
    h = hf.astype(BF16)
    tm = h.shape[0]
    scores = jax.nn.sigmoid(lax.dot_general(wr_ref[...], h, NT_DIMS, preferred_element_type=F32))
    biased = scores + br_ref[...]
    gsz = n_exp // N_EXPERT_GROUPS
    riota = lax.broadcasted_iota(jnp.int32, (gsz, tm), 0)
    gs = []
    for g in range(N_EXPERT_GROUPS):
        xg = biased[g * gsz:(g + 1) * gsz, :]
        m1 = jnp.max(xg, axis=0, keepdims=True)
        i1 = _first_index(xg == m1, riota, gsz)
        m2 = jnp.max(jnp.where(riota == i1, -jnp.inf, xg), axis=0, keepdims=True)
        gs.append(m1 + m2)
    gs = jnp.concatenate(gs, axis=0)
    giota = lax.broadcasted_iota(jnp.int32, gs.shape, 0)
    keep = jnp.zeros(gs.shape, F32)
    for _ in range(TOPK_GROUPS):
        gi = _first_index(gs == jnp.max(gs, axis=0, keepdims=True), giota, N_EXPERT_GROUPS)
        sel = giota == gi
        keep = jnp.where(sel, 1.0, keep)
        gs = jnp.where(sel, -jnp.inf, gs)
    masked = jnp.concatenate(
        [jnp.where(keep[g:g + 1, :] > 0.5, biased[g * gsz:(g + 1) * gsz, :], -jnp.inf)
         for g in range(N_EXPERT_GROUPS)], axis=0)
    eiota = lax.broadcasted_iota(jnp.int32, (n_exp, tm), 0)
    idxs, ws = [], []
    for _ in range(TOP_K):
        ik = _first_index(masked == jnp.max(masked, axis=0, keepdims=True), eiota, n_exp)
        hit = eiota == ik
        idxs.append(ik)
        ws.append(jnp.sum(jnp.where(hit, scores, 0.0), axis=0, keepdims=True))
        masked = jnp.where(hit, -jnp.inf, masked)
    idx_ref[...] = jnp.concatenate(idxs, axis=0)
    w = jnp.concatenate(ws, axis=0)
    w_ref[...] = w / jnp.sum(w, axis=0, keepdims=True) * ROUTE_SCALE


def _ffn_pre(x1, g, mod, w_router_t, b_router, tm):
    t, d = x1.shape
    e = w_router_t.shape[0]
    br = jnp.broadcast_to(b_router.astype(F32)[:, None], (e, tm))
    return pl.pallas_call(
        functools.partial(_ffn_pre_kernel, n_exp=e),
        grid=(t // tm,),
        in_specs=[pl.BlockSpec((tm, d), lambda i: (i, 0)),
                  pl.BlockSpec((1, d), lambda i: (0, 0)),
                  mod.spec(4), mod.spec(3),
                  pl.BlockSpec((e, d), lambda i: (0, 0)),
                  pl.BlockSpec((e, tm), lambda i: (0, 0))],
        out_specs=[pl.BlockSpec((tm, d // 2), lambda i: (i, 0)),
                   pl.BlockSpec((TOP_K, tm), lambda i: (0, i)),
                   pl.BlockSpec((TOP_K, tm), lambda i: (0, i))],
        out_shape=[jax.ShapeDtypeStruct((t, d // 2), jnp.uint32), jax.ShapeDtypeStruct((TOP_K, t), jnp.int32),
                   jax.ShapeDtypeStruct((TOP_K, t), F32)],
        compiler_params=_params("parallel"),
        name="ffn_pre",
    )(x1, g.reshape(1, d), mod.arr, mod.arr, w_router_t, br)


def _rank_kernel(idx_ref, rank_ref, cnt_ref, base_scr, *, n_exp):
    i = pl.program_id(0)
    tm = idx_ref.shape[1]

    @pl.when(i == 0)
    def _():
        base_scr[...] = jnp.zeros(base_scr.shape, F32)

    eiota = lax.broadcasted_iota(jnp.int32, (n_exp, tm), 0)
    idx = idx_ref[...]
    hits = [eiota == idx[k:k + 1, :] for k in range(TOP_K)]
    multi = jnp.zeros((n_exp, tm), F32)
    for hit in hits:
        multi = multi + jnp.where(hit, 1.0, 0.0)
    multi = multi.astype(BF16)
    ti = lax.broadcasted_iota(jnp.int32, (tm, tm), 0)
    tj = lax.broadcasted_iota(jnp.int32, (tm, tm), 1)
    earlier = jnp.where(ti < tj, 1.0, 0.0).astype(BF16)
    before = jnp.dot(multi, earlier, preferred_element_type=F32) + base_scr[:, 0:1]
    ranks = [jnp.sum(jnp.where(hit, before, 0.0), axis=0, keepdims=True) for hit in hits]
    rank_ref[...] = jnp.concatenate(ranks, axis=0).astype(jnp.int32)
    base_scr[...] = base_scr[...] + jnp.dot(multi, jnp.ones((tm, LANES), BF16), preferred_element_type=F32)
    cnt_ref[...] = base_scr[...]


def _expert_ranks(idx_t, n_exp, tm):
    t = idx_t.shape[1]
    return pl.pallas_call(
        functools.partial(_rank_kernel, n_exp=n_exp),
        grid=(t // tm,),
        in_specs=[pl.BlockSpec((TOP_K, tm), lambda i: (0, i))],
        out_specs=[pl.BlockSpec((TOP_K, tm), lambda i: (0, i)), pl.BlockSpec((n_exp, LANES), lambda i: (0, 0))],
        out_shape=[jax.ShapeDtypeStruct((TOP_K, t), jnp.int32), jax.ShapeDtypeStruct((n_exp, LANES), F32)],
        scratch_shapes=[pltpu.VMEM((n_exp, LANES), F32)],
        compiler_params=_params("arbitrary"),
        name="expert_ranks",
    )(idx_t)


def _pos_kernel(idx_ref, rank_ref, start_ref, pos_ref, *, n_exp):
    tm = idx_ref.shape[1]
    eiota = lax.broadcasted_iota(jnp.int32, (n_exp, tm), 0)
    idx = idx_ref[...]
    start = start_ref[:, 0:1]
    offs = [jnp.sum(jnp.where(eiota == idx[k:k + 1, :], start, 0.0), axis=0, keepdims=True) for k in range(TOP_K)]
    pos_ref[...] = rank_ref[...] + jnp.concatenate(offs, axis=0).astype(jnp.int32)


def _positions(idx_t, rank_t, pad_start, tm):
    t = idx_t.shape[1]
    n_exp = pad_start.shape[0]
    start = jnp.broadcast_to(pad_start.astype(F32)[:, None], (n_exp, LANES))
    spec = pl.BlockSpec((TOP_K, tm), lambda i: (0, i))
    return pl.pallas_call(
        functools.partial(_pos_kernel, n_exp=n_exp),
        grid=(t // tm,),
        in_specs=[spec, spec, pl.BlockSpec((n_exp, LANES), lambda i: (0, 0))],
        out_specs=spec,
        out_shape=jax.ShapeDtypeStruct((TOP_K, t), jnp.int32),
        compiler_params=_params("parallel"),
        name="positions",
    )(idx_t, rank_t, start)


def _experts_kernel(first_ref, count_ref, used_ref, x_hbm, wg_ref, wu_ref, wd_ref, o_hbm,
                    wg_scr, wu_scr, wd_scr, x_vmem, o_vmem, x_sem, o_sem):
    e = pl.program_id(0)
    nbuf, blk = x_vmem.shape[0], x_vmem.shape[1]
    first, count, used = first_ref[e], count_ref[e], used_ref[0]

    def block_rows(g):
        return pl.ds(pl.multiple_of(g * blk, blk), blk)

    def x_copy(g):
        slot = lax.rem(g, nbuf)
        return pltpu.make_async_copy(x_hbm.at[block_rows(g), :], x_vmem.at[slot], x_sem.at[slot])

    def o_copy(g):
        slot = lax.rem(g, nbuf)
        return pltpu.make_async_copy(o_vmem.at[slot], o_hbm.at[block_rows(g), :], o_sem.at[slot])

    @pl.when(e == 0)
    def _():
        for g in range(nbuf - 1):
            @pl.when(g < used)
            def _():
                x_copy(g).start()

    @pl.when(count > 0)
    def _():
        wg_scr[...] = wg_ref[...].astype(BF16)
        wu_scr[...] = wu_ref[...].astype(BF16)
        wd_scr[...] = wd_ref[...].astype(BF16)

    def one_block(j, carry):
        g = first + j
        slot = lax.rem(g, nbuf)
        x_copy(g).wait()

        @pl.when(g + nbuf - 1 < used)
        def _():
            x_copy(g + nbuf - 1).start()

        xb = _unpack_pairs(x_vmem[slot]).astype(BF16)
        act = _silu(jnp.dot(xb, wg_scr[...], preferred_element_type=F32)) * jnp.dot(
            xb, wu_scr[...], preferred_element_type=F32)
        res = _pack_pairs(jnp.dot(act.astype(BF16), wd_scr[...], preferred_element_type=F32))

        @pl.when(g >= nbuf)
        def _():
            o_copy(g - nbuf).wait()

        o_vmem[slot] = res
        o_copy(g).start()
        return carry

    lax.fori_loop(0, count, one_block, 0)

    @pl.when(e == pl.num_programs(0) - 1)
    def _():
        for back in range(nbuf, 0, -1):
            @pl.when(used >= back)
            def _():
                o_copy(used - back).wait()


def _sc_dispatch(h, pos_flat, rows):
    t, w = h.shape
    chunk, nbuf = 48, 2
    per_w = t // SC_WORKERS
    assert t % SC_WORKERS == 0 and per_w % (chunk * nbuf) == 0
    n = per_w // chunk
    mesh = plsc.VectorSubcoreMesh(core_axis_name="c", subcore_axis_name="s")

    @functools.partial(pl.kernel, mesh=mesh, out_type=jax.ShapeDtypeStruct((rows, w), h.dtype),
                       scratch_types=[pltpu.VMEM((nbuf, TOP_K, chunk), jnp.int32),
                                      pltpu.VMEM((nbuf, chunk, w), h.dtype),
                                      pltpu.SemaphoreType.DMA((nbuf,))])
    def scatter_rows(h_hbm, pos_hbm, x_hbm, idx_v, rows_v, sem):
        w0 = (lax.axis_index("s") * SC_CORES + lax.axis_index("c")) * per_w

        def scatter(slot, kk):
            return pltpu.make_async_copy(rows_v.at[slot], x_hbm.at[idx_v.at[slot, kk]], sem.at[slot])

        @pl.loop(0, n // nbuf)
        def _(g):
            for s in range(nbuf):
                @pl.when(g > 0)
                def _():
                    for kk in range(TOP_K):
                        scatter(s, kk).wait()
                base = pl.multiple_of(w0 + (g * nbuf + s) * chunk, SUBLANES)
                pltpu.sync_copy(h_hbm.at[pl.ds(base, chunk)], rows_v.at[s])
                for kk in range(TOP_K):
                    pltpu.sync_copy(pos_hbm.at[pl.ds(pl.multiple_of(kk * t + base, SUBLANES), chunk)],
                                    idx_v.at[s, kk])
                for kk in range(TOP_K):
                    scatter(s, kk).start()

        for s in range(nbuf):
            for kk in range(TOP_K):
                scatter(s, kk).wait()

    return scatter_rows(h, pos_flat)


def _sc_gather(table, idx_flat):
    n_idx = idx_flat.shape[0]
    w = table.shape[1]
    chunk, nbuf = 48, 4
    per_w = n_idx // SC_WORKERS
    assert n_idx % SC_WORKERS == 0 and per_w % (chunk * nbuf) == 0
    n = per_w // chunk
    mesh = plsc.VectorSubcoreMesh(core_axis_name="c", subcore_axis_name="s")

    @functools.partial(pl.kernel, mesh=mesh, out_type=jax.ShapeDtypeStruct((n_idx, w), table.dtype),
                       scratch_types=[pltpu.VMEM((nbuf, chunk), jnp.int32),
                                      pltpu.VMEM((nbuf, chunk, w), table.dtype),
                                      pltpu.SemaphoreType.DMA((nbuf,))])
    def gather_rows(t_hbm, idx_hbm, o_hbm, idx_v, rows_v, sem):
        w0 = (lax.axis_index("s") * SC_CORES + lax.axis_index("c")) * per_w

        def gather(slot):
            return pltpu.make_async_copy(t_hbm.at[idx_v.at[slot]], rows_v.at[slot], sem.at[slot])

        def issue(j, slot):
            base = pl.multiple_of(w0 + j * chunk, SUBLANES)
            pltpu.sync_copy(idx_hbm.at[pl.ds(base, chunk)], idx_v.at[slot])
            gather(slot).start()

        for s in range(nbuf):
            issue(s, s)

        @pl.loop(0, n // nbuf)
        def _(g):
            for s in range(nbuf):
                j = g * nbuf + s
                gather(s).wait()
                pltpu.sync_copy(rows_v.at[s], o_hbm.at[pl.ds(pl.multiple_of(w0 + j * chunk, SUBLANES), chunk)])

                @pl.when(j + nbuf < n)
                def _():
                    issue(j + nbuf, s)

    return gather_rows(table, idx_flat)


def _routed_experts(h, idx_t, wg, wu, wd):
    kk, t = idx_t.shape
    dp = h.shape[1]
    n_exp, d, de = wg.shape
    blk = EXPERT_ROWS
    n_assign = t * kk
    tm = 512
    assert t % tm == 0
    rank_t, cnt = _expert_ranks(idx_t, n_exp, tm)
    counts = cnt[:, 0].astype(jnp.int32)
    padded = (counts + blk - 1) // blk * blk
    pad_end = jnp.cumsum(padded)
    pad_start = pad_end - padded
    pos_flat = _positions(idx_t, rank_t, pad_start, tm).reshape(-1)
    n_blocks = (n_assign + n_exp * (blk - 1)) // blk
    rows = n_blocks * blk
    first_blk = (pad_start // blk).astype(jnp.int32)
    count_blk = (padded // blk).astype(jnp.int32)
    n_used = (pad_end[-1] // blk).astype(jnp.int32).reshape(1)
    x_buf = _sc_dispatch(h, pos_flat, rows)
    w_in_spec = pl.BlockSpec((None, d, de), lambda e, *_: (e, 0, 0))
    grid_spec = pltpu.PrefetchScalarGridSpec(
        num_scalar_prefetch=3,
        grid=(n_exp,),
        in_specs=[pl.BlockSpec(memory_space=pl.ANY), w_in_spec, w_in_spec,
                  pl.BlockSpec((None, de, d), lambda e, *_: (e, 0, 0))],
        out_specs=pl.BlockSpec(memory_space=pl.ANY),
        scratch_shapes=[pltpu.VMEM((d, de), BF16), pltpu.VMEM((d, de), BF16), pltpu.VMEM((de, d), BF16),
                        pltpu.VMEM((EXPERT_BUFFERS, blk, dp), jnp.uint32),
                        pltpu.VMEM((EXPERT_BUFFERS, blk, dp), jnp.uint32),
                        pltpu.SemaphoreType.DMA((EXPERT_BUFFERS,)), pltpu.SemaphoreType.DMA((EXPERT_BUFFERS,))],
    )
    out = pl.pallas_call(
        _experts_kernel,
        grid_spec=grid_spec,
        out_shape=jax.ShapeDtypeStruct((rows, dp), jnp.uint32),
        compiler_params=_params("arbitrary"),
        name="experts",
    )(first_blk, count_blk, n_used, x_buf, wg, wu, wd)
    return _sc_gather(out, pos_flat).reshape(kk, t, dp)


def _ffn_post_kernel(h_ref, r_ref, wt_ref, x_ref, gt_ref, wg_ref, wu_ref, wd_ref, g_ref, o_ref):
    h = _unpack_pairs(h_ref[...]).astype(BF16)
    act = _silu(jnp.dot(h, wg_ref[...], preferred_element_type=F32)) * jnp.dot(
        h, wu_ref[...], preferred_element_type=F32)
    f = jnp.dot(act.astype(BF16), wd_ref[...], preferred_element_type=F32)
    for k in range(TOP_K):
        f = f + _unpack_pairs(r_ref[k]) * wt_ref[:, k:k + 1]
    o_ref[...] = x_ref[...] + gt_ref[...] * _rms_rows(f, g_ref[...])


def _ffn_post(h2, routed, w_rows, row0, x1, mod, wg, wu, wd, g_post, tm):
    t, d = x1.shape
    assert row0 % tm == 0
    blk0 = row0 // tm
    full = lambda a: pl.BlockSpec(a.shape, lambda i: (0, 0))
    rows = lambda a: pl.BlockSpec((tm, a.shape[1]), lambda i: (i, 0))
    g2 = g_post.reshape(1, d)
    return pl.pallas_call(
        _ffn_post_kernel,
        grid=(t // tm,),
        in_specs=[rows(h2),
                  pl.BlockSpec((TOP_K, tm, d // 2), lambda i: (0, i + blk0, 0)),
                  pl.BlockSpec((tm, TOP_K), lambda i: (i + blk0, 0)),
                  rows(x1), mod.spec(5), full(wg), full(wu), full(wd), full(g2)],
        out_specs=pl.BlockSpec((tm, d), lambda i: (i, 0)),
        out_shape=jax.ShapeDtypeStruct((t, d), F32),
        compiler_params=_params("parallel"),
        name="ffn_post",
    )(h2, routed, w_rows, x1, mod.arr, wg, wu, wd, g2)


def _channel_mixer(x1s, mods, tms, lw):
    pre = [_ffn_pre(x1, lw['g_pre_ffn'], mod, lw['w_router_t'], lw['b_router'], tm)
           for x1, mod, tm in zip(x1s, mods, tms)]
    h2 = jnp.concatenate([p[0] for p in pre], axis=0)
    idx_t = jnp.concatenate([p[1] for p in pre], axis=1)
    w_rows = jnp.concatenate([p[2] for p in pre], axis=1).T
    routed = _routed_experts(h2, idx_t, lw['w_exp_gate'], lw['w_exp_up'], lw['w_exp_down'])
    outs, row0 = [], 0
    for x1, mod, tm, p in zip(x1s, mods, tms, pre):
        outs.append(_ffn_post(p[0], routed, w_rows, row0, x1, mod, lw['w_sh_gate'], lw['w_sh_up'],
                              lw['w_sh_down'], lw['g_post_ffn'], tm))
        row0 += x1.shape[0]
    return outs


def _token_mixer(x, mod_rows, attend, conv_buf, h0, lw, tm, ssm_dtype):
    b, seq, d = x.shape
    t = b * seq
    x2 = x.reshape(t, d)
    mod = _Mod(mod_rows, seq, tm, d)
    h = _prenorm(x2, lw['g_pre_mix'], mod, 1, 0, tm)
    proj = {name: _matmul(h, w, dt, tm, "proj_" + name) for name, (w, dt) in lw['w_in'].items()}
    q = proj['q'].reshape(b, seq, -1)
    k = proj['k'].reshape(b, seq, -1)
    v = proj['v'].reshape(b, seq, -1)
    o_att, new_k, new_v = attend(q, k, v)
    xbc = proj['xbc'].reshape(b, seq, -1)
    y_ssm, h_last = _ssd_branch(xbc, proj['z'].reshape(b, seq, -1), proj['dt'].reshape(b, seq, -1),
                                conv_buf, h0, lw['conv_w'], lw['conv_b'], lw['dt_bias'], lw['a_log'],
                                lw['d_skip'], lw['g_ssm_norm'], ssm_dtype)
    new_conv = jnp.concatenate([conv_buf, xbc], axis=1)[:, -(D_CONV - 1):] if seq < D_CONV - 1 \
        else xbc[:, seq - (D_CONV - 1):]
    x1 = _mix(o_att.reshape(t, -1), y_ssm.reshape(t, -1), proj['gate'], x2, mod,
              lw['w_branch_a'], lw['w_branch_b'], lw['w_out'], lw['g_post_mix'], tm)
    return x1, mod, new_k, new_v, new_conv, h_last


def _split_in_proj(w_in, d, d_inner, conv_dim, heads):
    sizes = (N_DIL * ATT_WIDTH, ATT_WIDTH, ATT_WIDTH, d_inner, conv_dim, heads, d, d)
    offs = [0]
    for s in sizes:
        offs.append(offs[-1] + s)
    part = lambda i, j=None: w_in[:, offs[i]:offs[(i if j is None else j) + 1]].astype(BF16)
    w_dt = jnp.pad(part(5), ((0, 0), (0, LANES - heads)))
    return {'q': (part(0), F32), 'k': (part(1), F32), 'v': (part(2), F32), 'z': (part(3), F32),
            'xbc': (part(4), F32), 'dt': (w_dt, F32), 'gate': (part(6, 7), F32)}


def kernel(x_prompt, x_sample, cache_win_k, cache_win_v, state_conv, state_ssm, c_prompt, c_sample, rel_bias, w_mod, b_mod, g_pre_mix, g_post_mix, g_pre_ffn, g_post_ffn, w_in, conv_w, conv_b, dt_bias, a_log, d_skip, g_ssm_norm, w_branch_a, w_branch_b, w_out, w_router, b_router, w_exp_gate, w_exp_up, w_exp_down, w_sh_gate, w_sh_up, w_sh_down):
    depth = w_mod.shape[0]
    bp, sp, d = x_prompt.shape
    bs, ss, _ = x_sample.shape
    d_inner = g_ssm_norm.shape[1]
    conv_dim = conv_w.shape[2]
    heads = dt_bias.shape[1]
    y_p, y_s = x_prompt, x_sample
    outs = [[] for _ in range(8)]
    for l in range(depth):
        lw = {
            'g_pre_mix': g_pre_mix[l], 'g_post_mix': g_post_mix[l],
            'g_pre_ffn': g_pre_ffn[l], 'g_post_ffn': g_post_ffn[l],
            'w_in': _split_in_proj(w_in[l], d, d_inner, conv_dim, heads),
            'conv_w': conv_w[l], 'conv_b': conv_b[l],
            'dt_bias': dt_bias[l], 'a_log': a_log[l], 'd_skip': d_skip[l], 'g_ssm_norm': g_ssm_norm[l],
            'w_branch_a': w_branch_a[l].astype(BF16), 'w_branch_b': w_branch_b[l].astype(BF16),
            'w_out': w_out[l].astype(BF16),
            'w_router_t': w_router[l].T.astype(BF16), 'b_router': b_router[l],
            'w_exp_gate': w_exp_gate[l], 'w_exp_up': w_exp_up[l], 'w_exp_down': w_exp_down[l],
            'w_sh_gate': w_sh_gate[l].astype(BF16), 'w_sh_up': w_sh_up[l].astype(BF16),
            'w_sh_down': w_sh_down[l].astype(BF16),
        }
        mod = _modulation(jnp.concatenate([c_prompt, c_sample], axis=0), w_mod[l], b_mod[l])
        conv0 = jnp.zeros((bp, D_CONV - 1, conv_dim), F32)
        h0 = jnp.zeros((bp, heads, SSM_HEAD_DIM, D_STATE), F32)
        tms = (512, 256)
        x1_p, mod_p, *state_p = _token_mixer(
            y_p, mod[:bp], functools.partial(_prompt_attend, rel_bias=rel_bias), conv0, h0, lw, tms[0], BF16)
        x1_s, mod_s, *state_s = _token_mixer(
            y_s, mod[bp:],
            functools.partial(_sample_attend, buf_k=cache_win_k[l], buf_v=cache_win_v[l], rel_bias=rel_bias),
            state_conv[l], state_ssm[l], lw, tms[1], F32)
        for o, val in zip(outs, state_p + state_s):
            o.append(val)
        y_p, y_s = _channel_mixer((x1_p, x1_s), (mod_p, mod_s), tms, lw)
        y_p = y_p.reshape(bp, sp, d)
        y_s = y_s.reshape(bs, ss, d)
    return (y_p, y_s) + tuple(jnp.stack(o) for o in outs)


def _prompt_attend(q, k, v, rel_bias):
    b, seq, _ = k.shape
    o = _attn_prompt(q, k, v, rel_bias)
    n_keep = min(MAX_WINDOW, seq)
    shape = (b, n_keep, ATT_HEADS, HEAD_DIM)
    return o, k[:, seq - n_keep:].reshape(shape), v[:, seq - n_keep:].reshape(shape)


def _sample_attend(q, k, v, buf_k, buf_v, rel_bias):
    return _attn_sample(q, k, v, buf_k, buf_v, rel_bias)
```

```python
import functools
import math

import jax
import jax.numpy as jnp
from jax import lax
from jax.experimental import pallas as pl
from jax.experimental.pallas import tpu as pltpu
from jax.experimental.pallas import tpu_sc as plsc

F32 = jnp.float32
BF16 = jnp.bfloat16

DIL_PATTERNS = ((128, 1), (512, 4), (2048, 16))
N_DIL = len(DIL_PATTERNS)
ATT_HEADS = 8
HEAD_DIM = 128
ATT_WIDTH = ATT_HEADS * HEAD_DIM
ATT_SCALE = HEAD_DIM ** -0.5
MAX_WINDOW = 2048
NUM_BUCKETS = 32
MAX_DISTANCE = 2048
SSM_HEAD_DIM = 64
SSM_GROUPS = 4
D_STATE = 128
D_CONV = 4
SSD_CHUNK = 128
TOP_K = 8
N_EXPERT_GROUPS = 8
TOPK_GROUPS = 4
ROUTE_SCALE = 2.5
RMS_EPS = 1e-6
LOG2_E = math.log2(math.e)

LANES = 128
SUBLANES = 8
VMEM_LIMIT = 56 * 1024 * 1024
EXPERT_ROWS = 512
EXPERT_BUFFERS = 4
IN_PROJ_CALLS = (('q',), ('k', 'v', 'z'), ('xbc', 'dt', 'gate'))
SC_CORES = 2
SC_SUBCORES = 16
SC_WORKERS = SC_CORES * SC_SUBCORES
NT_DIMS = (((1,), (1,)), ((), ()))


def _params(*sem):
    return pltpu.CompilerParams(dimension_semantics=sem, vmem_limit_bytes=VMEM_LIMIT)


def _sigmoid(x):
    return 0.5 * jnp.tanh(0.5 * x) + 0.5


def _silu(x):
    return x * _sigmoid(x)


def _rms_rows(x, g):
    return x * lax.rsqrt(jnp.mean(x * x, axis=-1, keepdims=True) + RMS_EPS) * g


def _pack_pairs(x):
    n = x.shape[1] // 2
    bits = pltpu.bitcast(x.astype(BF16).astype(F32), jnp.uint32)
    return bits[:, :n] | (bits[:, n:] >> jnp.uint32(16))


def _unpack_pairs(p):
    hi = pltpu.bitcast(p & jnp.uint32(0xFFFF0000), F32)
    lo = pltpu.bitcast(p << jnp.uint32(16), F32)
    return jnp.concatenate([hi, lo], axis=1)


def _mod_kernel(c_ref, w_ref, b_ref, o_ref):
    s = _silu(c_ref[...]).astype(BF16)
    o_ref[...] = jnp.dot(s, w_ref[...].astype(BF16), preferred_element_type=F32) + b_ref[...]


def _modulation(c, w_mod, b_mod):
    m, d = c.shape
    n = w_mod.shape[1]
    tn = n // 4
    return pl.pallas_call(
        _mod_kernel,
        grid=(n // tn,),
        in_specs=[pl.BlockSpec((m, d), lambda j: (0, 0)),
                  pl.BlockSpec((d, tn), lambda j: (0, j)),
                  pl.BlockSpec((1, tn), lambda j: (0, j))],
        out_specs=pl.BlockSpec((m, tn), lambda j: (0, j)),
        out_shape=jax.ShapeDtypeStruct((m, n), F32),
        compiler_params=_params("arbitrary"),
        name="modulation",
    )(c, w_mod, b_mod.reshape(1, n))


class _Mod:
    def __init__(self, mod, seq, tm, d):
        self.d = d
        b = mod.shape[0]
        if seq % tm == 0:
            per = seq // tm
            self.arr = mod.reshape(b, 1, mod.shape[1])
            self._spec = lambda col: pl.BlockSpec((None, 1, d), lambda i: (i // per, 0, col))
        else:
            self.arr = jnp.repeat(mod, seq, axis=0)
            self._spec = lambda col: pl.BlockSpec((tm, d), lambda i: (i, col))

    def spec(self, col):
        return self._spec(col)


def _prenorm_kernel(x_ref, g_ref, sc_ref, sh_ref, o_ref):
    y = _rms_rows(x_ref[...], g_ref[...])
    o_ref[...] = (y * (1 + sc_ref[...]) + sh_ref[...]).astype(o_ref.dtype)


def _prenorm(x2, g, mod, col_scale, col_shift, tm):
    t, d = x2.shape
    return pl.pallas_call(
        _prenorm_kernel,
        grid=(t // tm,),
        in_specs=[pl.BlockSpec((tm, d), lambda i: (i, 0)),
                  pl.BlockSpec((1, d), lambda i: (0, 0)),
                  mod.spec(col_scale), mod.spec(col_shift)],
        out_specs=pl.BlockSpec((tm, d), lambda i: (i, 0)),
        out_shape=jax.ShapeDtypeStruct((t, d), BF16),
        compiler_params=_params("parallel"),
        name="prenorm",
    )(x2, g.reshape(1, d), mod.arr, mod.arr)


def _mm_kernel(h_ref, *refs):
    n = len(refs) // 2
    h = h_ref[...]
    for w_ref, o_ref in zip(refs[:n], refs[n:]):
        o_ref[...] = jnp.dot(h, w_ref[...], preferred_element_type=F32).astype(o_ref.dtype)


def _matmuls(h, weights, tm, name):
    t, k = h.shape
    return pl.pallas_call(
        _mm_kernel,
        grid=(t // tm,),
        in_specs=[pl.BlockSpec((tm, k), lambda i: (i, 0))]
        + [pl.BlockSpec(w.shape, lambda i: (0, 0)) for w in weights],
        out_specs=[pl.BlockSpec((tm, w.shape[1]), lambda i: (i, 0)) for w in weights],
        out_shape=[jax.ShapeDtypeStruct((t, w.shape[1]), F32) for w in weights],
        compiler_params=_params("parallel"),
        name=name,
    )(h, *weights)


def _bucket(dist):
    max_exact = NUM_BUCKETS // 2
    far = max_exact + (jnp.log(jnp.maximum(dist, 1).astype(F32) / max_exact)
                       / math.log(MAX_DISTANCE / max_exact) * (NUM_BUCKETS - max_exact)).astype(jnp.int32)
    return jnp.where(dist < max_exact, dist, jnp.minimum(far, NUM_BUCKETS - 1))


def _bias_by_distance(rel_bias, n):
    hit = _bucket(jnp.arange(n, dtype=jnp.int32))[:, None] == jnp.arange(NUM_BUCKETS)[None, :]
    return jnp.sum(jnp.where(hit[:, :, None], rel_bias.astype(F32)[None], 0.0), axis=1)


def _band_bias(rel_bias, w):
    by_dist = _bias_by_distance(rel_bias, max(win for win, _ in DIL_PATTERNS) + 1)
    ext = 3 * w
    out = []
    for g, (_, dil) in enumerate(DIL_PATTERNS):
        vec = by_dist[::dil][:w + 1, g * ATT_HEADS:(g + 1) * ATT_HEADS].T
        v = jnp.concatenate([vec[:, ::-1], jnp.full((ATT_HEADS, ext - w - 1), -jnp.inf, F32)], axis=1)
        skew = jnp.tile(v, (1, w))[:, :w * (ext - 1)].reshape(ATT_HEADS, w, ext - 1)
        out.append(skew[:, :, :2 * w])
    return jnp.concatenate(out, axis=0)


def _attn_prompt_kernel(q0_ref, q1_ref, q2_ref, k_ref, v_ref, b0_ref, b1_ref, b2_ref, o_ref,
                        og_scr, lse_scr, *, seq, w):
    q_refs = (q0_ref, q1_ref, q2_ref)
    b_refs = (b0_ref, b1_ref, b2_ref)

    def rows(start, size, dil):
        return pl.ds(start, size) if dil == 1 else pl.ds(start, size, stride=dil)

    for g, (_, dil) in enumerate(DIL_PATTERNS):
        nb = seq // dil // w
        for r in range(dil):
            for n in range(nb):
                q_rows = rows(r + dil * n * w, w, dil)
                qb = q_refs[g][q_rows, :].astype(BF16)
                if n == 0:
                    k_rows = rows(r, w, dil)
                    bias = b_refs[g][:, w:]
                else:
                    k_rows = rows(r + dil * (n - 1) * w, 2 * w, dil)
                    bias = b_refs[g][...]
                kb = k_ref[k_rows, :].astype(BF16)
                vb = v_ref[k_rows, :].astype(BF16)
                s = lax.dot_general(qb, kb, NT_DIMS, preferred_element_type=F32) * (ATT_SCALE * LOG2_E) + bias
                m = jnp.max(s, axis=-1, keepdims=True)
                p = jnp.exp2(s - m)
                den = jnp.sum(p, axis=-1, keepdims=True)
                o = jnp.dot(p.astype(BF16), vb, preferred_element_type=F32) / den
                og_scr[g, q_rows, :] = o
                lse_scr[g, q_rows, :] = jnp.broadcast_to(m + jnp.log2(den), (w, LANES))

    step = 256
    for c in range(seq // step):
        sl = pl.ds(c * step, step)
        l0, l1, l2 = lse_scr[0, sl, :], lse_scr[1, sl, :], lse_scr[2, sl, :]
        mm = jnp.maximum(jnp.maximum(l0, l1), l2)
        e0, e1, e2 = jnp.exp2(l0 - mm), jnp.exp2(l1 - mm), jnp.exp2(l2 - mm)
        tot = e0 + e1 + e2
        o = (e0 / tot) * og_scr[0, sl, :] + (e1 / tot) * og_scr[1, sl, :] + (e2 / tot) * og_scr[2, sl, :]
        o_ref[sl, :] = o.astype(o_ref.dtype)


def _attn_prompt(q, k, v, rel_bias):
    b, seq, _ = k.shape
    w = DIL_PATTERNS[0][0] // DIL_PATTERNS[0][1]
    for win, dil in DIL_PATTERNS:
        assert win // dil == w and seq % (dil * w) == 0
    bias = _band_bias(rel_bias, w) * LOG2_E
    hd = HEAD_DIM

    def q_spec(g):
        return pl.BlockSpec((None, seq, hd), lambda i, h: (i, 0, g * ATT_HEADS + h))

    def b_spec(g):
        return pl.BlockSpec((None, w, 2 * w), lambda i, h: (g * ATT_HEADS + h, 0, 0))

    kv_spec = pl.BlockSpec((None, seq, hd), lambda i, h: (i, 0, h))
    return pl.pallas_call(
        functools.partial(_attn_prompt_kernel, seq=seq, w=w),
        grid=(b, ATT_HEADS),
        in_specs=[q_spec(0), q_spec(1), q_spec(2), kv_spec, kv_spec, b_spec(0), b_spec(1), b_spec(2)],
        out_specs=pl.BlockSpec((None, seq, hd), lambda i, h: (i, 0, h)),
        out_shape=jax.ShapeDtypeStruct((b, seq, ATT_WIDTH), BF16),
        scratch_shapes=[pltpu.VMEM((N_DIL, seq, hd), F32), pltpu.VMEM((N_DIL, seq, LANES), F32)],
        compiler_params=_params("parallel", "parallel"),
        name="attn_prompt",
    )(q, q, q, k, v, bias, bias, bias)


SAMPLE_KEY_CHUNK = 1024


def _attn_sample_kernel(q_ref, kn_ref, vn_ref, kn4_ref, vn4_ref, ck_ref, cv_ref, bm_ref, bn_ref,
                        o_ref, wk_ref, wv_ref, m_scr, l_scr, acc_scr, ck_scr, cv_scr, *, t, nch, n_full):
    c = pl.program_id(1)
    ch = ck_ref.shape[0]
    tr = t * ATT_HEADS
    rows = N_DIL * t

    @pl.when(c == 0)
    def _():
        m_scr[...] = jnp.full(m_scr.shape, -jnp.inf, F32)
        l_scr[...] = jnp.zeros(l_scr.shape, F32)
        acc_scr[...] = jnp.zeros(acc_scr.shape, F32)
        ck_scr[...] = kn4_ref[...]
        cv_scr[...] = vn4_ref[...]

    wk_ref[0:ch - tr, :] = ck_ref[tr:ch, :]
    wk_ref[ch - tr:ch, :] = ck_scr[...]
    ck_scr[...] = ck_ref[0:tr, :]
    wv_ref[0:ch - tr, :] = cv_ref[tr:ch, :]
    wv_ref[ch - tr:ch, :] = cv_scr[...]
    cv_scr[...] = cv_ref[0:tr, :]

    def head_q(h):
        return jnp.concatenate(
            [q_ref[:, g * ATT_WIDTH + h * HEAD_DIM:g * ATT_WIDTH + (h + 1) * HEAD_DIM] for g in range(N_DIL)], axis=0)

    def attend(h, r0):
        rs = slice(r0, rows)
        qa = head_q(h)[r0:].astype(BF16)
        head_rows = pl.ds(h, ch // ATT_HEADS, stride=ATT_HEADS)
        kh = ck_ref[head_rows, :].astype(BF16)
        vh = cv_ref[head_rows, :].astype(BF16)
        keys = ch // ATT_HEADS
        bias = bm_ref[h, rs, pl.ds(pl.multiple_of((nch - 1 - c) * keys, LANES), keys)]
        s = lax.dot_general(qa, kh, NT_DIMS, preferred_element_type=F32) * ATT_SCALE + bias
        m_old = m_scr[h, rs, :]
        m_new = jnp.maximum(m_old, jnp.max(s, axis=-1, keepdims=True))
        alpha = jnp.exp(m_old - m_new)
        p = jnp.exp(s - m_new[:, 0:1])
        l_scr[h, rs, :] = alpha * l_scr[h, rs, :] + jnp.sum(p, axis=-1, keepdims=True)
        acc_scr[h, rs, :] = alpha * acc_scr[h, rs, :] + jnp.dot(p.astype(BF16), vh, preferred_element_type=F32)
        m_scr[h, rs, :] = m_new

    @pl.when(c < n_full)
    def _():
        for h in range(ATT_HEADS):
            attend(h, 0)

    @pl.when(c >= n_full)
    def _():
        for h in range(ATT_HEADS):
            attend(h, rows - t)

    @pl.when(c == nch - 1)
    def _():
        for h in range(ATT_HEADS):
            cs = slice(h * HEAD_DIM, (h + 1) * HEAD_DIM)
            qa = head_q(h)
            s_new = [jnp.sum(qa * kn_ref[j:j + 1, cs], axis=-1, keepdims=True) * ATT_SCALE + bn_ref[h, j]
                     for j in range(t)]
            m_old = m_scr[h]
            m_new = m_old
            for sj in s_new:
                m_new = jnp.maximum(m_new, sj)
            alpha = jnp.exp(m_old - m_new)
            den = alpha * l_scr[h]
            acc = alpha * acc_scr[h]
            for j, sj in enumerate(s_new):
                pj = jnp.exp(sj - m_new)
                den = den + pj
                acc = acc + pj * vn_ref[j:j + 1, cs]
            o = acc / den
            lse = m_new + jnp.log(den)
            l0, l1, l2 = lse[0:t], lse[t:2 * t], lse[2 * t:rows]
            mm = jnp.maximum(jnp.maximum(l0, l1), l2)
            e0, e1, e2 = jnp.exp(l0 - mm), jnp.exp(l1 - mm), jnp.exp(l2 - mm)
            tot = e0 + e1 + e2
            o_ref[:, cs] = (e0 / tot) * o[0:t] + (e1 / tot) * o[t:2 * t] + (e2 / tot) * o[2 * t:rows]


def _sample_bias(rel_bias, t, n_buf):
    n_key = n_buf + t
    by_dist = _bias_by_distance(rel_bias, n_key)
    dist = jnp.arange(n_key)
    full = []
    for g, (win, dil) in enumerate(DIL_PATTERNS):
        ok = (dist % dil == 0) & (dist <= win)
        vec = jnp.where(ok[:, None], by_dist[:, g * ATT_HEADS:(g + 1) * ATT_HEADS], -jnp.inf)
        rev = jnp.concatenate([vec[::-1], jnp.full((t - 1, ATT_HEADS), -jnp.inf, F32)], axis=0)
        full.append(jnp.stack([rev[t - 1 - tok:t - 1 - tok + n_key] for tok in range(t)], axis=0))
    full = jnp.transpose(jnp.concatenate(full, axis=0), (2, 0, 1))
    new = jnp.transpose(full[:, :, n_buf:], (0, 2, 1))
    new = jnp.broadcast_to(new[..., None], new.shape + (LANES,))
    return full[:, :, :n_buf], new


def _attn_sample(q, k, v, cache_k, cache_v, rel_bias):
    b, t, _ = k.shape
    n_buf = cache_k.shape[1]
    ch = SAMPLE_KEY_CHUNK
    assert t == SUBLANES and n_buf % ch == 0
    assert all(win <= n_buf for win, _ in DIL_PATTERNS)
    nch = n_buf // ch
    assert all(win <= DIL_PATTERNS[-1][0] for win, _ in DIL_PATTERNS)
    n_full = min(nch, max(-(-win // ch) for win, _ in DIL_PATTERNS[:-1]))
    bm, bn = _sample_bias(rel_bias, t, n_buf)
    flat = lambda a: a.reshape(b, -1, HEAD_DIM)
    rows = N_DIL * t
    tr = t * ATT_HEADS

    def per_b(*shape):
        return pl.BlockSpec((None,) + shape, lambda i, c: (i,) + (0,) * len(shape))

    buf_spec = pl.BlockSpec((None, ch * ATT_HEADS, HEAD_DIM), lambda i, c: (i, nch - 1 - c, 0))
    buf_shape = jax.ShapeDtypeStruct((b, n_buf * ATT_HEADS, HEAD_DIM), F32)
    o, win_k, win_v = pl.pallas_call(
        functools.partial(_attn_sample_kernel, t=t, nch=nch, n_full=n_full),
        grid=(b, nch),
        in_specs=[per_b(t, N_DIL * ATT_WIDTH), per_b(t, ATT_WIDTH), per_b(t, ATT_WIDTH),
                  per_b(tr, HEAD_DIM), per_b(tr, HEAD_DIM), buf_spec, buf_spec,
                  pl.BlockSpec((ATT_HEADS, rows, n_buf), lambda i, c: (0, 0, 0)),
                  pl.BlockSpec((ATT_HEADS, t, rows, LANES), lambda i, c: (0, 0, 0, 0))],
        out_specs=[per_b(t, ATT_WIDTH), buf_spec, buf_spec],
        out_shape=[jax.ShapeDtypeStruct((b, t, ATT_WIDTH), F32), buf_shape, buf_shape],
        scratch_shapes=[pltpu.VMEM((ATT_HEADS, rows, LANES), F32), pltpu.VMEM((ATT_HEADS, rows, LANES), F32),
                        pltpu.VMEM((ATT_HEADS, rows, HEAD_DIM), F32),
                        pltpu.VMEM((tr, HEAD_DIM), F32), pltpu.VMEM((tr, HEAD_DIM), F32)],
        compiler_params=_params("parallel", "arbitrary"),
        name="attn_sample",
    )(q, k, v, flat(k), flat(v), flat(cache_k), flat(cache_v), bm, bn)
    return o, win_k.reshape(cache_k.shape), win_v.reshape(cache_v.shape)


def _softplus(x):
    return jnp.maximum(x, 0.0) + jnp.log(1.0 + jnp.exp(-jnp.abs(x)))


def _ssd_kernel(xbc_ref, z_ref, dt_ref, cbuf_ref, h0_ref, cw_ref, cb_ref, dtb_ref, alog_ref, dsk_ref, gn_ref,
                y_ref, hl_ref, ext_scr, st_scr, y_scr, acst_scr, dtt_scr, wstt_scr, *, lv, nc, d_inner):
    q = SSD_CHUNK
    n = D_STATE
    c = pl.program_id(1)
    n_pairs = d_inner // LANES
    pairs_per_group = n_pairs // SSM_GROUPS

    @pl.when(c == 0)
    def _():
        ext_scr[0:SUBLANES, :] = cbuf_ref[...]
        for i in range(n_pairs):
            st_scr[:, i * LANES:(i + 1) * LANES] = h0_ref[i * LANES:(i + 1) * LANES, :].T

    ext_scr[SUBLANES:SUBLANES + lv, :] = xbc_ref[...]
    if lv < q:
        ext_scr[SUBLANES + lv:SUBLANES + q, :] = jnp.zeros((q - lv, ext_scr.shape[1]), F32)
    first = SUBLANES - (D_CONV - 1)
    u = cb_ref[...] + cw_ref[0:1, :] * ext_scr[first:first + q, :]
    for j in range(1, D_CONV):
        u = u + cw_ref[j:j + 1, :] * ext_scr[first + j:first + j + q, :]
    u = _silu(u)
    if nc > 1:
        ext_scr[0:SUBLANES, :] = ext_scr[q:q + SUBLANES, :]

    dtv = _softplus(dt_ref_rows(dt_ref, lv, q) + dtb_ref[...])
    if lv < q:
        row = lax.broadcasted_iota(jnp.int32, (q, LANES), 0)
        dtv = jnp.where(row < lv, dtv, 0.0)
    a = -jnp.exp(alog_ref[...])
    da = dtv * a
    li = lax.broadcasted_iota(jnp.int32, (q, q), 0)
    si = lax.broadcasted_iota(jnp.int32, (q, q), 1)
    tri = li >= si
    acs = jnp.dot(tri.astype(F32), da, preferred_element_type=F32, precision=lax.Precision.HIGHEST) * LOG2_E
    acs_last = acs[q - 1:q, :]
    acst_scr[...] = acs.T
    dtt_scr[...] = dtv.T
    wstt_scr[...] = (jnp.exp2(acs_last - acs) * dtv).T

    lane = lax.broadcasted_iota(jnp.int32, (1, LANES), 1)
    left = lane < SSM_HEAD_DIM
    for g in range(SSM_GROUPS):
        bg = u[:, d_inner + g * n:d_inner + (g + 1) * n]
        cg = u[:, d_inner + (SSM_GROUPS + g) * n:d_inner + (SSM_GROUPS + g + 1) * n].astype(BF16)
        cb = lax.dot_general(cg, bg.astype(BF16), NT_DIMS, preferred_element_type=F32)
        bgt = bg.T
        for jp in range(pairs_per_group):
            pair = g * pairs_per_group + jp
            cols = slice(pair * LANES, (pair + 1) * LANES)
            x_pair = u[:, cols]
            lhs, e_col, dec = [], [], []
            for h in (2 * pair, 2 * pair + 1):
                a_col = jnp.broadcast_to(acs[:, h:h + 1], (q, q))
                seg = a_col - acst_scr[h:h + 1, :]
                m_h = cb * jnp.exp2(jnp.where(tri, seg, -jnp.inf)) * dtt_scr[h:h + 1, :]
                lhs.append(m_h.astype(BF16))
                e_col.append(jnp.exp2(a_col))
                dec.append(jnp.exp2(a_col[q - 1:q, :]))
            for h in (2 * pair, 2 * pair + 1):
                lhs.append((bgt * wstt_scr[h:h + 1, :]).astype(BF16))
            res = jnp.dot(jnp.concatenate(lhs, axis=0), x_pair.astype(BF16), preferred_element_type=F32)
            y_diag = jnp.where(left, res[0:q], res[q:2 * q])
            d_state = jnp.where(left, res[2 * q:2 * q + n], res[2 * q + n:2 * q + 2 * n])
            st = st_scr[:, cols]
            y_off = jnp.dot(cg, st.astype(BF16), preferred_element_type=F32) * jnp.where(left, e_col[0], e_col[1])
            st_scr[:, cols] = st * jnp.where(left, dec[0], dec[1]) + d_state
            y_scr[:, cols] = y_diag + y_off + dsk_ref[:, cols] * x_pair

    gw = d_inner // SSM_GROUPS
    for g in range(SSM_GROUPS):
        cols = slice(g * gw, (g + 1) * gw)
        yg = y_scr[0:lv, cols] * _silu(z_ref[:, cols])
        yg = yg * lax.rsqrt(jnp.mean(yg * yg, axis=-1, keepdims=True) + RMS_EPS)
        y_ref[:, cols] = (yg * gn_ref[:, cols]).astype(y_ref.dtype)

    @pl.when(c == nc - 1)
    def _():
        for i in range(n_pairs):
            hl_ref[i * LANES:(i + 1) * LANES, :] = st_scr[:, i * LANES:(i + 1) * LANES].T


def dt_ref_rows(dt_ref, lv, q):
    if lv == q:
        return dt_ref[...]
    return jnp.concatenate([dt_ref[...], jnp.zeros((q - lv, dt_ref.shape[1]), F32)], axis=0)


def _ssd_branch(xbc, z, dt, conv_buf, h0, conv_w, conv_b, dt_bias, a_log, d_skip, g_norm, out_dtype):
    b, seqlen, conv_dim = xbc.shape
    d_inner = z.shape[2]
    heads = d_inner // SSM_HEAD_DIM
    q = SSD_CHUNK
    lv = min(q, seqlen)
    assert seqlen % lv == 0 and lv % SUBLANES == 0 and heads <= LANES
    nc = seqlen // lv
    cbuf = jnp.pad(conv_buf, ((0, 0), (SUBLANES - (D_CONV - 1), 0), (0, 0)))
    h0f = h0.reshape(b, heads * SSM_HEAD_DIM, D_STATE)
    pad = LANES - heads
    row = lambda a: jnp.pad(a.astype(F32), (0, pad)).reshape(1, LANES)
    dsk = jnp.repeat(d_skip.astype(F32), SSM_HEAD_DIM).reshape(1, d_inner)

    def per_chunk(width):
        return pl.BlockSpec((None, lv, width), lambda i, c: (i, c, 0))

    def per_batch(r, width):
        return pl.BlockSpec((None, r, width), lambda i, c: (i, 0, 0))

    def const(r, width):
        return pl.BlockSpec((r, width), lambda i, c: (0, 0))

    y, h_last = pl.pallas_call(
        functools.partial(_ssd_kernel, lv=lv, nc=nc, d_inner=d_inner),
        grid=(b, nc),
        in_specs=[per_chunk(conv_dim), per_chunk(d_inner), per_chunk(LANES),
                  per_batch(SUBLANES, conv_dim), per_batch(heads * SSM_HEAD_DIM, D_STATE),
                  const(D_CONV, conv_dim), const(1, conv_dim), const(1, LANES), const(1, LANES),
                  const(1, d_inner), const(1, d_inner)],
        out_specs=[per_chunk(d_inner), per_batch(heads * SSM_HEAD_DIM, D_STATE)],
        out_shape=[jax.ShapeDtypeStruct((b, seqlen, d_inner), out_dtype),
                   jax.ShapeDtypeStruct((b, heads * SSM_HEAD_DIM, D_STATE), F32)],
        scratch_shapes=[pltpu.VMEM((q + 2 * SUBLANES, conv_dim), F32),
                        pltpu.VMEM((D_STATE, d_inner), F32),
                        pltpu.VMEM((q, d_inner), F32),
                        pltpu.VMEM((LANES, q), F32), pltpu.VMEM((LANES, q), F32), pltpu.VMEM((LANES, q), F32)],
        compiler_params=_params("parallel", "arbitrary"),
        name="ssd",
    )(xbc, z, dt, cbuf, h0f, conv_w, conv_b.reshape(1, conv_dim), row(dt_bias), row(a_log), dsk,
      g_norm.reshape(1, d_inner))
    return y, h_last.reshape(b, heads, SSM_HEAD_DIM, D_STATE)


def _mix_kernel(oa_ref, ys_ref, gate_ref, x_ref, gt_ref, wa_ref, wb_ref, wo_ref, g_ref, o_ref, *, d):
    ya = jnp.dot(oa_ref[...].astype(BF16), wa_ref[...], preferred_element_type=F32)
    yb = jnp.dot(ys_ref[...].astype(BF16), wb_ref[...], preferred_element_type=F32)
    mixed = _sigmoid(gate_ref[:, 0:d]) * ya + _sigmoid(gate_ref[:, d:2 * d]) * yb
    mix = jnp.dot(mixed.astype(BF16), wo_ref[...], preferred_element_type=F32)
    o_ref[...] = x_ref[...] + gt_ref[...] * _rms_rows(mix, g_ref[...])


def _mix(o_att, y_ssm, gates, x2, mod, wa, wb, wo, g_post, tm):
    t, d = x2.shape
    full = lambda a: pl.BlockSpec(a.shape, lambda i: (0, 0))
    rows = lambda a: pl.BlockSpec((tm, a.shape[1]), lambda i: (i, 0))
    g2 = g_post.reshape(1, d)
    return pl.pallas_call(
        functools.partial(_mix_kernel, d=d),
        grid=(t // tm,),
        in_specs=[rows(o_att), rows(y_ssm), rows(gates), rows(x2), mod.spec(2), full(wa), full(wb), full(wo),
                  full(g2)],
        out_specs=pl.BlockSpec((tm, d), lambda i: (i, 0)),
        out_shape=jax.ShapeDtypeStruct((t, d), F32),
        compiler_params=_params("parallel"),
        name="mix",
    )(o_att, y_ssm, gates, x2, mod.arr, wa, wb, wo, g2)


def _first_index(hit, iota, size):
    return jnp.min(jnp.where(hit, iota, size), axis=0, keepdims=True)


def _ffn_pre_kernel(x_ref, g_ref, sc_ref, sh_ref, wr_ref, br_ref, h_ref, idx_ref, w_ref, *, n_exp):
    y = _rms_rows(x_ref[...], g_ref[...])
    hf = y * (1 + sc_ref[...]) + sh_ref[...]
    h_ref[...] = _pack_pairs(hf)
    h = hf.astype(BF16)
    tm = h.shape[0]
    scores = jax.nn.sigmoid(lax.dot_general(wr_ref[...], h, NT_DIMS, preferred_element_type=F32))
    biased = scores + br_ref[...]
    gsz = n_exp // N_EXPERT_GROUPS
    riota = lax.broadcasted_iota(jnp.int32, (gsz, tm), 0)
    gs = []
    for g in range(N_EXPERT_GROUPS):
        xg = biased[g * gsz:(g + 1) * gsz, :]
        m1 = jnp.max(xg, axis=0, keepdims=True)
        i1 = _first_index(xg == m1, riota, gsz)
        m2 = jnp.max(jnp.where(riota == i1, -jnp.inf, xg), axis=0, keepdims=True)
        gs.append(m1 + m2)
    gs = jnp.concatenate(gs, axis=0)
    giota = lax.broadcasted_iota(jnp.int32, gs.shape, 0)
    keep = jnp.zeros(gs.shape, F32)
    for _ in range(TOPK_GROUPS):
        gi = _first_index(gs == jnp.max(gs, axis=0, keepdims=True), giota, N_EXPERT_GROUPS)
        sel = giota == gi
        keep = jnp.where(sel, 1.0, keep)
        gs = jnp.where(sel, -jnp.inf, gs)
    masked = jnp.concatenate(
        [jnp.where(keep[g:g + 1, :] > 0.5, biased[g * gsz:(g + 1) * gsz, :], -jnp.inf)
         for g in range(N_EXPERT_GROUPS)], axis=0)
    eiota = lax.broadcasted_iota(jnp.int32, (n_exp, tm), 0)
    idxs, ws = [], []
    for _ in range(TOP_K):
        ik = _first_index(masked == jnp.max(masked, axis=0, keepdims=True), eiota, n_exp)
        hit = eiota == ik
        idxs.append(ik)
        ws.append(jnp.sum(jnp.where(hit, scores, 0.0), axis=0, keepdims=True))
        masked = jnp.where(hit, -jnp.inf, masked)
    idx_ref[...] = jnp.concatenate(idxs, axis=0)
    w = jnp.concatenate(ws, axis=0)
    w_ref[...] = w / jnp.sum(w, axis=0, keepdims=True) * ROUTE_SCALE


def _ffn_pre(x1, g, mod, w_router_t, b_router, tm):
    t, d = x1.shape
    e = w_router_t.shape[0]
    br = jnp.broadcast_to(b_router.astype(F32)[:, None], (e, tm))
    return pl.pallas_call(
        functools.partial(_ffn_pre_kernel, n_exp=e),
        grid=(t // tm,),
        in_specs=[pl.BlockSpec((tm, d), lambda i: (i, 0)),
                  pl.BlockSpec((1, d), lambda i: (0, 0)),
                  mod.spec(4), mod.spec(3),
                  pl.BlockSpec((e, d), lambda i: (0, 0)),
                  pl.BlockSpec((e, tm), lambda i: (0, 0))],
        out_specs=[pl.BlockSpec((tm, d // 2), lambda i: (i, 0)),
                   pl.BlockSpec((TOP_K, tm), lambda i: (0, i)),
                   pl.BlockSpec((TOP_K, tm), lambda i: (0, i))],
        out_shape=[jax.ShapeDtypeStruct((t, d // 2), jnp.uint32), jax.ShapeDtypeStruct((TOP_K, t), jnp.int32),
                   jax.ShapeDtypeStruct((TOP_K, t), F32)],
        compiler_params=_params("parallel"),
        name="ffn_pre",
    )(x1, g.reshape(1, d), mod.arr, mod.arr, w_router_t, br)


def _rank_kernel(idx_ref, rank_ref, cnt_ref, base_scr, *, n_exp):
    i = pl.program_id(0)
    tm = idx_ref.shape[1]

    @pl.when(i == 0)
    def _():
        base_scr[...] = jnp.zeros(base_scr.shape, F32)

    eiota = lax.broadcasted_iota(jnp.int32, (n_exp, tm), 0)
    idx = idx_ref[...]
    hits = [eiota == idx[k:k + 1, :] for k in range(TOP_K)]
    multi = jnp.zeros((n_exp, tm), F32)
    for hit in hits:
        multi = multi + jnp.where(hit, 1.0, 0.0)
    multi = multi.astype(BF16)
    ti = lax.broadcasted_iota(jnp.int32, (tm, tm), 0)
    tj = lax.broadcasted_iota(jnp.int32, (tm, tm), 1)
    earlier = jnp.where(ti < tj, 1.0, 0.0).astype(BF16)
    before = jnp.dot(multi, earlier, preferred_element_type=F32) + base_scr[:, 0:1]
    ranks = [jnp.sum(jnp.where(hit, before, 0.0), axis=0, keepdims=True) for hit in hits]
    rank_ref[...] = jnp.concatenate(ranks, axis=0).astype(jnp.int32)
    base_scr[...] = base_scr[...] + jnp.dot(multi, jnp.ones((tm, LANES), BF16), preferred_element_type=F32)
    cnt_ref[...] = base_scr[...]


def _expert_ranks(idx_t, n_exp, tm):
    t = idx_t.shape[1]
    return pl.pallas_call(
        functools.partial(_rank_kernel, n_exp=n_exp),
        grid=(t // tm,),
        in_specs=[pl.BlockSpec((TOP_K, tm), lambda i: (0, i))],
        out_specs=[pl.BlockSpec((TOP_K, tm), lambda i: (0, i)), pl.BlockSpec((n_exp, LANES), lambda i: (0, 0))],
        out_shape=[jax.ShapeDtypeStruct((TOP_K, t), jnp.int32), jax.ShapeDtypeStruct((n_exp, LANES), F32)],
        scratch_shapes=[pltpu.VMEM((n_exp, LANES), F32)],
        compiler_params=_params("arbitrary"),
        name="expert_ranks",
    )(idx_t)


def _pos_kernel(idx_ref, rank_ref, start_ref, pos_ref, *, n_exp):
    tm = idx_ref.shape[1]
    eiota = lax.broadcasted_iota(jnp.int32, (n_exp, tm), 0)
    idx = idx_ref[...]
    start = start_ref[:, 0:1]
    offs = [jnp.sum(jnp.where(eiota == idx[k:k + 1, :], start, 0.0), axis=0, keepdims=True) for k in range(TOP_K)]
    pos_ref[...] = rank_ref[...] + jnp.concatenate(offs, axis=0).astype(jnp.int32)


def _positions(idx_t, rank_t, pad_start, tm):
    t = idx_t.shape[1]
    n_exp = pad_start.shape[0]
    start = jnp.broadcast_to(pad_start.astype(F32)[:, None], (n_exp, LANES))
    spec = pl.BlockSpec((TOP_K, tm), lambda i: (0, i))
    return pl.pallas_call(
        functools.partial(_pos_kernel, n_exp=n_exp),
        grid=(t // tm,),
        in_specs=[spec, spec, pl.BlockSpec((n_exp, LANES), lambda i: (0, 0))],
        out_specs=spec,
        out_shape=jax.ShapeDtypeStruct((TOP_K, t), jnp.int32),
        compiler_params=_params("parallel"),
        name="positions",
    )(idx_t, rank_t, start)


def _experts_kernel(first_ref, count_ref, used_ref, x_hbm, wg_ref, wu_ref, wd_ref, o_hbm,
                    wg_scr, wu_scr, wd_scr, x_vmem, o_vmem, x_sem, o_sem):
    e = pl.program_id(0)
    nbuf, blk = x_vmem.shape[0], x_vmem.shape[1]
    first, count, used = first_ref[e], count_ref[e], used_ref[0]

    def block_rows(g):
        return pl.ds(pl.multiple_of(g * blk, blk), blk)

    def x_copy(g):
        slot = lax.rem(g, nbuf)
        return pltpu.make_async_copy(x_hbm.at[block_rows(g), :], x_vmem.at[slot], x_sem.at[slot])

    def o_copy(g):
        slot = lax.rem(g, nbuf)
        return pltpu.make_async_copy(o_vmem.at[slot], o_hbm.at[block_rows(g), :], o_sem.at[slot])

    @pl.when(e == 0)
    def _():
        for g in range(nbuf - 1):
            @pl.when(g < used)
            def _():
                x_copy(g).start()

    @pl.when(count > 0)
    def _():
        wg_scr[...] = wg_ref[...].astype(BF16)
        wu_scr[...] = wu_ref[...].astype(BF16)
        wd_scr[...] = wd_ref[...].astype(BF16)

    def one_block(j, carry):
        g = first + j
        slot = lax.rem(g, nbuf)
        x_copy(g).wait()

        @pl.when(g + nbuf - 1 < used)
        def _():
            x_copy(g + nbuf - 1).start()

        xb = _unpack_pairs(x_vmem[slot]).astype(BF16)
        act = _silu(jnp.dot(xb, wg_scr[...], preferred_element_type=F32)) * jnp.dot(
            xb, wu_scr[...], preferred_element_type=F32)
        res = _pack_pairs(jnp.dot(act.astype(BF16), wd_scr[...], preferred_element_type=F32))

        @pl.when(g >= nbuf)
        def _():
            o_copy(g - nbuf).wait()

        o_vmem[slot] = res
        o_copy(g).start()
        return carry

    lax.fori_loop(0, count, one_block, 0)

    @pl.when(e == pl.num_programs(0) - 1)
    def _():
        for back in range(nbuf, 0, -1):
            @pl.when(used >= back)
            def _():
                o_copy(used - back).wait()


def _sc_dispatch(h, pos_flat, rows):
    t, w = h.shape
    chunk, nbuf = 48, 2
    per_w = t // SC_WORKERS
    assert t % SC_WORKERS == 0 and per_w % (chunk * nbuf) == 0
    n = per_w // chunk
    mesh = plsc.VectorSubcoreMesh(core_axis_name="c", subcore_axis_name="s")

    @functools.partial(pl.kernel, mesh=mesh, out_type=jax.ShapeDtypeStruct((rows, w), h.dtype),
                       scratch_types=[pltpu.VMEM((nbuf, TOP_K, chunk), jnp.int32),
                                      pltpu.VMEM((nbuf, chunk, w), h.dtype),
                                      pltpu.SemaphoreType.DMA((nbuf,))])
    def scatter_rows(h_hbm, pos_hbm, x_hbm, idx_v, rows_v, sem):
        w0 = (lax.axis_index("s") * SC_CORES + lax.axis_index("c")) * per_w

        def scatter(slot, kk):
            return pltpu.make_async_copy(rows_v.at[slot], x_hbm.at[idx_v.at[slot, kk]], sem.at[slot])

        @pl.loop(0, n // nbuf)
        def _(g):
            for s in range(nbuf):
                @pl.when(g > 0)
                def _():
                    for kk in range(TOP_K):
                        scatter(s, kk).wait()
                base = pl.multiple_of(w0 + (g * nbuf + s) * chunk, SUBLANES)
                pltpu.sync_copy(h_hbm.at[pl.ds(base, chunk)], rows_v.at[s])
                for kk in range(TOP_K):
                    pltpu.sync_copy(pos_hbm.at[pl.ds(pl.multiple_of(kk * t + base, SUBLANES), chunk)],
                                    idx_v.at[s, kk])
                for kk in range(TOP_K):
                    scatter(s, kk).start()

        for s in range(nbuf):
            for kk in range(TOP_K):
                scatter(s, kk).wait()

    return scatter_rows(h, pos_flat)


def _sc_gather(table, idx_flat):
    n_idx = idx_flat.shape[0]
    w = table.shape[1]
    chunk, nbuf = 48, 4
    per_w = n_idx // SC_WORKERS
    assert n_idx % SC_WORKERS == 0 and per_w % (chunk * nbuf) == 0
    n = per_w // chunk
    mesh = plsc.VectorSubcoreMesh(core_axis_name="c", subcore_axis_name="s")

    @functools.partial(pl.kernel, mesh=mesh, out_type=jax.ShapeDtypeStruct((n_idx, w), table.dtype),
                       scratch_types=[pltpu.VMEM((nbuf, chunk), jnp.int32),
                                      pltpu.VMEM((nbuf, chunk, w), table.dtype),
                                      pltpu.SemaphoreType.DMA((nbuf,))])
    def gather_rows(t_hbm, idx_hbm, o_hbm, idx_v, rows_v, sem):
        w0 = (lax.axis_index("s") * SC_CORES + lax.axis_index("c")) * per_w

        def gather(slot):
            return pltpu.make_async_copy(t_hbm.at[idx_v.at[slot]], rows_v.at[slot], sem.at[slot])

        def issue(j, slot):
            base = pl.multiple_of(w0 + j * chunk, SUBLANES)
            pltpu.sync_copy(idx_hbm.at[pl.ds(base, chunk)], idx_v.at[slot])
            gather(slot).start()

        for s in range(nbuf):
            issue(s, s)

        @pl.loop(0, n // nbuf)
        def _(g):
            for s in range(nbuf):
                j = g * nbuf + s
                gather(s).wait()
                pltpu.sync_copy(rows_v.at[s], o_hbm.at[pl.ds(pl.multiple_of(w0 + j * chunk, SUBLANES), chunk)])

                @pl.when(j + nbuf < n)
                def _():
                    issue(j + nbuf, s)

    return gather_rows(table, idx_flat)


def _routed_experts(h, idx_t, wg, wu, wd):
    kk, t = idx_t.shape
    dp = h.shape[1]
    n_exp, d, de = wg.shape
    blk = EXPERT_ROWS
    n_assign = t * kk
    tm = 512
    assert t % tm == 0
    rank_t, cnt = _expert_ranks(idx_t, n_exp, tm)
    counts = cnt[:, 0].astype(jnp.int32)
    padded = (counts + blk - 1) // blk * blk
    pad_end = jnp.cumsum(padded)
    pad_start = pad_end - padded
    pos_flat = _positions(idx_t, rank_t, pad_start, tm).reshape(-1)
    n_blocks = (n_assign + n_exp * (blk - 1)) // blk
    rows = n_blocks * blk
    first_blk = (pad_start // blk).astype(jnp.int32)
    count_blk = (padded // blk).astype(jnp.int32)
    n_used = (pad_end[-1] // blk).astype(jnp.int32).reshape(1)
    x_buf = _sc_dispatch(h, pos_flat, rows)
    w_in_spec = pl.BlockSpec((None, d, de), lambda e, *_: (e, 0, 0))
    grid_spec = pltpu.PrefetchScalarGridSpec(
        num_scalar_prefetch=3,
        grid=(n_exp,),
        in_specs=[pl.BlockSpec(memory_space=pl.ANY), w_in_spec, w_in_spec,
                  pl.BlockSpec((None, de, d), lambda e, *_: (e, 0, 0))],
        out_specs=pl.BlockSpec(memory_space=pl.ANY),
        scratch_shapes=[pltpu.VMEM((d, de), BF16), pltpu.VMEM((d, de), BF16), pltpu.VMEM((de, d), BF16),
                        pltpu.VMEM((EXPERT_BUFFERS, blk, dp), jnp.uint32),
                        pltpu.VMEM((EXPERT_BUFFERS, blk, dp), jnp.uint32),
                        pltpu.SemaphoreType.DMA((EXPERT_BUFFERS,)), pltpu.SemaphoreType.DMA((EXPERT_BUFFERS,))],
    )
    out = pl.pallas_call(
        _experts_kernel,
        grid_spec=grid_spec,
        out_shape=jax.ShapeDtypeStruct((rows, dp), jnp.uint32),
        compiler_params=_params("arbitrary"),
        name="experts",
    )(first_blk, count_blk, n_used, x_buf, wg, wu, wd)
    return _sc_gather(out, pos_flat).reshape(kk, t, dp)


def _ffn_post_kernel(h_ref, r_ref, wt_ref, x_ref, gt_ref, wg_ref, wu_ref, wd_ref, g_ref, o_ref):
    h = _unpack_pairs(h_ref[...]).astype(BF16)
    act = _silu(jnp.dot(h, wg_ref[...], preferred_element_type=F32)) * jnp.dot(
        h, wu_ref[...], preferred_element_type=F32)
    f = jnp.dot(act.astype(BF16), wd_ref[...], preferred_element_type=F32)
    for k in range(TOP_K):
        f = f + _unpack_pairs(r_ref[k]) * wt_ref[:, k:k + 1]
    o_ref[...] = x_ref[...] + gt_ref[...] * _rms_rows(f, g_ref[...])


def _ffn_post(h2, routed, w_rows, row0, x1, mod, wg, wu, wd, g_post, tm):
    t, d = x1.shape
    assert row0 % tm == 0
    blk0 = row0 // tm
    full = lambda a: pl.BlockSpec(a.shape, lambda i: (0, 0))
    rows = lambda a: pl.BlockSpec((tm, a.shape[1]), lambda i: (i, 0))
    g2 = g_post.reshape(1, d)
    return pl.pallas_call(
        _ffn_post_kernel,
        grid=(t // tm,),
        in_specs=[rows(h2),
                  pl.BlockSpec((TOP_K, tm, d // 2), lambda i: (0, i + blk0, 0)),
                  pl.BlockSpec((tm, TOP_K), lambda i: (i + blk0, 0)),
                  rows(x1), mod.spec(5), full(wg), full(wu), full(wd), full(g2)],
        out_specs=pl.BlockSpec((tm, d), lambda i: (i, 0)),
        out_shape=jax.ShapeDtypeStruct((t, d), F32),
        compiler_params=_params("parallel"),
        name="ffn_post",
    )(h2, routed, w_rows, x1, mod.arr, wg, wu, wd, g2)


def _channel_mixer(x1s, mods, tms, lw):
    pre = [_ffn_pre(x1, lw['g_pre_ffn'], mod, lw['w_router_t'], lw['b_router'], tm)
           for x1, mod, tm in zip(x1s, mods, tms)]
    h2 = jnp.concatenate([p[0] for p in pre], axis=0)
    idx_t = jnp.concatenate([p[1] for p in pre], axis=1)
    w_rows = jnp.concatenate([p[2] for p in pre], axis=1).T
    routed = _routed_experts(h2, idx_t, lw['w_exp_gate'], lw['w_exp_up'], lw['w_exp_down'])
    outs, row0 = [], 0
    for x1, mod, tm, p in zip(x1s, mods, tms, pre):
        outs.append(_ffn_post(p[0], routed, w_rows, row0, x1, mod, lw['w_sh_gate'], lw['w_sh_up'],
                              lw['w_sh_down'], lw['g_post_ffn'], tm))
        row0 += x1.shape[0]
    return outs


def _token_mixer(x, mod_rows, attend, conv_buf, h0, lw, tm, ssm_dtype):
    b, seq, d = x.shape
    t = b * seq
    x2 = x.reshape(t, d)
    mod = _Mod(mod_rows, seq, tm, d)
    h = _prenorm(x2, lw['g_pre_mix'], mod, 1, 0, tm)
    proj = {}
    for call, names in enumerate(IN_PROJ_CALLS):
        outs = _matmuls(h, [lw['w_in'][n] for n in names], tm, "proj_%d" % call)
        proj.update(zip(names, outs))
    q = proj['q'].reshape(b, seq, -1)
    k = proj['k'].reshape(b, seq, -1)
    v = proj['v'].reshape(b, seq, -1)
    o_att, new_k, new_v = attend(q, k, v)
    xbc = proj['xbc'].reshape(b, seq, -1)
    y_ssm, h_last = _ssd_branch(xbc, proj['z'].reshape(b, seq, -1), proj['dt'].reshape(b, seq, -1),
                                conv_buf, h0, lw['conv_w'], lw['conv_b'], lw['dt_bias'], lw['a_log'],
                                lw['d_skip'], lw['g_ssm_norm'], ssm_dtype)
    new_conv = jnp.concatenate([conv_buf, xbc], axis=1)[:, -(D_CONV - 1):] if seq < D_CONV - 1 \
        else xbc[:, seq - (D_CONV - 1):]
    x1 = _mix(o_att.reshape(t, -1), y_ssm.reshape(t, -1), proj['gate'], x2, mod,
              lw['w_branch_a'], lw['w_branch_b'], lw['w_out'], lw['g_post_mix'], tm)
    return x1, mod, new_k, new_v, new_conv, h_last


def _split_in_proj(w_in, d, d_inner, conv_dim, heads):
    sizes = (N_DIL * ATT_WIDTH, ATT_WIDTH, ATT_WIDTH, d_inner, conv_dim, heads, d, d)
    offs = [0]
    for s in sizes:
        offs.append(offs[-1] + s)
    part = lambda i, j=None: w_in[:, offs[i]:offs[(i if j is None else j) + 1]].astype(BF16)
    w_dt = jnp.pad(part(5), ((0, 0), (0, LANES - heads)))
    return {'q': part(0), 'k': part(1), 'v': part(2), 'z': part(3), 'xbc': part(4), 'dt': w_dt, 'gate': part(6, 7)}


def kernel(x_prompt, x_sample, cache_win_k, cache_win_v, state_conv, state_ssm, c_prompt, c_sample, rel_bias, w_mod, b_mod, g_pre_mix, g_post_mix, g_pre_ffn, g_post_ffn, w_in, conv_w, conv_b, dt_bias, a_log, d_skip, g_ssm_norm, w_branch_a, w_branch_b, w_out, w_router, b_router, w_exp_gate, w_exp_up, w_exp_down, w_sh_gate, w_sh_up, w_sh_down):
    depth = w_mod.shape[0]
    bp, sp, d = x_prompt.shape
    bs, ss, _ = x_sample.shape
    d_inner = g_ssm_norm.shape[1]
    conv_dim = conv_w.shape[2]
    heads = dt_bias.shape[1]
    y_p, y_s = x_prompt, x_sample
    outs = [[] for _ in range(8)]
    for l in range(depth):
        lw = {
            'g_pre_mix': g_pre_mix[l], 'g_post_mix': g_post_mix[l],
            'g_pre_ffn': g_pre_ffn[l], 'g_post_ffn': g_post_ffn[l],
            'w_in': _split_in_proj(w_in[l], d, d_inner, conv_dim, heads),
            'conv_w': conv_w[l], 'conv_b': conv_b[l],
            'dt_bias': dt_bias[l], 'a_log': a_log[l], 'd_skip': d_skip[l], 'g_ssm_norm': g_ssm_norm[l],
            'w_branch_a': w_branch_a[l].astype(BF16), 'w_branch_b': w_branch_b[l].astype(BF16),
            'w_out': w_out[l].astype(BF16),
            'w_router_t': w_router[l].T.astype(BF16), 'b_router': b_router[l],
            'w_exp_gate': w_exp_gate[l], 'w_exp_up': w_exp_up[l], 'w_exp_down': w_exp_down[l],
            'w_sh_gate': w_sh_gate[l].astype(BF16), 'w_sh_up': w_sh_up[l].astype(BF16),
            'w_sh_down': w_sh_down[l].astype(BF16),
        }
        mod = _modulation(jnp.concatenate([c_prompt, c_sample], axis=0), w_mod[l], b_mod[l])
        conv0 = jnp.zeros((bp, D_CONV - 1, conv_dim), F32)
        h0 = jnp.zeros((bp, heads, SSM_HEAD_DIM, D_STATE), F32)
        tms = (512, 256)
        x1_p, mod_p, *state_p = _token_mixer(
            y_p, mod[:bp], functools.partial(_prompt_attend, rel_bias=rel_bias), conv0, h0, lw, tms[0], BF16)
        x1_s, mod_s, *state_s = _token_mixer(
            y_s, mod[bp:],
            functools.partial(_sample_attend, buf_k=cache_win_k[l], buf_v=cache_win_v[l], rel_bias=rel_bias),
            state_conv[l], state_ssm[l], lw, tms[1], F32)
        for o, val in zip(outs, state_p + state_s):
            o.append(val)
        y_p, y_s = _channel_mixer((x1_p, x1_s), (mod_p, mod_s), tms, lw)
        y_p = y_p.reshape(bp, sp, d)
        y_s = y_s.reshape(bs, ss, d)
    return (y_p, y_s) + tuple(jnp.stack(o) for o in outs)


def _prompt_attend(q, k, v, rel_bias):
    b, seq, _ = k.shape
    o = _attn_prompt(q, k, v, rel_bias)
    n_keep = min(MAX_WINDOW, seq)
    shape = (b, n_keep, ATT_HEADS, HEAD_DIM)
    return o, k[:, seq - n_keep:].reshape(shape), v[:, seq - n_keep:].reshape(shape)


def _sample_attend(q, k, v, buf_k, buf_v, rel_bias):
    return _attn_sample(q, k, v, buf_k, buf_v, rel_bias)
```

```python
import functools
import math

import jax
import jax.numpy as jnp
from jax import lax
from jax.experimental import pallas as pl
from jax.experimental.pallas import tpu as pltpu
from jax.experimental.pallas import tpu_sc as plsc

F32 = jnp.float32
BF16 = jnp.bfloat16

DIL_PATTERNS = ((128, 1), (512, 4), (2048, 16))
N_DIL = len(DIL_PATTERNS)
ATT_HEADS = 8
HEAD_DIM = 128
ATT_WIDTH = ATT_HEADS * HEAD_DIM
ATT_SCALE = HEAD_DIM ** -0.5
MAX_WINDOW = 2048
NUM_BUCKETS = 32
MAX_DISTANCE = 2048
SSM_HEAD_DIM = 64
SSM_GROUPS = 4
D_STATE = 128
D_CONV = 4
SSD_CHUNK = 128
TOP_K = 8
N_EXPERT_GROUPS = 8
TOPK_GROUPS = 4
ROUTE_SCALE = 2.5
RMS_EPS = 1e-6
LOG2_E = math.log2(math.e)

LANES = 128
SUBLANES = 8
VMEM_LIMIT = 56 * 1024 * 1024
EXPERT_ROWS = 512
EXPERT_BUFFERS = 4
IN_PROJ_CALLS = (('q',), ('k', 'v', 'z'), ('xbc', 'dt', 'gate'))
SC_CORES = 2
SC_SUBCORES = 16
SC_WORKERS = SC_CORES * SC_SUBCORES
NT_DIMS = (((1,), (1,)), ((), ()))


def _params(*sem):
    return pltpu.CompilerParams(dimension_semantics=sem, vmem_limit_bytes=VMEM_LIMIT)


def _sigmoid(x):
    return 0.5 * jnp.tanh(0.5 * x) + 0.5


def _silu(x):
    return x * _sigmoid(x)


def _rms_rows(x, g):
    return x * lax.rsqrt(jnp.mean(x * x, axis=-1, keepdims=True) + RMS_EPS) * g


def _pack_pairs(x):
    n = x.shape[1] // 2
    bits = pltpu.bitcast(x.astype(BF16).astype(F32), jnp.uint32)
    return bits[:, :n] | (bits[:, n:] >> jnp.uint32(16))


def _unpack_pairs(p):
    hi = pltpu.bitcast(p & jnp.uint32(0xFFFF0000), F32)
    lo = pltpu.bitcast(p << jnp.uint32(16), F32)
    return jnp.concatenate([hi, lo], axis=1)


def _mod_kernel(c_ref, w_ref, b_ref, o_ref):
    s = _silu(c_ref[...]).astype(BF16)
    o_ref[...] = jnp.dot(s, w_ref[...].astype(BF16), preferred_element_type=F32) + b_ref[...]


def _modulation(c, w_mod, b_mod):
    m, d = c.shape
    n = w_mod.shape[1]
    tn = n // 4
    return pl.pallas_call(
        _mod_kernel,
        grid=(n // tn,),
        in_specs=[pl.BlockSpec((m, d), lambda j: (0, 0)),
                  pl.BlockSpec((d, tn), lambda j: (0, j)),
                  pl.BlockSpec((1, tn), lambda j: (0, j))],
        out_specs=pl.BlockSpec((m, tn), lambda j: (0, j)),
        out_shape=jax.ShapeDtypeStruct((m, n), F32),
        compiler_params=_params("arbitrary"),
        name="modulation",
    )(c, w_mod, b_mod.reshape(1, n))


class _Mod:
    def __init__(self, mod, seq, tm, d):
        self.d = d
        b = mod.shape[0]
        if seq % tm == 0:
            per = seq // tm
            self.arr = mod.reshape(b, 1, mod.shape[1])
            self._spec = lambda col: pl.BlockSpec((None, 1, d), lambda i: (i // per, 0, col))
        else:
            self.arr = jnp.repeat(mod, seq, axis=0)
            self._spec = lambda col: pl.BlockSpec((tm, d), lambda i: (i, col))

    def spec(self, col):
        return self._spec(col)


def _prenorm_kernel(x_ref, g_ref, sc_ref, sh_ref, o_ref):
    y = _rms_rows(x_ref[...], g_ref[...])
    o_ref[...] = (y * (1 + sc_ref[...]) + sh_ref[...]).astype(o_ref.dtype)


def _prenorm(x2, g, mod, col_scale, col_shift, tm):
    t, d = x2.shape
    return pl.pallas_call(
        _prenorm_kernel,
        grid=(t // tm,),
        in_specs=[pl.BlockSpec((tm, d), lambda i: (i, 0)),
                  pl.BlockSpec((1, d), lambda i: (0, 0)),
                  mod.spec(col_scale), mod.spec(col_shift)],
        out_specs=pl.BlockSpec((tm, d), lambda i: (i, 0)),
        out_shape=jax.ShapeDtypeStruct((t, d), BF16),
        compiler_params=_params("parallel"),
        name="prenorm",
    )(x2, g.reshape(1, d), mod.arr, mod.arr)


def _mm_kernel(h_ref, *refs):
    n = len(refs) // 2
    h = h_ref[...]
    for w_ref, o_ref in zip(refs[:n], refs[n:]):
        o_ref[...] = jnp.dot(h, w_ref[...], preferred_element_type=F32).astype(o_ref.dtype)


def _matmuls(h, weights, tm, name):
    t, k = h.shape
    return pl.pallas_call(
        _mm_kernel,
        grid=(t // tm,),
        in_specs=[pl.BlockSpec((tm, k), lambda i: (i, 0))]
        + [pl.BlockSpec(w.shape, lambda i: (0, 0)) for w in weights],
        out_specs=[pl.BlockSpec((tm, w.shape[1]), lambda i: (i, 0)) for w in weights],
        out_shape=[jax.ShapeDtypeStruct((t, w.shape[1]), F32) for w in weights],
        compiler_params=_params("parallel"),
        name=name,
    )(h, *weights)


def _bucket(dist):
    max_exact = NUM_BUCKETS // 2
    far = max_exact + (jnp.log(jnp.maximum(dist, 1).astype(F32) / max_exact)
                       / math.log(MAX_DISTANCE / max_exact) * (NUM_BUCKETS - max_exact)).astype(jnp.int32)
    return jnp.where(dist < max_exact, dist, jnp.minimum(far, NUM_BUCKETS - 1))


def _bias_by_distance(rel_bias, n):
    hit = _bucket(jnp.arange(n, dtype=jnp.int32))[:, None] == jnp.arange(NUM_BUCKETS)[None, :]
    return jnp.sum(jnp.where(hit[:, :, None], rel_bias.astype(F32)[None], 0.0), axis=1)


def _band_bias(rel_bias, w):
    by_dist = _bias_by_distance(rel_bias, max(win for win, _ in DIL_PATTERNS) + 1)
    ext = 3 * w
    out = []
    for g, (_, dil) in enumerate(DIL_PATTERNS):
        vec = by_dist[::dil][:w + 1, g * ATT_HEADS:(g + 1) * ATT_HEADS].T
        v = jnp.concatenate([vec[:, ::-1], jnp.full((ATT_HEADS, ext - w - 1), -jnp.inf, F32)], axis=1)
        skew = jnp.tile(v, (1, w))[:, :w * (ext - 1)].reshape(ATT_HEADS, w, ext - 1)
        out.append(skew[:, :, :2 * w])
    return jnp.concatenate(out, axis=0)


def _attn_prompt_kernel(q0_ref, q1_ref, q2_ref, k_ref, v_ref, b0_ref, b1_ref, b2_ref, o_ref,
                        og_scr, lse_scr, *, seq, w):
    q_refs = (q0_ref, q1_ref, q2_ref)
    b_refs = (b0_ref, b1_ref, b2_ref)

    def rows(start, size, dil):
        return pl.ds(start, size) if dil == 1 else pl.ds(start, size, stride=dil)

    for g, (_, dil) in enumerate(DIL_PATTERNS):
        nb = seq // dil // w
        for r in range(dil):
            for n in range(nb):
                q_rows = rows(r + dil * n * w, w, dil)
                qb = q_refs[g][q_rows, :].astype(BF16)
                if n == 0:
                    k_rows = rows(r, w, dil)
                    bias = b_refs[g][:, w:]
                else:
                    k_rows = rows(r + dil * (n - 1) * w, 2 * w, dil)
                    bias = b_refs[g][...]
                kb = k_ref[k_rows, :].astype(BF16)
                vb = v_ref[k_rows, :].astype(BF16)
                s = lax.dot_general(qb, kb, NT_DIMS, preferred_element_type=F32) * (ATT_SCALE * LOG2_E) + bias
                m = jnp.max(s, axis=-1, keepdims=True)
                p = jnp.exp2(s - m)
                den = jnp.sum(p, axis=-1, keepdims=True)
                o = jnp.dot(p.astype(BF16), vb, preferred_element_type=F32) / den
                og_scr[g, q_rows, :] = o
                lse_scr[g, q_rows, :] = jnp.broadcast_to(m + jnp.log2(den), (w, LANES))

    step = 256
    for c in range(seq // step):
        sl = pl.ds(c * step, step)
        l0, l1, l2 = lse_scr[0, sl, :], lse_scr[1, sl, :], lse_scr[2, sl, :]
        mm = jnp.maximum(jnp.maximum(l0, l1), l2)
        e0, e1, e2 = jnp.exp2(l0 - mm), jnp.exp2(l1 - mm), jnp.exp2(l2 - mm)
        tot = e0 + e1 + e2
        o = (e0 / tot) * og_scr[0, sl, :] + (e1 / tot) * og_scr[1, sl, :] + (e2 / tot) * og_scr[2, sl, :]
        o_ref[sl, :] = o.astype(o_ref.dtype)


def _attn_prompt(q, k, v, rel_bias):
    b, seq, _ = k.shape
    w = DIL_PATTERNS[0][0] // DIL_PATTERNS[0][1]
    for win, dil in DIL_PATTERNS:
        assert win // dil == w and seq % (dil * w) == 0
    bias = _band_bias(rel_bias, w) * LOG2_E
    hd = HEAD_DIM

    def q_spec(g):
        return pl.BlockSpec((None, seq, hd), lambda i, h: (i, 0, g * ATT_HEADS + h))

    def b_spec(g):
        return pl.BlockSpec((None, w, 2 * w), lambda i, h: (g * ATT_HEADS + h, 0, 0))

    kv_spec = pl.BlockSpec((None, seq, hd), lambda i, h: (i, 0, h))
    return pl.pallas_call(
        functools.partial(_attn_prompt_kernel, seq=seq, w=w),
        grid=(b, ATT_HEADS),
        in_specs=[q_spec(0), q_spec(1), q_spec(2), kv_spec, kv_spec, b_spec(0), b_spec(1), b_spec(2)],
        out_specs=pl.BlockSpec((None, seq, hd), lambda i, h: (i, 0, h)),
        out_shape=jax.ShapeDtypeStruct((b, seq, ATT_WIDTH), BF16),
        scratch_shapes=[pltpu.VMEM((N_DIL, seq, hd), F32), pltpu.VMEM((N_DIL, seq, LANES), F32)],
        compiler_params=_params("parallel", "parallel"),
        name="attn_prompt",
    )(q, q, q, k, v, bias, bias, bias)


SAMPLE_KEY_CHUNK = 1024


def _attn_sample_kernel(q_ref, kn_ref, vn_ref, kn4_ref, vn4_ref, ck_ref, cv_ref, bm_ref, bn_ref,
                        o_ref, wk_ref, wv_ref, m_scr, l_scr, acc_scr, ck_scr, cv_scr, *, t, nch, n_full):
    c = pl.program_id(1)
    ch = ck_ref.shape[0]
    tr = t * ATT_HEADS
    rows = N_DIL * t

    @pl.when(c == 0)
    def _():
        m_scr[...] = jnp.full(m_scr.shape, -jnp.inf, F32)
        l_scr[...] = jnp.zeros(l_scr.shape, F32)
        acc_scr[...] = jnp.zeros(acc_scr.shape, F32)
        ck_scr[...] = kn4_ref[...]
        cv_scr[...] = vn4_ref[...]

    wk_ref[0:ch - tr, :] = ck_ref[tr:ch, :]
    wk_ref[ch - tr:ch, :] = ck_scr[...]
    ck_scr[...] = ck_ref[0:tr, :]
    wv_ref[0:ch - tr, :] = cv_ref[tr:ch, :]
    wv_ref[ch - tr:ch, :] = cv_scr[...]
    cv_scr[...] = cv_ref[0:tr, :]

    def head_q(h):
        return jnp.concatenate(
            [q_ref[:, g * ATT_WIDTH + h * HEAD_DIM:g * ATT_WIDTH + (h + 1) * HEAD_DIM] for g in range(N_DIL)], axis=0)

    def attend(h, r0):
        rs = slice(r0, rows)
        qa = head_q(h)[r0:].astype(BF16)
        head_rows = pl.ds(h, ch // ATT_HEADS, stride=ATT_HEADS)
        kh = ck_ref[head_rows, :].astype(BF16)
        vh = cv_ref[head_rows, :].astype(BF16)
        keys = ch // ATT_HEADS
        bias = bm_ref[h, rs, pl.ds(pl.multiple_of((nch - 1 - c) * keys, LANES), keys)]
        s = lax.dot_general(qa, kh, NT_DIMS, preferred_element_type=F32) * ATT_SCALE + bias
        m_old = m_scr[h, rs, :]
        m_new = jnp.maximum(m_old, jnp.max(s, axis=-1, keepdims=True))
        alpha = jnp.exp(m_old - m_new)
        p = jnp.exp(s - m_new[:, 0:1])
        l_scr[h, rs, :] = alpha * l_scr[h, rs, :] + jnp.sum(p, axis=-1, keepdims=True)
        acc_scr[h, rs, :] = alpha * acc_scr[h, rs, :] + jnp.dot(p.astype(BF16), vh, preferred_element_type=F32)
        m_scr[h, rs, :] = m_new

    @pl.when(c < n_full)
    def _():
        for h in range(ATT_HEADS):
            attend(h, 0)

    @pl.when(c >= n_full)
    def _():
        for h in range(ATT_HEADS):
            attend(h, rows - t)

    @pl.when(c == nch - 1)
    def _():
        for h in range(ATT_HEADS):
            cs = slice(h * HEAD_DIM, (h + 1) * HEAD_DIM)
            qa = head_q(h)
            s_new = [jnp.sum(qa * kn_ref[j:j + 1, cs], axis=-1, keepdims=True) * ATT_SCALE + bn_ref[h, j]
                     for j in range(t)]
            m_old = m_scr[h]
            m_new = m_old
            for sj in s_new:
                m_new = jnp.maximum(m_new, sj)
            alpha = jnp.exp(m_old - m_new)
            den = alpha * l_scr[h]
            acc = alpha * acc_scr[h]
            for j, sj in enumerate(s_new):
                pj = jnp.exp(sj - m_new)
                den = den + pj
                acc = acc + pj * vn_ref[j:j + 1, cs]
            o = acc / den
            lse = m_new + jnp.log(den)
            l0, l1, l2 = lse[0:t], lse[t:2 * t], lse[2 * t:rows]
            mm = jnp.maximum(jnp.maximum(l0, l1), l2)
            e0, e1, e2 = jnp.exp(l0 - mm), jnp.exp(l1 - mm), jnp.exp(l2 - mm)
            tot = e0 + e1 + e2
            o_ref[:, cs] = (e0 / tot) * o[0:t] + (e1 / tot) * o[t:2 * t] + (e2 / tot) * o[2 * t:rows]


def _sample_bias(rel_bias, t, n_buf):
    n_key = n_buf + t
    by_dist = _bias_by_distance(rel_bias, n_key)
    dist = jnp.arange(n_key)
    full = []
    for g, (win, dil) in enumerate(DIL_PATTERNS):
        ok = (dist % dil == 0) & (dist <= win)
        vec = jnp.where(ok[:, None], by_dist[:, g * ATT_HEADS:(g + 1) * ATT_HEADS], -jnp.inf)
        rev = jnp.concatenate([vec[::-1], jnp.full((t - 1, ATT_HEADS), -jnp.inf, F32)], axis=0)
        full.append(jnp.stack([rev[t - 1 - tok:t - 1 - tok + n_key] for tok in range(t)], axis=0))
    full = jnp.transpose(jnp.concatenate(full, axis=0), (2, 0, 1))
    new = jnp.transpose(full[:, :, n_buf:], (0, 2, 1))
    new = jnp.broadcast_to(new[..., None], new.shape + (LANES,))
    return full[:, :, :n_buf], new


def _attn_sample(q, k, v, cache_k, cache_v, rel_bias):
    b, t, _ = k.shape
    n_buf = cache_k.shape[1]
    ch = SAMPLE_KEY_CHUNK
    assert t == SUBLANES and n_buf % ch == 0
    assert all(win <= n_buf for win, _ in DIL_PATTERNS)
    nch = n_buf // ch
    assert all(win <= DIL_PATTERNS[-1][0] for win, _ in DIL_PATTERNS)
    n_full = min(nch, max(-(-win // ch) for win, _ in DIL_PATTERNS[:-1]))
    bm, bn = _sample_bias(rel_bias, t, n_buf)
    flat = lambda a: a.reshape(b, -1, HEAD_DIM)
    rows = N_DIL * t
    tr = t * ATT_HEADS

    def per_b(*shape):
        return pl.BlockSpec((None,) + shape, lambda i, c: (i,) + (0,) * len(shape))

    buf_spec = pl.BlockSpec((None, ch * ATT_HEADS, HEAD_DIM), lambda i, c: (i, nch - 1 - c, 0))
    buf_shape = jax.ShapeDtypeStruct((b, n_buf * ATT_HEADS, HEAD_DIM), F32)
    o, win_k, win_v = pl.pallas_call(
        functools.partial(_attn_sample_kernel, t=t, nch=nch, n_full=n_full),
        grid=(b, nch),
        in_specs=[per_b(t, N_DIL * ATT_WIDTH), per_b(t, ATT_WIDTH), per_b(t, ATT_WIDTH),
                  per_b(tr, HEAD_DIM), per_b(tr, HEAD_DIM), buf_spec, buf_spec,
                  pl.BlockSpec((ATT_HEADS, rows, n_buf), lambda i, c: (0, 0, 0)),
                  pl.BlockSpec((ATT_HEADS, t, rows, LANES), lambda i, c: (0, 0, 0, 0))],
        out_specs=[per_b(t, ATT_WIDTH), buf_spec, buf_spec],
        out_shape=[jax.ShapeDtypeStruct((b, t, ATT_WIDTH), F32), buf_shape, buf_shape],
        scratch_shapes=[pltpu.VMEM((ATT_HEADS, rows, LANES), F32), pltpu.VMEM((ATT_HEADS, rows, LANES), F32),
                        pltpu.VMEM((ATT_HEADS, rows, HEAD_DIM), F32),
                        pltpu.VMEM((tr, HEAD_DIM), F32), pltpu.VMEM((tr, HEAD_DIM), F32)],
        compiler_params=_params("parallel", "arbitrary"),
        name="attn_sample",
    )(q, k, v, flat(k), flat(v), flat(cache_k), flat(cache_v), bm, bn)
    return o, win_k.reshape(cache_k.shape), win_v.reshape(cache_v.shape)


def _softplus(x):
    return jnp.maximum(x, 0.0) + jnp.log(1.0 + jnp.exp(-jnp.abs(x)))


def _ssd_kernel(xbc_ref, z_ref, dt_ref, cbuf_ref, h0_ref, cw_ref, cb_ref, dtb_ref, alog_ref, dsk_ref, gn_ref,
                y_ref, hl_ref, ext_scr, st_scr, y_scr, acst_scr, dtt_scr, wstt_scr, *, lv, nc, d_inner):
    q = SSD_CHUNK
    n = D_STATE
    c = pl.program_id(1)
    n_pairs = d_inner // LANES
    pairs_per_group = n_pairs // SSM_GROUPS

    @pl.when(c == 0)
    def _():
        ext_scr[0:SUBLANES, :] = cbuf_ref[...]
        for i in range(n_pairs):
            st_scr[:, i * LANES:(i + 1) * LANES] = h0_ref[i * LANES:(i + 1) * LANES, :].T

    ext_scr[SUBLANES:SUBLANES + lv, :] = xbc_ref[...]
    first = SUBLANES - (D_CONV - 1)
    u = cb_ref[...] + cw_ref[0:1, :] * ext_scr[first:first + lv, :]
    for j in range(1, D_CONV):
        u = u + cw_ref[j:j + 1, :] * ext_scr[first + j:first + j + lv, :]
    u = _silu(u)
    if lv < q:
        u = jnp.concatenate([u, jnp.zeros((q - lv, u.shape[1]), F32)], axis=0)
    if nc > 1:
        ext_scr[0:SUBLANES, :] = ext_scr[q:q + SUBLANES, :]

    dtv = _softplus(dt_ref_rows(dt_ref, lv, q) + dtb_ref[...])
    if lv < q:
        row = lax.broadcasted_iota(jnp.int32, (q, LANES), 0)
        dtv = jnp.where(row < lv, dtv, 0.0)
    a = -jnp.exp(alog_ref[...])
    da = dtv * a
    li = lax.broadcasted_iota(jnp.int32, (q, q), 0)
    si = lax.broadcasted_iota(jnp.int32, (q, q), 1)
    tri = li >= si
    acs = jnp.dot(tri.astype(F32), da, preferred_element_type=F32, precision=lax.Precision.HIGHEST) * LOG2_E
    acs_last = acs[q - 1:q, :]
    acst_scr[...] = acs.T
    dtt_scr[...] = dtv.T
    wstt_scr[...] = (jnp.exp2(acs_last - acs) * dtv).T

    lane = lax.broadcasted_iota(jnp.int32, (1, LANES), 1)
    left = lane < SSM_HEAD_DIM
    for g in range(SSM_GROUPS):
        bg = u[:, d_inner + g * n:d_inner + (g + 1) * n]
        cg = u[:, d_inner + (SSM_GROUPS + g) * n:d_inner + (SSM_GROUPS + g + 1) * n].astype(BF16)
        cb = lax.dot_general(cg, bg.astype(BF16), NT_DIMS, preferred_element_type=F32)
        bgt = bg.T
        for jp in range(pairs_per_group):
            pair = g * pairs_per_group + jp
            cols = slice(pair * LANES, (pair + 1) * LANES)
            x_pair = u[:, cols]
            lhs, e_col, dec = [], [], []
            for h in (2 * pair, 2 * pair + 1):
                a_col = jnp.broadcast_to(acs[:, h:h + 1], (q, q))
                seg = a_col - acst_scr[h:h + 1, :]
                m_h = cb * jnp.exp2(jnp.where(tri, seg, -jnp.inf)) * dtt_scr[h:h + 1, :]
                lhs.append(m_h.astype(BF16))
                e_col.append(jnp.exp2(a_col))
                dec.append(jnp.exp2(a_col[q - 1:q, :]))
            for h in (2 * pair, 2 * pair + 1):
                lhs.append((bgt * wstt_scr[h:h + 1, :]).astype(BF16))
            res = jnp.dot(jnp.concatenate(lhs, axis=0), x_pair.astype(BF16), preferred_element_type=F32)
            y_diag = jnp.where(left, res[0:q], res[q:2 * q])
            d_state = jnp.where(left, res[2 * q:2 * q + n], res[2 * q + n:2 * q + 2 * n])
            st = st_scr[:, cols]
            y_off = jnp.dot(cg, st.astype(BF16), preferred_element_type=F32) * jnp.where(left, e_col[0], e_col[1])
            st_scr[:, cols] = st * jnp.where(left, dec[0], dec[1]) + d_state
            y_scr[:, cols] = y_diag + y_off + dsk_ref[:, cols] * x_pair

    gw = d_inner // SSM_GROUPS
    for g in range(SSM_GROUPS):
        cols = slice(g * gw, (g + 1) * gw)
        yg = y_scr[0:lv, cols] * _silu(z_ref[:, cols])
        yg = yg * lax.rsqrt(jnp.mean(yg * yg, axis=-1, keepdims=True) + RMS_EPS)
        y_ref[:, cols] = (yg * gn_ref[:, cols]).astype(y_ref.dtype)

    @pl.when(c == nc - 1)
    def _():
        for i in range(n_pairs):
            hl_ref[i * LANES:(i + 1) * LANES, :] = st_scr[:, i * LANES:(i + 1) * LANES].T


def dt_ref_rows(dt_ref, lv, q):
    if lv == q:
        return dt_ref[...]
    return jnp.concatenate([dt_ref[...], jnp.zeros((q - lv, dt_ref.shape[1]), F32)], axis=0)


def _ssd_branch(xbc, z, dt, conv_buf, h0, conv_w, conv_b, dt_bias, a_log, d_skip, g_norm, out_dtype):
    b, seqlen, conv_dim = xbc.shape
    d_inner = z.shape[2]
    heads = d_inner // SSM_HEAD_DIM
    q = SSD_CHUNK
    lv = min(q, seqlen)
    assert seqlen % lv == 0 and lv % SUBLANES == 0 and heads <= LANES
    nc = seqlen // lv
    cbuf = jnp.pad(conv_buf, ((0, 0), (SUBLANES - (D_CONV - 1), 0), (0, 0)))
    h0f = h0.reshape(b, heads * SSM_HEAD_DIM, D_STATE)
    pad = LANES - heads
    row = lambda a: jnp.pad(a.astype(F32), (0, pad)).reshape(1, LANES)
    dsk = jnp.repeat(d_skip.astype(F32), SSM_HEAD_DIM).reshape(1, d_inner)

    def per_chunk(width):
        return pl.BlockSpec((None, lv, width), lambda i, c: (i, c, 0))

    def per_batch(r, width):
        return pl.BlockSpec((None, r, width), lambda i, c: (i, 0, 0))

    def const(r, width):
        return pl.BlockSpec((r, width), lambda i, c: (0, 0))

    y, h_last = pl.pallas_call(
        functools.partial(_ssd_kernel, lv=lv, nc=nc, d_inner=d_inner),
        grid=(b, nc),
        in_specs=[per_chunk(conv_dim), per_chunk(d_inner), per_chunk(LANES),
                  per_batch(SUBLANES, conv_dim), per_batch(heads * SSM_HEAD_DIM, D_STATE),
                  const(D_CONV, conv_dim), const(1, conv_dim), const(1, LANES), const(1, LANES),
                  const(1, d_inner), const(1, d_inner)],
        out_specs=[per_chunk(d_inner), per_batch(heads * SSM_HEAD_DIM, D_STATE)],
        out_shape=[jax.ShapeDtypeStruct((b, seqlen, d_inner), out_dtype),
                   jax.ShapeDtypeStruct((b, heads * SSM_HEAD_DIM, D_STATE), F32)],
        scratch_shapes=[pltpu.VMEM((q + 2 * SUBLANES, conv_dim), F32),
                        pltpu.VMEM((D_STATE, d_inner), F32),
                        pltpu.VMEM((q, d_inner), F32),
                        pltpu.VMEM((LANES, q), F32), pltpu.VMEM((LANES, q), F32), pltpu.VMEM((LANES, q), F32)],
        compiler_params=_params("parallel", "arbitrary"),
        name="ssd",
    )(xbc, z, dt, cbuf, h0f, conv_w, conv_b.reshape(1, conv_dim), row(dt_bias), row(a_log), dsk,
      g_norm.reshape(1, d_inner))
    return y, h_last.reshape(b, heads, SSM_HEAD_DIM, D_STATE)


def _mix_kernel(oa_ref, ys_ref, gate_ref, x_ref, gt_ref, wa_ref, wb_ref, wo_ref, g_ref, o_ref, *, d):
    ya = jnp.dot(oa_ref[...].astype(BF16), wa_ref[...], preferred_element_type=F32)
    yb = jnp.dot(ys_ref[...].astype(BF16), wb_ref[...], preferred_element_type=F32)
    mixed = _sigmoid(gate_ref[:, 0:d]) * ya + _sigmoid(gate_ref[:, d:2 * d]) * yb
    mix = jnp.dot(mixed.astype(BF16), wo_ref[...], preferred_element_type=F32)
    o_ref[...] = x_ref[...] + gt_ref[...] * _rms_rows(mix, g_ref[...])


def _mix(o_att, y_ssm, gates, x2, mod, wa, wb, wo, g_post, tm):
    t, d = x2.shape
    full = lambda a: pl.BlockSpec(a.shape, lambda i: (0, 0))
    rows = lambda a: pl.BlockSpec((tm, a.shape[1]), lambda i: (i, 0))
    g2 = g_post.reshape(1, d)
    return pl.pallas_call(
        functools.partial(_mix_kernel, d=d),
        grid=(t // tm,),
        in_specs=[rows(o_att), rows(y_ssm), rows(gates), rows(x2), mod.spec(2), full(wa), full(wb), full(wo),
                  full(g2)],
        out_specs=pl.BlockSpec((tm, d), lambda i: (i, 0)),
        out_shape=jax.ShapeDtypeStruct((t, d), F32),
        compiler_params=_params("parallel"),
        name="mix",
    )(o_att, y_ssm, gates, x2, mod.arr, wa, wb, wo, g2)


def _first_index(hit, iota, size):
    return jnp.min(jnp.where(hit, iota, size), axis=0, keepdims=True)


def _ffn_pre_kernel(x_ref, g_ref, sc_ref, sh_ref, wr_ref, br_ref, h_ref, idx_ref, w_ref, *, n_exp):
    y = _rms_rows(x_ref[...], g_ref[...])
    hf = y * (1 + sc_ref[...]) + sh_ref[...]
    h_ref[...] = _pack_pairs(hf)
    h = hf.astype(BF16)
    tm = h.shape[0]
    scores = jax.nn.sigmoid(lax.dot_general(wr_ref[...], h, NT_DIMS, preferred_element_type=F32))
    biased = scores + br_ref[...]
    gsz = n_exp // N_EXPERT_GROUPS
    riota = lax.broadcasted_iota(jnp.int32, (gsz, tm), 0)
    gs = []
    for g in range(N_EXPERT_GROUPS):
        xg = biased[g * gsz:(g + 1) * gsz, :]
        m1 = jnp.max(xg, axis=0, keepdims=True)
        i1 = _first_index(xg == m1, riota, gsz)
        m2 = jnp.max(jnp.where(riota == i1, -jnp.inf, xg), axis=0, keepdims=True)
        gs.append(m1 + m2)
    gs = jnp.concatenate(gs, axis=0)
    giota = lax.broadcasted_iota(jnp.int32, gs.shape, 0)
    keep = jnp.zeros(gs.shape, F32)
    for _ in range(TOPK_GROUPS):
        gi = _first_index(gs == jnp.max(gs, axis=0, keepdims=True), giota, N_EXPERT_GROUPS)
        sel = giota == gi
        keep = jnp.where(sel, 1.0, keep)
        gs = jnp.where(sel, -jnp.inf, gs)
    masked = jnp.concatenate(
        [jnp.where(keep[g:g + 1, :] > 0.5, biased[g * gsz:(g + 1) * gsz, :], -jnp.inf)
         for g in range(N_EXPERT_GROUPS)], axis=0)
    eiota = lax.broadcasted_iota(jnp.int32, (n_exp, tm), 0)
    idxs, ws = [], []
    for _ in range(TOP_K):
        ik = _first_index(masked == jnp.max(masked, axis=0, keepdims=True), eiota, n_exp)
        hit = eiota == ik
        idxs.append(ik)
        ws.append(jnp.sum(jnp.where(hit, scores, 0.0), axis=0, keepdims=True))
        masked = jnp.where(hit, -jnp.inf, masked)
    idx_ref[...] = jnp.concatenate(idxs, axis=0)
    w = jnp.concatenate(ws, axis=0)
    w_ref[...] = w / jnp.sum(w, axis=0, keepdims=True) * ROUTE_SCALE


def _ffn_pre(x1, g, mod, w_router_t, b_router, tm):
    t, d = x1.shape
    e = w_router_t.shape[0]
    br = jnp.broadcast_to(b_router.astype(F32)[:, None], (e, tm))
    return pl.pallas_call(
        functools.partial(_ffn_pre_kernel, n_exp=e),
        grid=(t // tm,),
        in_specs=[pl.BlockSpec((tm, d), lambda i: (i, 0)),
                  pl.BlockSpec((1, d), lambda i: (0, 0)),
                  mod.spec(4), mod.spec(3),
                  pl.BlockSpec((e, d), lambda i: (0, 0)),
                  pl.BlockSpec((e, tm), lambda i: (0, 0))],
        out_specs=[pl.BlockSpec((tm, d // 2), lambda i: (i, 0)),
                   pl.BlockSpec((TOP_K, tm), lambda i: (0, i)),
                   pl.BlockSpec((TOP_K, tm), lambda i: (0, i))],
        out_shape=[jax.ShapeDtypeStruct((t, d // 2), jnp.uint32), jax.ShapeDtypeStruct((TOP_K, t), jnp.int32),
                   jax.ShapeDtypeStruct((TOP_K, t), F32)],
        compiler_params=_params("parallel"),
        name="ffn_pre",
    )(x1, g.reshape(1, d), mod.arr, mod.arr, w_router_t, br)


def _rank_kernel(idx_ref, rank_ref, cnt_ref, base_scr, *, n_exp):
    i = pl.program_id(0)
    tm = idx_ref.shape[1]

    @pl.when(i == 0)
    def _():
        base_scr[...] = jnp.zeros(base_scr.shape, F32)

    eiota = lax.broadcasted_iota(jnp.int32, (n_exp, tm), 0)
    idx = idx_ref[...]
    hits = [eiota == idx[k:k + 1, :] for k in range(TOP_K)]
    multi = jnp.zeros((n_exp, tm), F32)
    for hit in hits:
        multi = multi + jnp.where(hit, 1.0, 0.0)
    multi = multi.astype(BF16)
    ti = lax.broadcasted_iota(jnp.int32, (tm, tm), 0)
    tj = lax.broadcasted_iota(jnp.int32, (tm, tm), 1)
    earlier = jnp.where(ti < tj, 1.0, 0.0).astype(BF16)
    before = jnp.dot(multi, earlier, preferred_element_type=F32) + base_scr[:, 0:1]
    ranks = [jnp.sum(jnp.where(hit, before, 0.0), axis=0, keepdims=True) for hit in hits]
    rank_ref[...] = jnp.concatenate(ranks, axis=0).astype(jnp.int32)
    base_scr[...] = base_scr[...] + jnp.dot(multi, jnp.ones((tm, LANES), BF16), preferred_element_type=F32)
    cnt_ref[...] = base_scr[...]


def _expert_ranks(idx_t, n_exp, tm):
    t = idx_t.shape[1]
    return pl.pallas_call(
        functools.partial(_rank_kernel, n_exp=n_exp),
        grid=(t // tm,),
        in_specs=[pl.BlockSpec((TOP_K, tm), lambda i: (0, i))],
        out_specs=[pl.BlockSpec((TOP_K, tm), lambda i: (0, i)), pl.BlockSpec((n_exp, LANES), lambda i: (0, 0))],
        out_shape=[jax.ShapeDtypeStruct((TOP_K, t), jnp.int32), jax.ShapeDtypeStruct((n_exp, LANES), F32)],
        scratch_shapes=[pltpu.VMEM((n_exp, LANES), F32)],
        compiler_params=_params("arbitrary"),
        name="expert_ranks",
    )(idx_t)


def _pos_kernel(idx_ref, rank_ref, start_ref, pos_ref, *, n_exp):
    tm = idx_ref.shape[1]
    eiota = lax.broadcasted_iota(jnp.int32, (n_exp, tm), 0)
    idx = idx_ref[...]
    start = start_ref[:, 0:1]
    offs = [jnp.sum(jnp.where(eiota == idx[k:k + 1, :], start, 0.0), axis=0, keepdims=True) for k in range(TOP_K)]
    pos_ref[...] = rank_ref[...] + jnp.concatenate(offs, axis=0).astype(jnp.int32)


def _positions(idx_t, rank_t, pad_start, tm):
    t = idx_t.shape[1]
    n_exp = pad_start.shape[0]
    start = jnp.broadcast_to(pad_start.astype(F32)[:, None], (n_exp, LANES))
    spec = pl.BlockSpec((TOP_K, tm), lambda i: (0, i))
    return pl.pallas_call(
        functools.partial(_pos_kernel, n_exp=n_exp),
        grid=(t // tm,),
        in_specs=[spec, spec, pl.BlockSpec((n_exp, LANES), lambda i: (0, 0))],
        out_specs=spec,
        out_shape=jax.ShapeDtypeStruct((TOP_K, t), jnp.int32),
        compiler_params=_params("parallel"),
        name="positions",
    )(idx_t, rank_t, start)


def _experts_kernel(first_ref, count_ref, used_ref, x_hbm, wg_ref, wu_ref, wd_ref, o_hbm,
                    wg_scr, wu_scr, wd_scr, x_vmem, o_vmem, x_sem, o_sem):
    e = pl.program_id(0)
    nbuf, blk = x_vmem.shape[0], x_vmem.shape[1]
    first, count, used = first_ref[e], count_ref[e], used_ref[0]

    def block_rows(g):
        return pl.ds(pl.multiple_of(g * blk, blk), blk)

    def x_copy(g):
        slot = lax.rem(g, nbuf)
        return pltpu.make_async_copy(x_hbm.at[block_rows(g), :], x_vmem.at[slot], x_sem.at[slot])

    def o_copy(g):
        slot = lax.rem(g, nbuf)
        return pltpu.make_async_copy(o_vmem.at[slot], o_hbm.at[block_rows(g), :], o_sem.at[slot])

    @pl.when(e == 0)
    def _():
        for g in range(nbuf - 1):
            @pl.when(g < used)
            def _():
                x_copy(g).start()

    @pl.when(count > 0)
    def _():
        wg_scr[...] = wg_ref[...].astype(BF16)
        wu_scr[...] = wu_ref[...].astype(BF16)
        wd_scr[...] = wd_ref[...].astype(BF16)

    def one_block(j, carry):
        g = first + j
        slot = lax.rem(g, nbuf)
        x_copy(g).wait()

        @pl.when(g + nbuf - 1 < used)
        def _():
            x_copy(g + nbuf - 1).start()

        xb = _unpack_pairs(x_vmem[slot]).astype(BF16)
        act = _silu(jnp.dot(xb, wg_scr[...], preferred_element_type=F32)) * jnp.dot(
            xb, wu_scr[...], preferred_element_type=F32)
        res = _pack_pairs(jnp.dot(act.astype(BF16), wd_scr[...], preferred_element_type=F32))

        @pl.when(g >= nbuf)
        def _():
            o_copy(g - nbuf).wait()

        o_vmem[slot] = res
        o_copy(g).start()
        return carry

    lax.fori_loop(0, count, one_block, 0)

    @pl.when(e == pl.num_programs(0) - 1)
    def _():
        for back in range(nbuf, 0, -1):
            @pl.when(used >= back)
            def _():
                o_copy(used - back).wait()


def _sc_dispatch(h, pos_flat, rows):
    t, w = h.shape
    chunk, nbuf = 48, 2
    per_w = t // SC_WORKERS
    assert t % SC_WORKERS == 0 and per_w % (chunk * nbuf) == 0
    n = per_w // chunk
    mesh = plsc.VectorSubcoreMesh(core_axis_name="c", subcore_axis_name="s")

    @functools.partial(pl.kernel, mesh=mesh, out_type=jax.ShapeDtypeStruct((rows, w), h.dtype),
                       scratch_types=[pltpu.VMEM((nbuf, TOP_K, chunk), jnp.int32),
                                      pltpu.VMEM((nbuf, chunk, w), h.dtype),
                                      pltpu.SemaphoreType.DMA((nbuf,))])
    def scatter_rows(h_hbm, pos_hbm, x_hbm, idx_v, rows_v, sem):
        w0 = (lax.axis_index("s") * SC_CORES + lax.axis_index("c")) * per_w

        def scatter(slot, kk):
            return pltpu.make_async_copy(rows_v.at[slot], x_hbm.at[idx_v.at[slot, kk]], sem.at[slot])

        @pl.loop(0, n // nbuf)
        def _(g):
            for s in range(nbuf):
                @pl.when(g > 0)
                def _():
                    for kk in range(TOP_K):
                        scatter(s, kk).wait()
                base = pl.multiple_of(w0 + (g * nbuf + s) * chunk, SUBLANES)
                pltpu.sync_copy(h_hbm.at[pl.ds(base, chunk)], rows_v.at[s])
                for kk in range(TOP_K):
                    pltpu.sync_copy(pos_hbm.at[pl.ds(pl.multiple_of(kk * t + base, SUBLANES), chunk)],
                                    idx_v.at[s, kk])
                for kk in range(TOP_K):
                    scatter(s, kk).start()

        for s in range(nbuf):
            for kk in range(TOP_K):
                scatter(s, kk).wait()

    return scatter_rows(h, pos_flat)


def _sc_gather(table, idx_flat):
    n_idx = idx_flat.shape[0]
    w = table.shape[1]
    chunk, nbuf = 48, 4
    per_w = n_idx // SC_WORKERS
    assert n_idx % SC_WORKERS == 0 and per_w % (chunk * nbuf) == 0
    n = per_w // chunk
    mesh = plsc.VectorSubcoreMesh(core_axis_name="c", subcore_axis_name="s")

    @functools.partial(pl.kernel, mesh=mesh, out_type=jax.ShapeDtypeStruct((n_idx, w), table.dtype),
                       scratch_types=[pltpu.VMEM((nbuf, chunk), jnp.int32),
                                      pltpu.VMEM((nbuf, chunk, w), table.dtype),
                                      pltpu.SemaphoreType.DMA((nbuf,))])
    def gather_rows(t_hbm, idx_hbm, o_hbm, idx_v, rows_v, sem):
        w0 = (lax.axis_index("s") * SC_CORES + lax.axis_index("c")) * per_w

        def gather(slot):
            return pltpu.make_async_copy(t_hbm.at[idx_v.at[slot]], rows_v.at[slot], sem.at[slot])

        def issue(j, slot):
            base = pl.multiple_of(w0 + j * chunk, SUBLANES)
            pltpu.sync_copy(idx_hbm.at[pl.ds(base, chunk)], idx_v.at[slot])
            gather(slot).start()

        for s in range(nbuf):
            issue(s, s)

        @pl.loop(0, n // nbuf)
        def _(g):
            for s in range(nbuf):
                j = g * nbuf + s
                gather(s).wait()
                pltpu.sync_copy(rows_v.at[s], o_hbm.at[pl.ds(pl.multiple_of(w0 + j * chunk, SUBLANES), chunk)])

                @pl.when(j + nbuf < n)
                def _():
                    issue(j + nbuf, s)

    return gather_rows(table, idx_flat)


def _routed_experts(h, idx_t, wg, wu, wd):
    kk, t = idx_t.shape
    dp = h.shape[1]
    n_exp, d, de = wg.shape
    blk = EXPERT_ROWS
    n_assign = t * kk
    tm = 512
    assert t % tm == 0
    rank_t, cnt = _expert_ranks(idx_t, n_exp, tm)
    counts = cnt[:, 0].astype(jnp.int32)
    padded = (counts + blk - 1) // blk * blk
    pad_end = jnp.cumsum(padded)
    pad_start = pad_end - padded
    pos_flat = _positions(idx_t, rank_t, pad_start, tm).reshape(-1)
    n_blocks = (n_assign + n_exp * (blk - 1)) // blk
    rows = n_blocks * blk
    first_blk = (pad_start // blk).astype(jnp.int32)
    count_blk = (padded // blk).astype(jnp.int32)
    n_used = (pad_end[-1] // blk).astype(jnp.int32).reshape(1)
    x_buf = _sc_dispatch(h, pos_flat, rows)
    w_in_spec = pl.BlockSpec((None, d, de), lambda e, *_: (e, 0, 0))
    grid_spec = pltpu.PrefetchScalarGridSpec(
        num_scalar_prefetch=3,
        grid=(n_exp,),
        in_specs=[pl.BlockSpec(memory_space=pl.ANY), w_in_spec, w_in_spec,
                  pl.BlockSpec((None, de, d), lambda e, *_: (e, 0, 0))],
        out_specs=pl.BlockSpec(memory_space=pl.ANY),
        scratch_shapes=[pltpu.VMEM((d, de), BF16), pltpu.VMEM((d, de), BF16), pltpu.VMEM((de, d), BF16),
                        pltpu.VMEM((EXPERT_BUFFERS, blk, dp), jnp.uint32),
                        pltpu.VMEM((EXPERT_BUFFERS, blk, dp), jnp.uint32),
                        pltpu.SemaphoreType.DMA((EXPERT_BUFFERS,)), pltpu.SemaphoreType.DMA((EXPERT_BUFFERS,))],
    )
    out = pl.pallas_call(
        _experts_kernel,
        grid_spec=grid_spec,
        out_shape=jax.ShapeDtypeStruct((rows, dp), jnp.uint32),
        compiler_params=_params("arbitrary"),
        name="experts",
    )(first_blk, count_blk, n_used, x_buf, wg, wu, wd)
    return _sc_gather(out, pos_flat).reshape(kk, t, dp)


def _ffn_post_kernel(h_ref, r_ref, wt_ref, x_ref, gt_ref, wg_ref, wu_ref, wd_ref, g_ref, o_ref):
    h = _unpack_pairs(h_ref[...]).astype(BF16)
    act = _silu(jnp.dot(h, wg_ref[...], preferred_element_type=F32)) * jnp.dot(
        h, wu_ref[...], preferred_element_type=F32)
    f = jnp.dot(act.astype(BF16), wd_ref[...], preferred_element_type=F32)
    for k in range(TOP_K):
        f = f + _unpack_pairs(r_ref[k]) * wt_ref[:, k:k + 1]
    o_ref[...] = x_ref[...] + gt_ref[...] * _rms_rows(f, g_ref[...])


def _ffn_post(h2, routed, w_rows, row0, x1, mod, wg, wu, wd, g_post, tm):
    t, d = x1.shape
    assert row0 % tm == 0
    blk0 = row0 // tm
    full = lambda a: pl.BlockSpec(a.shape, lambda i: (0, 0))
    rows = lambda a: pl.BlockSpec((tm, a.shape[1]), lambda i: (i, 0))
    g2 = g_post.reshape(1, d)
    return pl.pallas_call(
        _ffn_post_kernel,
        grid=(t // tm,),
        in_specs=[rows(h2),
                  pl.BlockSpec((TOP_K, tm, d // 2), lambda i: (0, i + blk0, 0)),
                  pl.BlockSpec((tm, TOP_K), lambda i: (i + blk0, 0)),
                  rows(x1), mod.spec(5), full(wg), full(wu), full(wd), full(g2)],
        out_specs=pl.BlockSpec((tm, d), lambda i: (i, 0)),
        out_shape=jax.ShapeDtypeStruct((t, d), F32),
        compiler_params=_params("parallel"),
        name="ffn_post",
    )(h2, routed, w_rows, x1, mod.arr, wg, wu, wd, g2)


def _channel_mixer(x1s, mods, tms, lw):
    pre = [_ffn_pre(x1, lw['g_pre_ffn'], mod, lw['w_router_t'], lw['b_router'], tm)
           for x1, mod, tm in zip(x1s, mods, tms)]
    h2 = jnp.concatenate([p[0] for p in pre], axis=0)
    idx_t = jnp.concatenate([p[1] for p in pre], axis=1)
    w_rows = jnp.concatenate([p[2] for p in pre], axis=1).T
    routed = _routed_experts(h2, idx_t, lw['w_exp_gate'], lw['w_exp_up'], lw['w_exp_down'])
    outs, row0 = [], 0
    for x1, mod, tm, p in zip(x1s, mods, tms, pre):
        outs.append(_ffn_post(p[0], routed, w_rows, row0, x1, mod, lw['w_sh_gate'], lw['w_sh_up'],
                              lw['w_sh_down'], lw['g_post_ffn'], tm))
        row0 += x1.shape[0]
    return outs


def _token_mixer(x, mod_rows, attend, conv_buf, h0, lw, tm, ssm_dtype):
    b, seq, d = x.shape
    t = b * seq
    x2 = x.reshape(t, d)
    mod = _Mod(mod_rows, seq, tm, d)
    h = _prenorm(x2, lw['g_pre_mix'], mod, 1, 0, tm)
    proj = {}
    for call, names in enumerate(IN_PROJ_CALLS):
        outs = _matmuls(h, [lw['w_in'][n] for n in names], tm, "proj_%d" % call)
        proj.update(zip(names, outs))
    q = proj['q'].reshape(b, seq, -1)
    k = proj['k'].reshape(b, seq, -1)
    v = proj['v'].reshape(b, seq, -1)
    o_att, new_k, new_v = attend(q, k, v)
    xbc = proj['xbc'].reshape(b, seq, -1)
    y_ssm, h_last = _ssd_branch(xbc, proj['z'].reshape(b, seq, -1), proj['dt'].reshape(b, seq, -1),
                                conv_buf, h0, lw['conv_w'], lw['conv_b'], lw['dt_bias'], lw['a_log'],
                                lw['d_skip'], lw['g_ssm_norm'], ssm_dtype)
    new_conv = jnp.concatenate([conv_buf, xbc], axis=1)[:, -(D_CONV - 1):] if seq < D_CONV - 1 \
        else xbc[:, seq - (D_CONV - 1):]
    x1 = _mix(o_att.reshape(t, -1), y_ssm.reshape(t, -1), proj['gate'], x2, mod,
              lw['w_branch_a'], lw['w_branch_b'], lw['w_out'], lw['g_post_mix'], tm)
    return x1, mod, new_k, new_v, new_conv, h_last


def _split_in_proj(w_in, d, d_inner, conv_dim, heads):
    sizes = (N_DIL * ATT_WIDTH, ATT_WIDTH, ATT_WIDTH, d_inner, conv_dim, heads, d, d)
    offs = [0]
    for s in sizes:
        offs.append(offs[-1] + s)
    part = lambda i, j=None: w_in[:, offs[i]:offs[(i if j is None else j) + 1]].astype(BF16)
    w_dt = jnp.pad(part(5), ((0, 0), (0, LANES - heads)))
    return {'q': part(0), 'k': part(1), 'v': part(2), 'z': part(3), 'xbc': part(4), 'dt': w_dt, 'gate': part(6, 7)}


def kernel(x_prompt, x_sample, cache_win_k, cache_win_v, state_conv, state_ssm, c_prompt, c_sample, rel_bias, w_mod, b_mod, g_pre_mix, g_post_mix, g_pre_ffn, g_post_ffn, w_in, conv_w, conv_b, dt_bias, a_log, d_skip, g_ssm_norm, w_branch_a, w_branch_b, w_out, w_router, b_router, w_exp_gate, w_exp_up, w_exp_down, w_sh_gate, w_sh_up, w_sh_down):
    depth = w_mod.shape[0]
    bp, sp, d = x_prompt.shape
    bs, ss, _ = x_sample.shape
    d_inner = g_ssm_norm.shape[1]
    conv_dim = conv_w.shape[2]
    heads = dt_bias.shape[1]
    y_p, y_s = x_prompt, x_sample
    outs = [[] for _ in range(8)]
    for l in range(depth):
        lw = {
            'g_pre_mix': g_pre_mix[l], 'g_post_mix': g_post_mix[l],
            'g_pre_ffn': g_pre_ffn[l], 'g_post_ffn': g_post_ffn[l],
            'w_in': _split_in_proj(w_in[l], d, d_inner, conv_dim, heads),
            'conv_w': conv_w[l], 'conv_b': conv_b[l],
            'dt_bias': dt_bias[l], 'a_log': a_log[l], 'd_skip': d_skip[l], 'g_ssm_norm': g_ssm_norm[l],
            'w_branch_a': w_branch_a[l].astype(BF16), 'w_branch_b': w_branch_b[l].astype(BF16),
            'w_out': w_out[l].astype(BF16),
            'w_router_t': w_router[l].T.astype(BF16), 'b_router': b_router[l],
            'w_exp_gate': w_exp_gate[l], 'w_exp_up': w_exp_up[l], 'w_exp_down': w_exp_down[l],
            'w_sh_gate': w_sh_gate[l].astype(BF16), 'w_sh_up': w_sh_up[l].astype(BF16),
            'w_sh_down': w_sh_down[l].astype(BF16),
        }
        mod = _modulation(jnp.concatenate([c_prompt, c_sample], axis=0), w_mod[l], b_mod[l])
        conv0 = jnp.zeros((bp, D_CONV - 1, conv_dim), F32)
        h0 = jnp.zeros((bp, heads, SSM_HEAD_DIM, D_STATE), F32)
        tms = (512, 256)
        x1_p, mod_p, *state_p = _token_mixer(
            y_p, mod[:bp], functools.partial(_prompt_attend, rel_bias=rel_bias), conv0, h0, lw, tms[0], BF16)
        x1_s, mod_s, *state_s = _token_mixer(
            y_s, mod[bp:],
            functools.partial(_sample_attend, buf_k=cache_win_k[l], buf_v=cache_win_v[l], rel_bias=rel_bias),
            state_conv[l], state_ssm[l], lw, tms[1], F32)
        for o, val in zip(outs, state_p + state_s):
            o.append(val)
        y_p, y_s = _channel_mixer((x1_p, x1_s), (mod_p, mod_s), tms, lw)
        y_p = y_p.reshape(bp, sp, d)
        y_s = y_s.reshape(bs, ss, d)
    return (y_p, y_s) + tuple(jnp.stack(o) for o in outs)


def _prompt_attend(q, k, v, rel_bias):
    b, seq, _ = k.shape
    o = _attn_prompt(q, k, v, rel_bias)
    n_keep = min(MAX_WINDOW, seq)
    shape = (b, n_keep, ATT_HEADS, HEAD_DIM)
    return o, k[:, seq - n_keep:].reshape(shape), v[:, seq - n_keep:].reshape(shape)


def _sample_attend(q, k, v, buf_k, buf_v, rel_bias):
    return _attn_sample(q, k, v, buf_k, buf_v, rel_bias)
```
